```python
import math
import jax, jax.numpy as jnp
from jax import lax
import numpy as np

D_MODEL = 1024
BATCH = 8
SEQ = 4096
DEPTH = 4

GRID_W = 64
CTX_LEN = 256
N_MIXERS = 3
N_LAYERS_A = (DEPTH + 2) // 3
N_LAYERS_B = (DEPTH + 1) // 3
N_LAYERS_C = DEPTH // 3
Q_BLOCK = 128
ROPE_BASE = 10000.0
EPS = 1e-6

MLA_HEADS = 16
MLA_Q_LORA = 512
MLA_KV_LORA = 256
MLA_NOPE = 64
MLA_ROPE = 32
MLA_V = 64
MLA_QK = MLA_NOPE + MLA_ROPE
MLA_DOWN = MLA_Q_LORA + MLA_KV_LORA + MLA_ROPE

DIFF_HEADS = 8
DIFF_HEAD_DIM = 64
DIFF_V_DIM = 2 * DIFF_HEAD_DIM

SG_CHUNK = 128
SG_WIDTH = 2 * D_MODEL
SG_GROUPS = 8
SG_GROUP_DIM = SG_WIDTH // SG_GROUPS

N_EXPERTS = 16
EXPERT_FF = D_MODEL
CAPACITY_FACTOR = 2

kernel_name = "hybrid_mla_diff_chunkmlp_ec_moe_dit"


def rms_norm(x, g):
    xf = x.astype(jnp.float32)
    y = xf * lax.rsqrt(jnp.mean(xf * xf, axis=-1, keepdims=True) + EPS)
    return (y * g).astype(x.dtype)


def layer_norm(x, g, b):
    xf = x.astype(jnp.float32)
    mu = jnp.mean(xf, axis=-1, keepdims=True)
    var = jnp.mean(jnp.square(xf - mu), axis=-1, keepdims=True)
    return ((xf - mu) * lax.rsqrt(var + EPS) * g + b).astype(x.dtype)


def modulate(h, shift, scale):
    return h * (1 + scale) + shift


def axial_rope(n_tokens, rot_dim):
    n_rows = n_tokens // GRID_W
    rows = jnp.repeat(jnp.arange(n_rows, dtype=jnp.float32), GRID_W)
    cols = jnp.tile(jnp.arange(GRID_W, dtype=jnp.float32), n_rows)
    n_freq = rot_dim // 4
    inv_freq = ROPE_BASE ** (-jnp.arange(n_freq, dtype=jnp.float32) / n_freq)
    ang = jnp.concatenate([rows[:, None] * inv_freq, cols[:, None] * inv_freq], axis=-1)
    return jnp.cos(ang), jnp.sin(ang)


def apply_rope(x, cos, sin):
    half = x.shape[-1] // 2
    x1, x2 = x[..., :half], x[..., half:]
    return jnp.concatenate([x1 * cos - x2 * sin, x1 * sin + x2 * cos], axis=-1)


def merge_heads(o):
    b, h, n, d = o.shape
    return o.transpose(0, 2, 1, 3).reshape(b, n, h * d)


def block_softmax_attention(q, k, v):
    b, h, n, dk = q.shape
    nb = n // Q_BLOCK
    scale = dk ** -0.5
    qb = jnp.moveaxis(q.reshape(b, h, nb, Q_BLOCK, dk), 2, 0)

    def one(qi):
        p = jax.nn.softmax(jnp.einsum('bhqd,bhkd->bhqk', qi, k) * scale, axis=-1)
        return jnp.einsum('bhqk,bhkd->bhqd', p.astype(v.dtype), v)

    out = lax.map(one, qb)
    return jnp.moveaxis(out, 0, 2).reshape(b, h, n, -1)


def block_diff_attention(q, k, v, lam):
    b, h, _, n, dk = q.shape
    nb = n // Q_BLOCK
    scale = dk ** -0.5
    qb = jnp.moveaxis(q.reshape(b, h, 2, nb, Q_BLOCK, dk), 3, 0)

    def one(qi):
        p = jax.nn.softmax(jnp.einsum('bhcqd,bhckd->bhcqk', qi, k) * scale, axis=-1)
        a = p[:, :, 0] - lam * p[:, :, 1]
        return jnp.einsum('bhqk,bhkd->bhqd', a.astype(v.dtype), v)

    out = lax.map(one, qb)
    return jnp.moveaxis(out, 0, 2).reshape(b, h, n, -1)


def mla_project_q(down, q_norm_g, w_uq, qn_g):
    b, n, _ = down.shape
    cq = rms_norm(down[..., :MLA_Q_LORA], q_norm_g)
    q = (cq @ w_uq).reshape(b, n, MLA_HEADS, MLA_QK)
    return rms_norm(q, qn_g).astype(jnp.float32).transpose(0, 2, 1, 3)


def mla_project_kv(down, kv_norm_g, w_ukv, kn_g):
    b, n, _ = down.shape
    ckv = rms_norm(down[..., MLA_Q_LORA:MLA_Q_LORA + MLA_KV_LORA], kv_norm_g)
    k_rope = down[..., MLA_Q_LORA + MLA_KV_LORA:]
    kv = (ckv @ w_ukv).reshape(b, n, MLA_HEADS, MLA_NOPE + MLA_V)
    k_nope, v = kv[..., :MLA_NOPE], kv[..., MLA_NOPE:]
    k = jnp.concatenate([k_nope, jnp.broadcast_to(k_rope[:, :, None, :], (b, n, MLA_HEADS, MLA_ROPE))], axis=-1)
    k = rms_norm(k, kn_g).astype(jnp.float32)
    return k.transpose(0, 2, 1, 3), v.transpose(0, 2, 1, 3)


def rope_tail(t, cos, sin):
    return jnp.concatenate([t[..., :MLA_NOPE], apply_rope(t[..., MLA_NOPE:], cos, sin)], axis=-1)


def mla_mixer(h_lat, h_ctx, w_in, q_norm_g, w_uq, kv_norm_g, w_ukv, qn_g, kn_g, w_out, cos, sin, need_ctx_out):
    d_lat = h_lat @ w_in
    d_ctx = h_ctx @ w_in
    q_lat = rope_tail(mla_project_q(d_lat, q_norm_g, w_uq, qn_g), cos, sin)
    k_lat, v_lat = mla_project_kv(d_lat, kv_norm_g, w_ukv, kn_g)
    k_lat = rope_tail(k_lat, cos, sin)
    k_ctx, v_ctx = mla_project_kv(d_ctx, kv_norm_g, w_ukv, kn_g)
    k_all = jnp.concatenate([k_ctx, k_lat], axis=2)
    v_all = jnp.concatenate([v_ctx, v_lat], axis=2)
    y_lat = merge_heads(block_softmax_attention(q_lat, k_all, v_all)) @ w_out
    y_ctx = None
    if need_ctx_out:
        q_ctx = mla_project_q(d_ctx, q_norm_g, w_uq, qn_g)
        y_ctx = merge_heads(block_softmax_attention(q_ctx, k_ctx, v_ctx)) @ w_out
    return y_lat, y_ctx


def diff_heads_qk(t, g):
    b, n, _ = t.shape
    t = rms_norm(t.reshape(b, n, DIFF_HEADS, 2, DIFF_HEAD_DIM), g)
    return t.astype(jnp.float32).transpose(0, 2, 3, 1, 4)


def diff_heads_v(t):
    b, n, _ = t.shape
    return t.reshape(b, n, DIFF_HEADS, DIFF_V_DIM).transpose(0, 2, 1, 3)


def diff_mixer(h_lat, h_ctx, w_in, qn_g, kn_g, lq1, lk1, lq2, lk2, sub_g, w_out, lam_init, cos, sin, need_ctx_out):
    D = D_MODEL
    qkv_lat = h_lat @ w_in
    q_lat = apply_rope(diff_heads_qk(qkv_lat[..., :D], qn_g), cos, sin)
    k_lat = apply_rope(diff_heads_qk(qkv_lat[..., D:2 * D], kn_g), cos, sin)
    v_lat = diff_heads_v(qkv_lat[..., 2 * D:])
    if need_ctx_out:
        qkv_ctx = h_ctx @ w_in
        kv_ctx = qkv_ctx[..., D:]
    else:
        kv_ctx = h_ctx @ w_in[:, D:]
    k_ctx = diff_heads_qk(kv_ctx[..., :D], kn_g)
    v_ctx = diff_heads_v(kv_ctx[..., D:])
    lam = (jnp.exp(jnp.sum(lq1.astype(jnp.float32) * lk1.astype(jnp.float32)))
           - jnp.exp(jnp.sum(lq2.astype(jnp.float32) * lk2.astype(jnp.float32))) + lam_init)

    def finish(o):
        return merge_heads(rms_norm(o, sub_g) * (1.0 - lam_init)) @ w_out

    k_all = jnp.concatenate([k_ctx, k_lat], axis=3)
    v_all = jnp.concatenate([v_ctx, v_lat], axis=2)
    y_lat = finish(block_diff_attention(q_lat, k_all, v_all, lam))
    y_ctx = None
    if need_ctx_out:
        q_ctx = diff_heads_qk(qkv_ctx[..., :D], qn_g)
        y_ctx = finish(block_diff_attention(q_ctx, k_ctx, v_ctx, lam))
    return y_lat, y_ctx


def chunk_gate(h, w_in, ln_g, ln_b, w_s, b_s, w_out):
    b, n, _ = h.shape
    z = jax.nn.gelu(h @ w_in)
    u, v = z[..., :SG_WIDTH], z[..., SG_WIDTH:]
    v = layer_norm(v, ln_g, ln_b).reshape(b, n // SG_CHUNK, SG_CHUNK, SG_GROUPS, SG_GROUP_DIM)
    v = jnp.einsum('gpq,bnqgc->bnpgc', w_s, v) + b_s.T[None, None, :, :, None]
    return (u * v.reshape(b, n, SG_WIDTH)) @ w_out


def chunk_mixer(h_lat, h_ctx, w_in, ln_g, ln_b, w_s, b_s, w_out, need_ctx_out):
    y_lat = chunk_gate(h_lat, w_in, ln_g, ln_b, w_s, b_s, w_out)
    y_ctx = chunk_gate(h_ctx, w_in, ln_g, ln_b, w_s, b_s, w_out) if need_ctx_out else None
    return y_lat, y_ctx


def ec_moe(h, router, w_gate, w_up, w_down):
    b, n, d = h.shape
    cap = CAPACITY_FACTOR * n // N_EXPERTS
    aff = jax.nn.softmax(jnp.einsum('bnd,de->bne', h, router).astype(jnp.float32), axis=-1)
    gates, idx = lax.top_k(aff.transpose(0, 2, 1), cap)
    xg = jax.vmap(lambda hb, ib: hb[ib])(h, idx)
    hid = jax.nn.silu(jnp.einsum('becd,edf->becf', xg, w_gate)) * jnp.einsum('becd,edf->becf', xg, w_up)
    y = jnp.einsum('becf,efd->becd', hid, w_down) * gates.astype(h.dtype)[..., None]
    return jax.vmap(lambda yb, ib: jnp.zeros((n, d), yb.dtype).at[ib.reshape(-1)].add(yb.reshape(-1, d)))(y, idx)


def setup_inputs(seed: int = 0) -> dict:
    key = jax.random.key(seed)
    ks = iter(jax.random.split(key, 64))
    D = D_MODEL

    def nrm(shape, scale=1.0):
        return jax.random.normal(next(ks), shape, jnp.float32) * scale

    def gain(shape):
        return 1.0 + nrm(shape, 0.05)

    return {
        "x": nrm((BATCH, SEQ, D)),
        "c": nrm((BATCH, D)),
        "ctx": nrm((BATCH, CTX_LEN, D)),
        "c_ctx": nrm((D,)),
        "ada_w": nrm((DEPTH, D, 6 * D), 0.5 * D ** -0.5),
        "ada_b": nrm((DEPTH, 6 * D), 0.02),
        "norm_mix_g": gain((DEPTH, D)),
        "norm_ffn_g": gain((DEPTH, D)),
        "mla_w_in": nrm((N_LAYERS_A, D, MLA_DOWN), D ** -0.5),
        "mla_q_norm_g": gain((N_LAYERS_A, MLA_Q_LORA)),
        "mla_w_uq": nrm((N_LAYERS_A, MLA_Q_LORA, MLA_HEADS * MLA_QK), MLA_Q_LORA ** -0.5),
        "mla_kv_norm_g": gain((N_LAYERS_A, MLA_KV_LORA)),
        "mla_w_ukv": nrm((N_LAYERS_A, MLA_KV_LORA, MLA_HEADS * (MLA_NOPE + MLA_V)), MLA_KV_LORA ** -0.5),
        "mla_qn_g": gain((N_LAYERS_A, MLA_QK)),
        "mla_kn_g": gain((N_LAYERS_A, MLA_QK)),
        "mla_w_out": nrm((N_LAYERS_A, MLA_HEADS * MLA_V, D), (MLA_HEADS * MLA_V) ** -0.5),
        "diff_w_in": nrm((N_LAYERS_B, D, 3 * D), D ** -0.5),
        "diff_qn_g": gain((N_LAYERS_B, DIFF_HEAD_DIM)),
        "diff_kn_g": gain((N_LAYERS_B, DIFF_HEAD_DIM)),
        "diff_lambda_q1": nrm((N_LAYERS_B, DIFF_HEAD_DIM), 0.1),
        "diff_lambda_k1": nrm((N_LAYERS_B, DIFF_HEAD_DIM), 0.1),
        "diff_lambda_q2": nrm((N_LAYERS_B, DIFF_HEAD_DIM), 0.1),
        "diff_lambda_k2": nrm((N_LAYERS_B, DIFF_HEAD_DIM), 0.1),
        "diff_sub_g": gain((N_LAYERS_B, DIFF_V_DIM)),
        "diff_w_out": nrm((N_LAYERS_B, D, D), D ** -0.5),
        "sg_w_in": nrm((N_LAYERS_C, D, 2 * SG_WIDTH), D ** -0.5),
        "sg_ln_g": gain((N_LAYERS_C, SG_WIDTH)),
        "sg_ln_b": nrm((N_LAYERS_C, SG_WIDTH), 0.02),
        "sg_w_s": nrm((N_LAYERS_C, SG_GROUPS, SG_CHUNK, SG_CHUNK), SG_CHUNK ** -0.5),
        "sg_b_s": 1.0 + nrm((N_LAYERS_C, SG_GROUPS, SG_CHUNK), 0.05),
        "sg_w_out": nrm((N_LAYERS_C, SG_WIDTH, D), SG_WIDTH ** -0.5),
        "moe_router": nrm((DEPTH, D, N_EXPERTS), D ** -0.5),
        "moe_w_gate": nrm((DEPTH, N_EXPERTS, D, EXPERT_FF), D ** -0.5),
        "moe_w_up": nrm((DEPTH, N_EXPERTS, D, EXPERT_FF), D ** -0.5),
        "moe_w_down": nrm((DEPTH, N_EXPERTS, EXPERT_FF, D), EXPERT_FF ** -0.5),
    }


def reference(x, c, ctx, c_ctx, ada_w, ada_b, norm_mix_g, norm_ffn_g,
              mla_w_in, mla_q_norm_g, mla_w_uq, mla_kv_norm_g, mla_w_ukv, mla_qn_g, mla_kn_g, mla_w_out,
              diff_w_in, diff_qn_g, diff_kn_g, diff_lambda_q1, diff_lambda_k1, diff_lambda_q2, diff_lambda_k2,
              diff_sub_g, diff_w_out,
              sg_w_in, sg_ln_g, sg_ln_b, sg_w_s, sg_b_s, sg_w_out,
              moe_router, moe_w_gate, moe_w_up, moe_w_down):
    n_lat = x.shape[1]
    cos_a, sin_a = axial_rope(n_lat, MLA_ROPE)
    cos_b, sin_b = axial_rope(n_lat, DIFF_HEAD_DIM)
    x_lat, x_ctx = x, ctx
    silu_c = jax.nn.silu(c)
    silu_cc = jax.nn.silu(c_ctx)[None]
    for i in range(DEPTH):
        kind = i % N_MIXERS
        j = i // N_MIXERS
        last = i == DEPTH - 1
        need_ctx_out = not last
        need_ctx_in = not (last and kind == 2)
        sh1, sc1, g1, sh2, sc2, g2 = jnp.split((silu_c @ ada_w[i] + ada_b[i])[:, None, :], 6, axis=-1)
        h_lat = modulate(rms_norm(x_lat, norm_mix_g[i]), sh1, sc1)
        h_ctx = None
        if need_ctx_in:
            csh1, csc1, cg1, csh2, csc2, cg2 = jnp.split((silu_cc @ ada_w[i] + ada_b[i])[:, None, :], 6, axis=-1)
            h_ctx = modulate(rms_norm(x_ctx, norm_mix_g[i]), csh1, csc1)
        if kind == 0:
            y_lat, y_ctx = mla_mixer(h_lat, h_ctx, mla_w_in[j], mla_q_norm_g[j], mla_w_uq[j], mla_kv_norm_g[j],
                                     mla_w_ukv[j], mla_qn_g[j], mla_kn_g[j], mla_w_out[j], cos_a, sin_a, need_ctx_out)
        elif kind == 1:
            lam_init = 0.8 - 0.6 * math.exp(-0.3 * i)
            y_lat, y_ctx = diff_mixer(h_lat, h_ctx, diff_w_in[j], diff_qn_g[j], diff_kn_g[j],
                                      diff_lambda_q1[j], diff_lambda_k1[j], diff_lambda_q2[j], diff_lambda_k2[j],
                                      diff_sub_g[j], diff_w_out[j], lam_init, cos_b, sin_b, need_ctx_out)
        else:
            y_lat, y_ctx = chunk_mixer(h_lat, h_ctx, sg_w_in[j], sg_ln_g[j], sg_ln_b[j], sg_w_s[j], sg_b_s[j],
                                       sg_w_out[j], need_ctx_out)
        x_lat = x_lat + g1 * y_lat
        x_lat = x_lat + g2 * ec_moe(modulate(rms_norm(x_lat, norm_ffn_g[i]), sh2, sc2),
                                    moe_router[i], moe_w_gate[i], moe_w_up[i], moe_w_down[i])
        if need_ctx_out:
            x_ctx = x_ctx + cg1 * y_ctx
            x_ctx = x_ctx + cg2 * ec_moe(modulate(rms_norm(x_ctx, norm_ffn_g[i]), csh2, csc2),
                                        moe_router[i], moe_w_gate[i], moe_w_up[i], moe_w_down[i])
    return x_lat
```

```python
import functools
import math

import jax
import jax.numpy as jnp
from jax import lax
from jax.experimental import pallas as pl
from jax.experimental.pallas import tpu as pltpu

F32 = jnp.float32
BF16 = jnp.bfloat16

GRID_W = 64
ROPE_BASE = 10000.0
EPS = 1e-6
N_MIXERS = 3

MLA_HEADS = 16
MLA_Q_LORA = 512
MLA_KV_LORA = 256
MLA_NOPE = 64
MLA_ROPE = 32
MLA_V = 64
MLA_QK = MLA_NOPE + MLA_ROPE

DIFF_HEADS = 8
DIFF_HEAD_DIM = 64
DIFF_V_DIM = 2 * DIFF_HEAD_DIM

SG_CHUNK = 128
SG_GROUPS = 8

N_EXPERTS = 16
CAPACITY_FACTOR = 2

LANE = 128
ROW_TILE = 256
PREFIX_CHUNK = 256
GATHER_CHUNK = 1024
VMEM_LIMIT = 56 * 1024 * 1024


def _cparams(n_axes):
    return pltpu.CompilerParams(dimension_semantics=("arbitrary",) * n_axes, vmem_limit_bytes=VMEM_LIMIT)


def _const_spec(shape):
    nd = len(shape)
    return pl.BlockSpec(shape, lambda *_: (0,) * nd)


def _dot(a, b):
    return jnp.dot(a, b, preferred_element_type=F32)


def _dot_nt(a, b):
    return lax.dot_general(a, b, (((1,), (1,)), ((), ())), preferred_element_type=F32)


def _rms(x, n):
    return x * lax.rsqrt(jnp.sum(x * x, axis=-1, keepdims=True) * (1.0 / n) + EPS)


def _norm_mod(x, g, shift, scale):
    return _rms(x, x.shape[-1]) * g * (1.0 + scale) + shift


def _silu(x):
    return x / (1.0 + jnp.exp(-x))


def _gelu_tanh(x):
    c = math.sqrt(2.0 / math.pi)
    return 0.5 * x * (1.0 + jnp.tanh(c * (x + 0.044715 * (x * x * x))))


def _rope(x, cos, sin_a, sin_b, half):
    return x * cos + pltpu.roll(x, LANE - half, 1) * sin_a + pltpu.roll(x, half, 1) * sin_b


def _ada_kernel(c_ref, w_ref, b_ref, o_ref):
    s = _silu(c_ref[...])
    o_ref[0] = jnp.dot(s, w_ref[0], preferred_element_type=F32, precision=lax.Precision.HIGHEST) + b_ref[0]


def _ada(cc, ada_w, ada_b):
    depth, d, six_d = ada_w.shape
    rows = cc.shape[0]
    tn = 1536
    return pl.pallas_call(
        _ada_kernel,
        grid=(depth, six_d // tn),
        in_specs=[
            _const_spec((rows, d)),
            pl.BlockSpec((1, d, tn), lambda i, j: (i, 0, j)),
            pl.BlockSpec((1, 1, tn), lambda i, j: (i, 0, j)),
        ],
        out_specs=pl.BlockSpec((1, rows, tn), lambda i, j: (i, 0, j)),
        out_shape=jax.ShapeDtypeStruct((depth, rows, six_d), F32),
        compiler_params=_cparams(2),
        name="ada",
    )(cc, ada_w, ada_b.reshape(depth, 1, six_d))


def _mod_spec(n_lat_tiles, d):
    return pl.BlockSpec((1, 1, 6, d), lambda b, i: (b, jnp.minimum(i // n_lat_tiles, 1), 0, 0))


def _mla_in_kernel(x_ref, g_ref, mod_ref, win_ref, qng_ref, wuq_ref, kvng_ref, wuk_ref, wuv_ref, qn_ref, kn_ref,
                   cos_ref, sa_ref, sb_ref, q_ref, k_ref, v_ref):
    mod = mod_ref[0, 0]
    h = _norm_mod(x_ref[0], g_ref[...], mod[0:1], mod[1:2])
    down = _dot(h.astype(BF16), win_ref[...])
    cq = _rms(down[:, :MLA_Q_LORA], MLA_Q_LORA) * qng_ref[...]
    ckv = (_rms(down[:, MLA_Q_LORA:MLA_Q_LORA + MLA_KV_LORA], MLA_KV_LORA) * kvng_ref[...]).astype(BF16)
    q = _dot(cq.astype(BF16), wuq_ref[...])
    kn = _dot(ckv, wuk_ref[...])
    v_ref[0] = _dot(ckv, wuv_ref[...]).astype(BF16)
    kr = down[:, MLA_Q_LORA + MLA_KV_LORA:]
    cos, sa, sb = cos_ref[...], sa_ref[...], sb_ref[...]
    qn, kng = qn_ref[...], kn_ref[...]
    scale = MLA_QK ** -0.5
    for hd in range(MLA_HEADS):
        sl = slice(hd * LANE, (hd + 1) * LANE)
        qh = _rope(_rms(q[:, sl], MLA_QK) * qn, cos, sa, sb, MLA_ROPE // 2)
        q_ref[0, :, sl] = (qh * scale).astype(BF16)
        kh = _rope(_rms(kn[:, sl] + kr, MLA_QK) * kng, cos, sa, sb, MLA_ROPE // 2)
        k_ref[0, :, sl] = kh.astype(BF16)


def _mla_in(x, g, mod, w, tabs, n_lat):
    b, t, d = x.shape
    tm = ROW_TILE
    hw = MLA_HEADS * LANE
    row = lambda bb, i: (bb, i, 0)
    tab = pl.BlockSpec((tm, LANE), lambda bb, i: (i, 0))
    consts = [w["win"], w["qng"], w["wuq"], w["kvng"], w["wuk"], w["wuv"], w["qn"], w["kn"]]
    return pl.pallas_call(
        _mla_in_kernel,
        grid=(b, t // tm),
        in_specs=[pl.BlockSpec((1, tm, d), row), _const_spec((1, d)), _mod_spec(n_lat // tm, d)]
        + [_const_spec(a.shape) for a in consts] + [tab, tab, tab],
        out_specs=[pl.BlockSpec((1, tm, hw), row), pl.BlockSpec((1, tm, hw), row),
                   pl.BlockSpec((1, tm, MLA_HEADS * MLA_V), row)],
        out_shape=[jax.ShapeDtypeStruct((b, t, hw), BF16), jax.ShapeDtypeStruct((b, t, hw), BF16),
                   jax.ShapeDtypeStruct((b, t, MLA_HEADS * MLA_V), BF16)],
        compiler_params=_cparams(2),
        name="mla_in",
    )(x, g, mod, *consts, *tabs)


def _diff_in_kernel(x_ref, g_ref, mod_ref, win_ref, qn_ref, kn_ref, cos_ref, sa_ref, sb_ref,
                    q0_ref, q1_ref, k_ref, v_ref):
    mod = mod_ref[0, 0]
    d = x_ref.shape[-1]
    h = _norm_mod(x_ref[0], g_ref[...], mod[0:1], mod[1:2])
    qkv = _dot(h.astype(BF16), win_ref[...])
    v_ref[0] = qkv[:, 2 * d:].astype(BF16)
    cos, sa, sb = cos_ref[...], sa_ref[...], sb_ref[...]
    lo = lax.broadcasted_iota(jnp.int32, (x_ref.shape[1], LANE), 1) < DIFF_HEAD_DIM
    scale = DIFF_HEAD_DIM ** -0.5

    def norm2(xh, gains):
        sq = xh * xh
        s_lo = jnp.sum(jnp.where(lo, sq, 0.0), axis=-1, keepdims=True)
        s_hi = jnp.sum(jnp.where(lo, 0.0, sq), axis=-1, keepdims=True)
        ms = jnp.where(lo, s_lo, s_hi) * (1.0 / DIFF_HEAD_DIM)
        return xh * lax.rsqrt(ms + EPS) * gains

    for hd in range(DIFF_HEADS):
        sl = slice(hd * LANE, (hd + 1) * LANE)
        qh = _rope(norm2(qkv[:, sl], qn_ref[...]), cos, sa, sb, DIFF_HEAD_DIM // 2) * scale
        q0_ref[0, :, sl] = jnp.where(lo, qh, 0.0).astype(BF16)
        q1_ref[0, :, sl] = jnp.where(lo, 0.0, qh).astype(BF16)
        kh = _rope(norm2(qkv[:, d + hd * LANE:d + (hd + 1) * LANE], kn_ref[...]), cos, sa, sb, DIFF_HEAD_DIM // 2)
        k_ref[0, :, sl] = kh.astype(BF16)


def _diff_in(x, g, mod, w, tabs, n_lat):
    b, t, d = x.shape
    tm = ROW_TILE
    row = lambda bb, i: (bb, i, 0)
    tab = pl.BlockSpec((tm, LANE), lambda bb, i: (i, 0))
    consts = [w["win"], w["qn"], w["kn"]]
    out = jax.ShapeDtypeStruct((b, t, d), BF16)
    return pl.pallas_call(
        _diff_in_kernel,
        grid=(b, t // tm),
        in_specs=[pl.BlockSpec((1, tm, d), row), _const_spec((1, d)), _mod_spec(n_lat // tm, d)]
        + [_const_spec(a.shape) for a in consts] + [tab, tab, tab],
        out_specs=[pl.BlockSpec((1, tm, d), row)] * 4,
        out_shape=[out] * 4,
        compiler_params=_cparams(2),
        name="diff_in",
    )(x, g, mod, *consts, *tabs)


def _softmax_pv(q, k, v):
    s = _dot_nt(q, k)
    p = jnp.exp(s - jnp.max(s, axis=-1, keepdims=True))
    return _dot(p.astype(BF16), v) / jnp.sum(p, axis=-1, keepdims=True)


def _mla_attn_kernel(q_ref, k_ref, v_ref, o_ref, *, n_lat):
    first = lax.broadcasted_iota(jnp.int32, o_ref.shape[1:], 1) < MLA_V

    def run(ks):
        v = v_ref[0, ks, :]
        o0 = _softmax_pv(q_ref[0, :, :LANE], k_ref[0, ks, :LANE], v)
        o1 = _softmax_pv(q_ref[0, :, LANE:], k_ref[0, ks, LANE:], v)
        o_ref[0] = jnp.where(first, o0, o1).astype(o_ref.dtype)

    is_lat = pl.program_id(2) * q_ref.shape[1] < n_lat
    pl.when(is_lat)(lambda: run(slice(None)))
    pl.when(jnp.logical_not(is_lat))(lambda: run(slice(n_lat, None)))


def _mla_attn(q, k, v, n_lat, n_q_tiles):
    b, t, _ = q.shape
    tq = ROW_TILE
    return pl.pallas_call(
        functools.partial(_mla_attn_kernel, n_lat=n_lat),
        grid=(b, MLA_HEADS // 2, n_q_tiles),
        in_specs=[pl.BlockSpec((1, tq, 2 * LANE), lambda bb, hp, i: (bb, i, hp)),
                  pl.BlockSpec((1, t, 2 * LANE), lambda bb, hp, i: (bb, 0, hp)),
                  pl.BlockSpec((1, t, LANE), lambda bb, hp, i: (bb, 0, hp))],
        out_specs=pl.BlockSpec((1, tq, LANE), lambda bb, hp, i: (bb, i, hp)),
        out_shape=jax.ShapeDtypeStruct((b, t, MLA_HEADS * MLA_V), BF16),
        compiler_params=_cparams(3),
        name="mla_attn",
    )(q, k, v)


def _diff_attn_kernel(q0_ref, q1_ref, k_ref, v_ref, lq1_ref, lk1_ref, lq2_ref, lk2_ref, sub_ref, o_ref, *,
                      n_lat, lam_init):
    lam = (jnp.exp(jnp.sum(lq1_ref[...] * lk1_ref[...], axis=-1, keepdims=True))
           - jnp.exp(jnp.sum(lq2_ref[...] * lk2_ref[...], axis=-1, keepdims=True)) + lam_init)

    def run(ks):
        k, v = k_ref[0, ks, :], v_ref[0, ks, :]
        o = _softmax_pv(q0_ref[0], k, v) - lam * _softmax_pv(q1_ref[0], k, v)
        o_ref[0] = (_rms(o, DIFF_V_DIM) * sub_ref[...] * (1.0 - lam_init)).astype(o_ref.dtype)

    is_lat = pl.program_id(2) * q0_ref.shape[1] < n_lat
    pl.when(is_lat)(lambda: run(slice(None)))
    pl.when(jnp.logical_not(is_lat))(lambda: run(slice(n_lat, None)))


def _diff_attn(q0, q1, k, v, w, n_lat, n_q_tiles, lam_init):
    b, t, d = k.shape
    tq = ROW_TILE
    kv = pl.BlockSpec((1, t, LANE), lambda bb, hd, i: (bb, 0, hd))
    qo = pl.BlockSpec((1, tq, LANE), lambda bb, hd, i: (bb, i, hd))
    consts = [w["lq1"], w["lk1"], w["lq2"], w["lk2"], w["sub"]]
    return pl.pallas_call(
        functools.partial(_diff_attn_kernel, n_lat=n_lat, lam_init=lam_init),
        grid=(b, DIFF_HEADS, n_q_tiles),
        in_specs=[qo, qo, kv, kv] + [_const_spec(a.shape) for a in consts],
        out_specs=qo,
        out_shape=jax.ShapeDtypeStruct((b, t, d), BF16),
        compiler_params=_cparams(3),
        name="diff_attn",
    )(q0, q1, k, v, *consts)


def _out_proj_kernel(x_ref, a_ref, w_ref, mod_ref, o_ref):
    gate = mod_ref[0, 0][2:3]
    o_ref[0] = x_ref[0] + gate * _dot(a_ref[0], w_ref[...])


def _out_proj(x, a, w, mod, n_lat, n_tiles):
    b, t, d = x.shape
    tm = ROW_TILE
    row = lambda bb, i: (bb, i, 0)
    return pl.pallas_call(
        _out_proj_kernel,
        grid=(b, n_tiles),
        in_specs=[pl.BlockSpec((1, tm, d), row), pl.BlockSpec((1, tm, a.shape[-1]), row), _const_spec(w.shape),
                  _mod_spec(n_lat // tm, d)],
        out_specs=pl.BlockSpec((1, tm, d), row),
        out_shape=jax.ShapeDtypeStruct(x.shape, F32),
        input_output_aliases={0: 0},
        compiler_params=_cparams(2),
        name="out_proj",
    )(x, a, w, mod)


def _sg_kernel(x_ref, g_ref, mod_ref, win_ref, lng_ref, lnb_ref, ws_ref, bs_ref, wout_ref, o_ref, gated_ref):
    mod = mod_ref[0, 0]
    x = x_ref[0]
    tm = x.shape[0]
    width = lng_ref.shape[-1]
    gdim = width // SG_GROUPS
    h = _norm_mod(x, g_ref[...], mod[0:1], mod[1:2])
    z = _gelu_tanh(_dot(h.astype(BF16), win_ref[...]))
    u, v = z[:, :width], z[:, width:]
    mu = jnp.mean(v, axis=-1, keepdims=True)
    vc = v - mu
    var = jnp.mean(vc * vc, axis=-1, keepdims=True)
    vn = (vc * lax.rsqrt(var + EPS) * lng_ref[...] + lnb_ref[...]).astype(BF16)
    for c in range(tm // SG_CHUNK):
        rows = slice(c * SG_CHUNK, (c + 1) * SG_CHUNK)
        for gi in range(SG_GROUPS):
            cols = slice(gi * gdim, (gi + 1) * gdim)
            mixed = _dot(ws_ref[gi], vn[rows, cols]) + bs_ref[:, gi:gi + 1]
            gated_ref[rows, cols] = (u[rows, cols] * mixed).astype(BF16)
    o_ref[0] = x + mod[2:3] * _dot(gated_ref[...], wout_ref[...])


def _sg(x, g, mod, w, n_lat):
    b, t, d = x.shape
    tm = ROW_TILE
    row = lambda bb, i: (bb, i, 0)
    consts = [w["win"], w["lng"], w["lnb"], w["ws"], w["bs"], w["wout"]]
    return pl.pallas_call(
        _sg_kernel,
        grid=(b, t // tm),
        in_specs=[pl.BlockSpec((1, tm, d), row), _const_spec((1, d)), _mod_spec(n_lat // tm, d)]
        + [_const_spec(a.shape) for a in consts],
        out_specs=pl.BlockSpec((1, tm, d), row),
        out_shape=jax.ShapeDtypeStruct(x.shape, F32),
        scratch_shapes=[pltpu.VMEM((tm, w["lng"].shape[-1]), BF16)],
        input_output_aliases={0: 0},
        compiler_params=_cparams(2),
        name="chunk_mlp",
    )(x, g, mod, *consts)


def _router_kernel(x_ref, g_ref, mod_ref, r_ref, h_ref, aff_ref):
    mod = mod_ref[0, 0]
    h = _norm_mod(x_ref[0], g_ref[...], mod[3:4], mod[4:5])
    h_ref[0] = h.astype(BF16)
    logits = jnp.dot(h, r_ref[...], preferred_element_type=F32, precision=lax.Precision.HIGHEST)
    e = jnp.exp(logits - jnp.max(logits, axis=-1, keepdims=True))
    aff_ref[0] = e / jnp.sum(e, axis=-1, keepdims=True)


def _router(x, g, mod, router, n_lat, n_tiles):
    b, t, d = x.shape
    tm = ROW_TILE
    ne = router.shape[-1]
    row = lambda bb, i: (bb, i, 0)
    return pl.pallas_call(
        _router_kernel,
        grid=(b, n_tiles),
        in_specs=[pl.BlockSpec((1, tm, d), row), _const_spec((1, d)), _mod_spec(n_lat // tm, d),
                  _const_spec(router.shape)],
        out_specs=[pl.BlockSpec((1, tm, d), row), pl.BlockSpec((1, tm, ne), row)],
        out_shape=[jax.ShapeDtypeStruct((b, n_tiles * tm, d), BF16), jax.ShapeDtypeStruct((b, n_tiles * tm, ne), F32)],
        compiler_params=_cparams(2),
        name="router",
    )(x, g, mod, router)


def _prefix_counts(mask, tri):
    out = []
    carry = jnp.zeros((mask.shape[0], 1), F32)
    for c in range(mask.shape[1] // PREFIX_CHUNK):
        m = mask[:, c * PREFIX_CHUNK:(c + 1) * PREFIX_CHUNK]
        out.append(_dot(m.astype(BF16), tri) + carry)
        carry = carry + jnp.sum(m, axis=-1, keepdims=True)
    return out


def _topk_kernel(aff_ref, pos_ref, *, cap):
    a = aff_ref[0]
    ne = a.shape[0]
    bits = pltpu.bitcast(a, jnp.int32)

    def step(i, lo):
        cand = lo | jnp.left_shift(jnp.int32(1), 30 - i)
        cnt = jnp.sum(jnp.where(bits >= cand, 1.0, 0.0), axis=-1, keepdims=True)
        return jnp.where(cnt >= cap, cand, lo)

    thr = lax.fori_loop(0, 31, step, jnp.zeros((ne, 1), jnp.int32))
    gt = jnp.where(bits > thr, 1.0, 0.0)
    eq = jnp.where(bits == thr, 1.0, 0.0)
    room = cap - jnp.sum(gt, axis=-1, keepdims=True)
    ri = lax.broadcasted_iota(jnp.int32, (PREFIX_CHUNK, PREFIX_CHUNK), 0)
    ci = lax.broadcasted_iota(jnp.int32, (PREFIX_CHUNK, PREFIX_CHUNK), 1)
    tri = jnp.where(ri <= ci, 1.0, 0.0).astype(BF16)
    eq_rank = _prefix_counts(eq, tri)
    sel = jnp.concatenate([
        jnp.maximum(gt[:, c * PREFIX_CHUNK:(c + 1) * PREFIX_CHUNK],
                    jnp.where(r <= room, eq[:, c * PREFIX_CHUNK:(c + 1) * PREFIX_CHUNK], 0.0))
        for c, r in enumerate(eq_rank)], axis=-1)
    for c, r in enumerate(_prefix_counts(sel, tri)):
        cols = slice(c * PREFIX_CHUNK, (c + 1) * PREFIX_CHUNK)
        pos_ref[0, :, cols] = jnp.where(sel[:, cols] > 0.0, r - 1.0, -1.0).astype(jnp.int32)


def _topk(aff_t, cap):
    b, ne, n = aff_t.shape
    spec = pl.BlockSpec((1, ne, n), lambda bb: (bb, 0, 0))
    return pl.pallas_call(
        functools.partial(_topk_kernel, cap=cap),
        grid=(b,),
        in_specs=[spec],
        out_specs=spec,
        out_shape=jax.ShapeDtypeStruct((b, ne, n), jnp.int32),
        compiler_params=_cparams(1),
        name="topk",
    )(aff_t)


def _moe_ffn_kernel(h_ref, pos_ref, aff_ref, wg_ref, wu_ref, wd_ref, y_ref, *, cap):
    bg, n, d = h_ref.shape
    kc = min(GATHER_CHUNK, n)
    slot = lax.broadcasted_iota(jnp.int32, (cap, kc), 0)
    xs, gates = [], []
    for bb in range(bg):
        xb, gb = jnp.zeros((cap, d), F32), jnp.zeros((cap, 1), F32)
        for c in range(n // kc):
            cols = slice(c * kc, (c + 1) * kc)
            hit = pos_ref[bb, 0, :, cols] == slot
            xb = xb + _dot(jnp.where(hit, 1.0, 0.0).astype(BF16), h_ref[bb, cols, :])
            gb = gb + jnp.sum(jnp.where(hit, aff_ref[bb, 0, :, cols], 0.0), axis=-1, keepdims=True)
        xs.append(xb)
        gates.append(gb)
    xg = jnp.concatenate(xs, axis=0).astype(BF16)
    gate = jnp.concatenate(gates, axis=0)
    hid = _silu(_dot(xg, wg_ref[0])) * _dot(xg, wu_ref[0])
    y = _dot(hid.astype(BF16), wd_ref[0]) * gate
    for bb in range(bg):
        y_ref[bb, 0] = y[bb * cap:(bb + 1) * cap].astype(BF16)


def _moe_ffn(h, pos, aff_t, wg, wu, wd, set_block, n_set, cap, bg):
    b, _, d = h.shape
    ne, _, ff = wg.shape
    sel = pl.BlockSpec((bg, 1, 1, n_set), lambda i, e: (i, e, 0, 0))
    return pl.pallas_call(
        functools.partial(_moe_ffn_kernel, cap=cap),
        grid=(b // bg, ne),
        in_specs=[pl.BlockSpec((bg, n_set, d), lambda i, e: (i, set_block, 0)), sel, sel,
                  pl.BlockSpec((1, d, ff), lambda i, e: (e, 0, 0)),
                  pl.BlockSpec((1, d, ff), lambda i, e: (e, 0, 0)),
                  pl.BlockSpec((1, ff, d), lambda i, e: (e, 0, 0))],
        out_specs=pl.BlockSpec((bg, 1, cap, d), lambda i, e: (i, e, 0, 0)),
        out_shape=jax.ShapeDtypeStruct((b, ne, cap, d), BF16),
        compiler_params=_cparams(2),
        name="moe_ffn",
    )(h, pos.reshape(b, ne, 1, n_set), aff_t.reshape(b, ne, 1, n_set), wg, wu, wd)


def _moe_combine_kernel(x_ref, pos_ref, y_ref, mod_ref, o_ref, *, cap):
    pos = pos_ref[0]
    tn, ne = pos.shape
    lane = lax.broadcasted_iota(jnp.int32, (tn, cap), 1)
    acc = jnp.zeros(x_ref.shape[1:], F32)
    for e in range(ne):
        onehot = jnp.where(pos[:, e:e + 1] == lane, 1.0, 0.0).astype(BF16)
        acc = acc + _dot(onehot, y_ref[0, e])
    o_ref[0] = x_ref[0] + mod_ref[0, 0][5:6] * acc


def _moe_combine(x, pos_t, y, mod, tile_off, n_set, tn, mod_sel, out_rows):
    b, t, d = x.shape
    _, ne, cap, _ = y.shape
    alias = out_rows == t
    out_off = tile_off if alias else 0
    return pl.pallas_call(
        functools.partial(_moe_combine_kernel, cap=cap),
        grid=(b, n_set // tn),
        in_specs=[pl.BlockSpec((1, tn, d), lambda bb, i: (bb, i + tile_off, 0)),
                  pl.BlockSpec((1, tn, ne), lambda bb, i: (bb, i, 0)),
                  pl.BlockSpec((1, ne, cap, d), lambda bb, i: (bb, 0, 0, 0)),
                  pl.BlockSpec((1, 1, 6, d), lambda bb, i: (bb, mod_sel, 0, 0))],
        out_specs=pl.BlockSpec((1, tn, d), lambda bb, i: (bb, i + out_off, 0)),
        out_shape=jax.ShapeDtypeStruct((b, out_rows, d), F32),
        input_output_aliases={0: 0} if alias else {},
        compiler_params=_cparams(2),
        name="moe_combine",
    )(x, pos_t, y, mod)


def _moe(x, g, mod, router, wg, wu, wd, n_lat, with_ctx, final):
    b, t, d = x.shape
    n_ctx = t - n_lat
    tm = ROW_TILE
    h, aff = _router(x, g, mod, router, n_lat, (t if with_ctx else n_lat) // tm)
    aff_t = jnp.swapaxes(aff, 1, 2)
    sets = [(0, n_lat, 0, min(512, n_lat), 1)]
    if with_ctx:
        sets.append((n_lat, n_ctx, 1, n_ctx, b))
    for start, n_set, mod_sel, tn, bg in sets:
        cap = CAPACITY_FACTOR * n_set // N_EXPERTS
        a_set = aff_t[:, :, start:start + n_set]
        pos = _topk(a_set, cap)
        y = _moe_ffn(h, pos, a_set, wg, wu, wd, start // n_set, n_set, cap, bg)
        x = _moe_combine(x, jnp.swapaxes(pos, 1, 2), y, mod, start // tn, n_set, tn, mod_sel,
                         n_lat if final else t)
    return x


def _rope_tables(n_lat, n_ctx, rot_dim, starts):
    n_rows = n_lat // GRID_W
    rows = jnp.repeat(jnp.arange(n_rows, dtype=F32), GRID_W)
    cols = jnp.tile(jnp.arange(GRID_W, dtype=F32), n_rows)
    n_freq = rot_dim // 4
    inv_freq = ROPE_BASE ** (-jnp.arange(n_freq, dtype=F32) / n_freq)
    ang = jnp.concatenate([rows[:, None] * inv_freq, cols[:, None] * inv_freq], axis=-1)
    half = rot_dim // 2
    cos_l, sin_l = jnp.cos(ang), jnp.sin(ang)
    cos = jnp.ones((n_lat, LANE), F32)
    sa = jnp.zeros((n_lat, LANE), F32)
    sb = jnp.zeros((n_lat, LANE), F32)
    for s in starts:
        cos = cos.at[:, s:s + half].set(cos_l).at[:, s + half:s + rot_dim].set(cos_l)
        sa = sa.at[:, s:s + half].set(-sin_l)
        sb = sb.at[:, s + half:s + rot_dim].set(sin_l)
    pad = lambda a, v: jnp.concatenate([a, jnp.full((n_ctx, LANE), v, F32)], axis=0)
    return pad(cos, 1.0), pad(sa, 0.0), pad(sb, 0.0)


def _pad_heads(w, heads, width):
    k = w.shape[0]
    w = w.reshape(k, heads, width)
    return jnp.pad(w, ((0, 0), (0, 0), (0, LANE - width))).reshape(k, heads * LANE)


def _mla_weights(w_in, q_norm_g, w_uq, kv_norm_g, w_ukv, qn_g, kn_g):
    d = w_in.shape[0]
    lat = MLA_Q_LORA + MLA_KV_LORA
    rope_block = jnp.concatenate([jnp.zeros((d, MLA_NOPE), F32), w_in[:, lat:],
                                  jnp.zeros((d, LANE - MLA_QK), F32)], axis=1)
    ukv = w_ukv.reshape(MLA_KV_LORA, MLA_HEADS, MLA_NOPE + MLA_V)
    pad_g = lambda g: jnp.pad(g, (0, LANE - MLA_QK)).reshape(1, LANE)
    return {
        "win": jnp.concatenate([w_in[:, :lat], rope_block], axis=1).astype(BF16),
        "qng": q_norm_g.reshape(1, -1),
        "wuq": _pad_heads(w_uq, MLA_HEADS, MLA_QK).astype(BF16),
        "kvng": kv_norm_g.reshape(1, -1),
        "wuk": _pad_heads(ukv[:, :, :MLA_NOPE].reshape(MLA_KV_LORA, -1), MLA_HEADS, MLA_NOPE).astype(BF16),
        "wuv": ukv[:, :, MLA_NOPE:].reshape(MLA_KV_LORA, -1).astype(BF16),
        "qn": pad_g(qn_g),
        "kn": pad_g(kn_g),
    }


def kernel(x, c, ctx, c_ctx, ada_w, ada_b, norm_mix_g, norm_ffn_g, mla_w_in, mla_q_norm_g, mla_w_uq, mla_kv_norm_g, mla_w_ukv, mla_qn_g, mla_kn_g, mla_w_out, diff_w_in, diff_qn_g, diff_kn_g, diff_lambda_q1, diff_lambda_k1, diff_lambda_q2, diff_lambda_k2, diff_sub_g, diff_w_out, sg_w_in, sg_ln_g, sg_ln_b, sg_w_s, sg_b_s, sg_w_out, moe_router, moe_w_gate, moe_w_up, moe_w_down):
    b, n_lat, d = x.shape
    n_ctx = ctx.shape[1]
    depth = ada_w.shape[0]
    assert n_lat % ROW_TILE == 0 and n_ctx == ROW_TILE and n_lat % n_ctx == 0

    rows = -(-(b + 1) // 8) * 8
    cc = jnp.concatenate([c, c_ctx[None], jnp.zeros((rows - b - 1, d), F32)], axis=0)
    mods = _ada(cc, ada_w, ada_b).reshape(depth, rows, 6, d)
    xs = jnp.concatenate([x, ctx], axis=1)

    tabs_a = _rope_tables(n_lat, n_ctx, MLA_ROPE, (MLA_NOPE,))
    tabs_b = _rope_tables(n_lat, n_ctx, DIFF_HEAD_DIM, (0, DIFF_HEAD_DIM))

    for i in range(depth):
        kind, j = i % N_MIXERS, i // N_MIXERS
        last = i == depth - 1
        with_ctx = not last
        q_tiles = (n_lat + (n_ctx if with_ctx else 0)) // ROW_TILE
        mod = jnp.stack([mods[i, :b], jnp.broadcast_to(mods[i, b], (b, 6, d))], axis=1)
        g_mix = norm_mix_g[i].reshape(1, d)
        if kind == 0:
            w = _mla_weights(mla_w_in[j], mla_q_norm_g[j], mla_w_uq[j], mla_kv_norm_g[j], mla_w_ukv[j],
                             mla_qn_g[j], mla_kn_g[j])
            q, k, v = _mla_in(xs, g_mix, mod, w, tabs_a, n_lat)
            a = _mla_attn(q, k, v, n_lat, q_tiles)
            xs = _out_proj(xs, a, mla_w_out[j].astype(BF16), mod, n_lat, q_tiles)
        elif kind == 1:
            lam_init = 0.8 - 0.6 * math.exp(-0.3 * i)
            tile2 = lambda g: jnp.tile(g, 2).reshape(1, LANE)
            w = {"win": diff_w_in[j].astype(BF16), "qn": tile2(diff_qn_g[j]), "kn": tile2(diff_kn_g[j])}
            q0, q1, k, v = _diff_in(xs, g_mix, mod, w, tabs_b, n_lat)
            wa = {"lq1": diff_lambda_q1[j].reshape(1, -1), "lk1": diff_lambda_k1[j].reshape(1, -1),
                  "lq2": diff_lambda_q2[j].reshape(1, -1), "lk2": diff_lambda_k2[j].reshape(1, -1),
                  "sub": diff_sub_g[j].reshape(1, -1)}
            a = _diff_attn(q0, q1, k, v, wa, n_lat, q_tiles, lam_init)
            xs = _out_proj(xs, a, diff_w_out[j].astype(BF16), mod, n_lat, q_tiles)
        else:
            w = {"win": sg_w_in[j].astype(BF16), "lng": sg_ln_g[j].reshape(1, -1), "lnb": sg_ln_b[j].reshape(1, -1),
                 "ws": sg_w_s[j].astype(BF16), "bs": sg_b_s[j].T, "wout": sg_w_out[j].astype(BF16)}
            xs = _sg(xs, g_mix, mod, w, n_lat)
        xs = _moe(xs, norm_ffn_g[i].reshape(1, d), mod, moe_router[i], moe_w_gate[i].astype(BF16),
                  moe_w_up[i].astype(BF16), moe_w_down[i].astype(BF16), n_lat, with_ctx, last)
    return xs
```

```python
import functools
import math

import jax
import jax.numpy as jnp
from jax import lax
from jax.experimental import pallas as pl
from jax.experimental.pallas import tpu as pltpu

F32 = jnp.float32
BF16 = jnp.bfloat16

GRID_W = 64
ROPE_BASE = 10000.0
EPS = 1e-6
N_MIXERS = 3

MLA_HEADS = 16
MLA_Q_LORA = 512
MLA_KV_LORA = 256
MLA_NOPE = 64
MLA_ROPE = 32
MLA_V = 64
MLA_QK = MLA_NOPE + MLA_ROPE

DIFF_HEADS = 8
DIFF_HEAD_DIM = 64
DIFF_V_DIM = 2 * DIFF_HEAD_DIM

SG_CHUNK = 128
SG_GROUPS = 8

N_EXPERTS = 16
CAPACITY_FACTOR = 2

LANE = 128
ROW_TILE = 256
PREFIX_CHUNK = 256
GATHER_CHUNK = 1024
MLA_HEADS_PER_STEP = 4
DIFF_HEADS_PER_STEP = 2
LOG2E = math.log2(math.e)
VMEM_LIMIT = 56 * 1024 * 1024


def _cparams(n_axes):
    return pltpu.CompilerParams(dimension_semantics=("arbitrary",) * n_axes, vmem_limit_bytes=VMEM_LIMIT)


def _const_spec(shape):
    nd = len(shape)
    return pl.BlockSpec(shape, lambda *_: (0,) * nd)


def _dot(a, b):
    return jnp.dot(a, b, preferred_element_type=F32)


def _dot_nt(a, b):
    return lax.dot_general(a, b, (((1,), (1,)), ((), ())), preferred_element_type=F32)


def _rms(x, n):
    return x * lax.rsqrt(jnp.sum(x * x, axis=-1, keepdims=True) * (1.0 / n) + EPS)


def _norm_mod(x, g, shift, scale):
    return _rms(x, x.shape[-1]) * g * (1.0 + scale) + shift


def _silu(x):
    return x / (1.0 + jnp.exp(-x))


def _gelu_tanh(x):
    c = math.sqrt(2.0 / math.pi)
    return 0.5 * x * (1.0 + jnp.tanh(c * (x + 0.044715 * (x * x * x))))


def _rope(x, cos, sin_a, sin_b, half):
    return x * cos + pltpu.roll(x, LANE - half, 1) * sin_a + pltpu.roll(x, half, 1) * sin_b


def _ada_kernel(c_ref, w_ref, b_ref, o_ref):
    s = _silu(c_ref[...])
    o_ref[0] = jnp.dot(s, w_ref[0], preferred_element_type=F32, precision=lax.Precision.HIGHEST) + b_ref[0]


def _ada(cc, ada_w, ada_b):
    depth, d, six_d = ada_w.shape
    rows = cc.shape[0]
    tn = 1536
    return pl.pallas_call(
        _ada_kernel,
        grid=(depth, six_d // tn),
        in_specs=[
            _const_spec((rows, d)),
            pl.BlockSpec((1, d, tn), lambda i, j: (i, 0, j)),
            pl.BlockSpec((1, 1, tn), lambda i, j: (i, 0, j)),
        ],
        out_specs=pl.BlockSpec((1, rows, tn), lambda i, j: (i, 0, j)),
        out_shape=jax.ShapeDtypeStruct((depth, rows, six_d), F32),
        compiler_params=_cparams(2),
        name="ada",
    )(cc, ada_w, ada_b.reshape(depth, 1, six_d))


def _mod_spec(n_lat_tiles, d):
    return pl.BlockSpec((1, 1, 6, d), lambda b, i: (b, jnp.minimum(i // n_lat_tiles, 1), 0, 0))


def _mla_in_kernel(x_ref, g_ref, mod_ref, win_ref, qng_ref, wuq_ref, kvng_ref, wuk_ref, wuv_ref, qn_ref, kn_ref,
                   cos_ref, sa_ref, sb_ref, q_ref, k_ref, v_ref):
    mod = mod_ref[0, 0]
    h = _norm_mod(x_ref[0], g_ref[...], mod[0:1], mod[1:2])
    down = _dot(h.astype(BF16), win_ref[...])
    cq = _rms(down[:, :MLA_Q_LORA], MLA_Q_LORA) * qng_ref[...]
    ckv = (_rms(down[:, MLA_Q_LORA:MLA_Q_LORA + MLA_KV_LORA], MLA_KV_LORA) * kvng_ref[...]).astype(BF16)
    q = _dot(cq.astype(BF16), wuq_ref[...])
    kn = _dot(ckv, wuk_ref[...])
    v = _dot(ckv, wuv_ref[...])
    is_v = (lax.broadcasted_iota(jnp.int32, v.shape, 1) & (LANE - 1)) < MLA_V
    v_ref[0] = jnp.where(is_v, v, 1.0).astype(BF16)
    kr = down[:, MLA_Q_LORA + MLA_KV_LORA:]
    cos, sa, sb = cos_ref[...], sa_ref[...], sb_ref[...]
    qn, kng = qn_ref[...], kn_ref[...]
    scale = MLA_QK ** -0.5 * LOG2E
    for hd in range(MLA_HEADS):
        sl = slice(hd * LANE, (hd + 1) * LANE)
        qh = _rope(_rms(q[:, sl], MLA_QK) * qn, cos, sa, sb, MLA_ROPE // 2)
        q_ref[0, :, sl] = (qh * scale).astype(BF16)
        kh = _rope(_rms(kn[:, sl] + kr, MLA_QK) * kng, cos, sa, sb, MLA_ROPE // 2)
        k_ref[0, :, sl] = kh.astype(BF16)


def _mla_in(x, g, mod, w, tabs, n_lat):
    b, t, d = x.shape
    tm = ROW_TILE
    hw = MLA_HEADS * LANE
    row = lambda bb, i: (bb, i, 0)
    tab = pl.BlockSpec((tm, LANE), lambda bb, i: (i, 0))
    consts = [w["win"], w["qng"], w["wuq"], w["kvng"], w["wuk"], w["wuv"], w["qn"], w["kn"]]
    return pl.pallas_call(
        _mla_in_kernel,
        grid=(b, t // tm),
        in_specs=[pl.BlockSpec((1, tm, d), row), _const_spec((1, d)), _mod_spec(n_lat // tm, d)]
        + [_const_spec(a.shape) for a in consts] + [tab, tab, tab],
        out_specs=[pl.BlockSpec((1, tm, hw), row)] * 3,
        out_shape=[jax.ShapeDtypeStruct((b, t, hw), BF16)] * 3,
        compiler_params=_cparams(2),
        name="mla_in",
    )(x, g, mod, *consts, *tabs)


def _diff_in_kernel(x_ref, g_ref, mod_ref, win_ref, qn_ref, kn_ref, cos_ref, sa_ref, sb_ref,
                    q0_ref, q1_ref, k_ref, v_ref):
    mod = mod_ref[0, 0]
    d = x_ref.shape[-1]
    h = _norm_mod(x_ref[0], g_ref[...], mod[0:1], mod[1:2])
    qkv = _dot(h.astype(BF16), win_ref[...])
    v_ref[0] = qkv[:, 2 * d:].astype(BF16)
    cos, sa, sb = cos_ref[...], sa_ref[...], sb_ref[...]
    lo = lax.broadcasted_iota(jnp.int32, (x_ref.shape[1], LANE), 1) < DIFF_HEAD_DIM
    scale = DIFF_HEAD_DIM ** -0.5 * LOG2E

    def norm2(xh, gains):
        sq = xh * xh
        s_lo = jnp.sum(jnp.where(lo, sq, 0.0), axis=-1, keepdims=True)
        s_hi = jnp.sum(jnp.where(lo, 0.0, sq), axis=-1, keepdims=True)
        ms = jnp.where(lo, s_lo, s_hi) * (1.0 / DIFF_HEAD_DIM)
        return xh * lax.rsqrt(ms + EPS) * gains

    for hd in range(DIFF_HEADS):
        sl = slice(hd * LANE, (hd + 1) * LANE)
        qh = _rope(norm2(qkv[:, sl], qn_ref[...]), cos, sa, sb, DIFF_HEAD_DIM // 2) * scale
        q0_ref[0, :, sl] = jnp.where(lo, qh, 0.0).astype(BF16)
        q1_ref[0, :, sl] = jnp.where(lo, 0.0, qh).astype(BF16)
        kh = _rope(norm2(qkv[:, d + hd * LANE:d + (hd + 1) * LANE], kn_ref[...]), cos, sa, sb, DIFF_HEAD_DIM // 2)
        k_ref[0, :, sl] = kh.astype(BF16)


def _diff_in(x, g, mod, w, tabs, n_lat):
    b, t, d = x.shape
    tm = ROW_TILE
    row = lambda bb, i: (bb, i, 0)
    tab = pl.BlockSpec((tm, LANE), lambda bb, i: (i, 0))
    consts = [w["win"], w["qn"], w["kn"]]
    out = jax.ShapeDtypeStruct((b, t, d), BF16)
    return pl.pallas_call(
        _diff_in_kernel,
        grid=(b, t // tm),
        in_specs=[pl.BlockSpec((1, tm, d), row), _const_spec((1, d)), _mod_spec(n_lat // tm, d)]
        + [_const_spec(a.shape) for a in consts] + [tab, tab, tab],
        out_specs=[pl.BlockSpec((1, tm, d), row)] * 4,
        out_shape=[out] * 4,
        compiler_params=_cparams(2),
        name="diff_in",
    )(x, g, mod, *consts, *tabs)


def _exp2_scores(q, k):
    s = _dot_nt(q, k)
    return jnp.exp2(s - jnp.max(s, axis=-1, keepdims=True)).astype(BF16)


def _mla_attn_kernel(q_ref, k_ref, v_ref, o_ref, *, n_lat):
    tq = q_ref.shape[1]
    low = lax.broadcasted_iota(jnp.int32, (tq, LANE), 1) < MLA_V

    def run(ks):
        for pr in range(MLA_HEADS_PER_STEP // 2):
            halves = []
            for hd in (2 * pr, 2 * pr + 1):
                sl = slice(hd * LANE, (hd + 1) * LANE)
                r = _dot(_exp2_scores(q_ref[0, :, sl], k_ref[0, ks, sl]), v_ref[0, ks, sl])
                halves.append(r / pltpu.roll(r, LANE - MLA_V, 1))
            pair = jnp.where(low, halves[0], pltpu.roll(halves[1], MLA_V, 1))
            o_ref[0, :, pr * LANE:(pr + 1) * LANE] = pair.astype(o_ref.dtype)

    is_lat = pl.program_id(2) * tq < n_lat
    pl.when(is_lat)(lambda: run(slice(None)))
    pl.when(jnp.logical_not(is_lat))(lambda: run(slice(n_lat, None)))


def _mla_attn(q, k, v, n_lat, n_q_tiles):
    b, t, _ = q.shape
    tq = ROW_TILE
    hs = MLA_HEADS_PER_STEP
    kv = pl.BlockSpec((1, t, hs * LANE), lambda bb, hg, i: (bb, 0, hg))
    return pl.pallas_call(
        functools.partial(_mla_attn_kernel, n_lat=n_lat),
        grid=(b, MLA_HEADS // hs, n_q_tiles),
        in_specs=[pl.BlockSpec((1, tq, hs * LANE), lambda bb, hg, i: (bb, i, hg)), kv, kv],
        out_specs=pl.BlockSpec((1, tq, hs * MLA_V), lambda bb, hg, i: (bb, i, hg)),
        out_shape=jax.ShapeDtypeStruct((b, t, MLA_HEADS * MLA_V), BF16),
        compiler_params=_cparams(3),
        name="mla_attn",
    )(q, k, v)


def _diff_attn_kernel(q0_ref, q1_ref, k_ref, v_ref, lq1_ref, lk1_ref, lq2_ref, lk2_ref, sub_ref, o_ref,
                      *, n_lat, lam_init):
    tq = q0_ref.shape[1]
    lam = (jnp.exp(jnp.sum(lq1_ref[...] * lk1_ref[...], axis=-1, keepdims=True))
           - jnp.exp(jnp.sum(lq2_ref[...] * lk2_ref[...], axis=-1, keepdims=True)) + lam_init)

    def run(k0):
        for a in range(DIFF_HEADS_PER_STEP):
            sl = slice(a * LANE, (a + 1) * LANE)
            k = k_ref[0, k0:, sl]
            v = v_ref[0, k0:, sl]
            outs = []
            for q_ref in (q0_ref, q1_ref):
                s = _dot_nt(q_ref[0, :, sl], k)
                p = jnp.exp2(s - jnp.max(s, axis=-1, keepdims=True))
                outs.append(_dot(p.astype(BF16), v) / jnp.sum(p, axis=-1, keepdims=True))
            o = outs[0] - lam * outs[1]
            o_ref[0, :, sl] = (_rms(o, DIFF_V_DIM) * sub_ref[...] * (1.0 - lam_init)).astype(o_ref.dtype)

    is_lat = pl.program_id(2) * tq < n_lat
    pl.when(is_lat)(lambda: run(0))
    pl.when(jnp.logical_not(is_lat))(lambda: run(n_lat))


def _diff_attn(q0, q1, k, v, w, n_lat, n_q_tiles, lam_init):
    b, t, d = k.shape
    tq = ROW_TILE
    hs = DIFF_HEADS_PER_STEP
    kv = pl.BlockSpec((1, t, hs * LANE), lambda bb, hg, i: (bb, 0, hg))
    qo = pl.BlockSpec((1, tq, hs * LANE), lambda bb, hg, i: (bb, i, hg))
    consts = [w["lq1"], w["lk1"], w["lq2"], w["lk2"], w["sub"]]
    return pl.pallas_call(
        functools.partial(_diff_attn_kernel, n_lat=n_lat, lam_init=lam_init),
        grid=(b, DIFF_HEADS // hs, n_q_tiles),
        in_specs=[qo, qo, kv, kv] + [_const_spec(a.shape) for a in consts],
        out_specs=qo,
        out_shape=jax.ShapeDtypeStruct((b, t, d), BF16),
        compiler_params=_cparams(3),
        name="diff_attn",
    )(q0, q1, k, v, *consts)


def _out_proj_kernel(x_ref, a_ref, w_ref, mod_ref, o_ref):
    gate = mod_ref[0, 0][2:3]
    o_ref[0] = x_ref[0] + gate * _dot(a_ref[0], w_ref[...])


def _out_proj(x, a, w, mod, n_lat, n_tiles):
    b, t, d = x.shape
    tm = ROW_TILE
    row = lambda bb, i: (bb, i, 0)
    return pl.pallas_call(
        _out_proj_kernel,
        grid=(b, n_tiles),
        in_specs=[pl.BlockSpec((1, tm, d), row), pl.BlockSpec((1, tm, a.shape[-1]), row), _const_spec(w.shape),
                  _mod_spec(n_lat // tm, d)],
        out_specs=pl.BlockSpec((1, tm, d), row),
        out_shape=jax.ShapeDtypeStruct(x.shape, F32),
        input_output_aliases={0: 0},
        compiler_params=_cparams(2),
        name="out_proj",
    )(x, a, w, mod)


def _sg_kernel(x_ref, g_ref, mod_ref, win_ref, lng_ref, lnb_ref, ws_ref, bs_ref, wout_ref, o_ref, gated_ref):
    mod = mod_ref[0, 0]
    x = x_ref[0]
    tm = x.shape[0]
    width = lng_ref.shape[-1]
    gdim = width // SG_GROUPS
    h = _norm_mod(x, g_ref[...], mod[0:1], mod[1:2])
    z = _gelu_tanh(_dot(h.astype(BF16), win_ref[...]))
    u, v = z[:, :width], z[:, width:]
    mu = jnp.mean(v, axis=-1, keepdims=True)
    vc = v - mu
    var = jnp.mean(vc * vc, axis=-1, keepdims=True)
    vn = (vc * lax.rsqrt(var + EPS) * lng_ref[...] + lnb_ref[...]).astype(BF16)
    for c in range(tm // SG_CHUNK):
        rows = slice(c * SG_CHUNK, (c + 1) * SG_CHUNK)
        for gi in range(SG_GROUPS):
            cols = slice(gi * gdim, (gi + 1) * gdim)
            mixed = _dot(ws_ref[gi], vn[rows, cols]) + bs_ref[:, gi:gi + 1]
            gated_ref[rows, cols] = (u[rows, cols] * mixed).astype(BF16)
    o_ref[0] = x + mod[2:3] * _dot(gated_ref[...], wout_ref[...])


def _sg(x, g, mod, w, n_lat):
    b, t, d = x.shape
    tm = ROW_TILE
    row = lambda bb, i: (bb, i, 0)
    consts = [w["win"], w["lng"], w["lnb"], w["ws"], w["bs"], w["wout"]]
    return pl.pallas_call(
        _sg_kernel,
        grid=(b, t // tm),
        in_specs=[pl.BlockSpec((1, tm, d), row), _const_spec((1, d)), _mod_spec(n_lat // tm, d)]
        + [_const_spec(a.shape) for a in consts],
        out_specs=pl.BlockSpec((1, tm, d), row),
        out_shape=jax.ShapeDtypeStruct(x.shape, F32),
        scratch_shapes=[pltpu.VMEM((tm, w["lng"].shape[-1]), BF16)],
        input_output_aliases={0: 0},
        compiler_params=_cparams(2),
        name="chunk_mlp",
    )(x, g, mod, *consts)


def _router_kernel(x_ref, g_ref, mod_ref, r_ref, h_ref, aff_ref):
    mod = mod_ref[0, 0]
    h = _norm_mod(x_ref[0], g_ref[...], mod[3:4], mod[4:5])
    h_ref[0] = h.astype(BF16)
    logits = jnp.dot(h, r_ref[...], preferred_element_type=F32, precision=lax.Precision.HIGHEST)
    e = jnp.exp(logits - jnp.max(logits, axis=-1, keepdims=True))
    aff_ref[0] = e / jnp.sum(e, axis=-1, keepdims=True)


def _router(x, g, mod, router, n_lat, n_tiles):
    b, t, d = x.shape
    tm = ROW_TILE
    ne = router.shape[-1]
    row = lambda bb, i: (bb, i, 0)
    return pl.pallas_call(
        _router_kernel,
        grid=(b, n_tiles),
        in_specs=[pl.BlockSpec((1, tm, d), row), _const_spec((1, d)), _mod_spec(n_lat // tm, d),
                  _const_spec(router.shape)],
        out_specs=[pl.BlockSpec((1, tm, d), row), pl.BlockSpec((1, tm, ne), row)],
        out_shape=[jax.ShapeDtypeStruct((b, n_tiles * tm, d), BF16), jax.ShapeDtypeStruct((b, n_tiles * tm, ne), F32)],
        compiler_params=_cparams(2),
        name="router",
    )(x, g, mod, router)


def _prefix_counts(mask, tri):
    out = []
    carry = jnp.zeros((mask.shape[0], 1), F32)
    for c in range(mask.shape[1] // PREFIX_CHUNK):
        m = mask[:, c * PREFIX_CHUNK:(c + 1) * PREFIX_CHUNK]
        out.append(_dot(m.astype(BF16), tri) + carry)
        carry = carry + jnp.sum(m, axis=-1, keepdims=True)
    return out


def _topk_kernel(aff_ref, pos_ref, *, cap):
    a = aff_ref[0]
    ne = a.shape[0]
    bits = pltpu.bitcast(a, jnp.int32)

    def step(i, lo):
        cand = lo | jnp.left_shift(jnp.int32(1), 30 - i)
        cnt = jnp.sum(jnp.where(bits >= cand, 1.0, 0.0), axis=-1, keepdims=True)
        return jnp.where(cnt >= cap, cand, lo)

    thr = lax.fori_loop(0, 31, step, jnp.zeros((ne, 1), jnp.int32))
    gt = jnp.where(bits > thr, 1.0, 0.0)
    eq = jnp.where(bits == thr, 1.0, 0.0)
    room = cap - jnp.sum(gt, axis=-1, keepdims=True)
    ri = lax.broadcasted_iota(jnp.int32, (PREFIX_CHUNK, PREFIX_CHUNK), 0)
    ci = lax.broadcasted_iota(jnp.int32, (PREFIX_CHUNK, PREFIX_CHUNK), 1)
    tri = jnp.where(ri <= ci, 1.0, 0.0).astype(BF16)
    eq_rank = _prefix_counts(eq, tri)
    sel = jnp.concatenate([
        jnp.maximum(gt[:, c * PREFIX_CHUNK:(c + 1) * PREFIX_CHUNK],
                    jnp.where(r <= room, eq[:, c * PREFIX_CHUNK:(c + 1) * PREFIX_CHUNK], 0.0))
        for c, r in enumerate(eq_rank)], axis=-1)
    for c, r in enumerate(_prefix_counts(sel, tri)):
        cols = slice(c * PREFIX_CHUNK, (c + 1) * PREFIX_CHUNK)
        pos_ref[0, :, cols] = jnp.where(sel[:, cols] > 0.0, r - 1.0, -1.0).astype(jnp.int32)


def _topk(aff_t, cap):
    b, ne, n = aff_t.shape
    spec = pl.BlockSpec((1, ne, n), lambda bb: (bb, 0, 0))
    return pl.pallas_call(
        functools.partial(_topk_kernel, cap=cap),
        grid=(b,),
        in_specs=[spec],
        out_specs=spec,
        out_shape=jax.ShapeDtypeStruct((b, ne, n), jnp.int32),
        compiler_params=_cparams(1),
        name="topk",
    )(aff_t)


def _moe_ffn_kernel(h_ref, pos_ref, aff_ref, wg_ref, wu_ref, wd_ref, y_ref, *, cap):
    bg, n, d = h_ref.shape
    kc = min(GATHER_CHUNK, n)
    slot = lax.broadcasted_iota(jnp.int32, (cap, kc), 0)
    xs, gates = [], []
    for bb in range(bg):
        xb, gb = jnp.zeros((cap, d), F32), jnp.zeros((cap, 1), F32)
        for c in range(n // kc):
            cols = slice(c * kc, (c + 1) * kc)
            hit = pos_ref[bb, 0, :, cols] == slot
            xb = xb + _dot(jnp.where(hit, 1.0, 0.0).astype(BF16), h_ref[bb, cols, :])
            gb = gb + jnp.sum(jnp.where(hit, aff_ref[bb, 0, :, cols], 0.0), axis=-1, keepdims=True)
        xs.append(xb)
        gates.append(gb)
    xg = jnp.concatenate(xs, axis=0).astype(BF16)
    gate = jnp.concatenate(gates, axis=0)
    hid = _silu(_dot(xg, wg_ref[0])) * _dot(xg, wu_ref[0])
    y = _dot(hid.astype(BF16), wd_ref[0]) * gate
    for bb in range(bg):
        y_ref[bb, 0] = y[bb * cap:(bb + 1) * cap].astype(BF16)


def _moe_ffn(h, pos, aff_t, wg, wu, wd, set_block, n_set, cap, bg):
    b, _, d = h.shape
    ne, _, ff = wg.shape
    sel = pl.BlockSpec((bg, 1, 1, n_set), lambda i, e: (i, e, 0, 0))
    return pl.pallas_call(
        functools.partial(_moe_ffn_kernel, cap=cap),
        grid=(b // bg, ne),
        in_specs=[pl.BlockSpec((bg, n_set, d), lambda i, e: (i, set_block, 0)), sel, sel,
                  pl.BlockSpec((1, d, ff), lambda i, e: (e, 0, 0)),
                  pl.BlockSpec((1, d, ff), lambda i, e: (e, 0, 0)),
                  pl.BlockSpec((1, ff, d), lambda i, e: (e, 0, 0))],
        out_specs=pl.BlockSpec((bg, 1, cap, d), lambda i, e: (i, e, 0, 0)),
        out_shape=jax.ShapeDtypeStruct((b, ne, cap, d), BF16),
        compiler_params=_cparams(2),
        name="moe_ffn",
    )(h, pos.reshape(b, ne, 1, n_set), aff_t.reshape(b, ne, 1, n_set), wg, wu, wd)


def _moe_combine_kernel(x_ref, pos_ref, y_ref, mod_ref, o_ref, *, cap):
    pos = pos_ref[0]
    tn, ne = pos.shape
    lane = lax.broadcasted_iota(jnp.int32, (tn, cap), 1)
    acc = jnp.zeros(x_ref.shape[1:], F32)
    for e in range(ne):
        onehot = jnp.where(pos[:, e:e + 1] == lane, 1.0, 0.0).astype(BF16)
        acc = acc + _dot(onehot, y_ref[0, e])
    o_ref[0] = x_ref[0] + mod_ref[0, 0][5:6] * acc


def _moe_combine(x, pos_t, y, mod, tile_off, n_set, tn, mod_sel, out_rows):
    b, t, d = x.shape
    _, ne, cap, _ = y.shape
    alias = out_rows == t
    out_off = tile_off if alias else 0
    return pl.pallas_call(
        functools.partial(_moe_combine_kernel, cap=cap),
        grid=(b, n_set // tn),
        in_specs=[pl.BlockSpec((1, tn, d), lambda bb, i: (bb, i + tile_off, 0)),
                  pl.BlockSpec((1, tn, ne), lambda bb, i: (bb, i, 0)),
                  pl.BlockSpec((1, ne, cap, d), lambda bb, i: (bb, 0, 0, 0)),
                  pl.BlockSpec((1, 1, 6, d), lambda bb, i: (bb, mod_sel, 0, 0))],
        out_specs=pl.BlockSpec((1, tn, d), lambda bb, i: (bb, i + out_off, 0)),
        out_shape=jax.ShapeDtypeStruct((b, out_rows, d), F32),
        input_output_aliases={0: 0} if alias else {},
        compiler_params=_cparams(2),
        name="moe_combine",
    )(x, pos_t, y, mod)


def _moe(x, g, mod, router, wg, wu, wd, n_lat, with_ctx, final):
    b, t, d = x.shape
    n_ctx = t - n_lat
    tm = ROW_TILE
    h, aff = _router(x, g, mod, router, n_lat, (t if with_ctx else n_lat) // tm)
    aff_t = jnp.swapaxes(aff, 1, 2)
    sets = [(0, n_lat, 0, min(512, n_lat), 1)]
    if with_ctx:
        sets.append((n_lat, n_ctx, 1, n_ctx, b))
    for start, n_set, mod_sel, tn, bg in sets:
        cap = CAPACITY_FACTOR * n_set // N_EXPERTS
        a_set = aff_t[:, :, start:start + n_set]
        pos = _topk(a_set, cap)
        y = _moe_ffn(h, pos, a_set, wg, wu, wd, start // n_set, n_set, cap, bg)
        x = _moe_combine(x, jnp.swapaxes(pos, 1, 2), y, mod, start // tn, n_set, tn, mod_sel,
                         n_lat if final else t)
    return x


def _rope_tables(n_lat, n_ctx, rot_dim, starts):
    n_rows = n_lat // GRID_W
    rows = jnp.repeat(jnp.arange(n_rows, dtype=F32), GRID_W)
    cols = jnp.tile(jnp.arange(GRID_W, dtype=F32), n_rows)
    n_freq = rot_dim // 4
    inv_freq = ROPE_BASE ** (-jnp.arange(n_freq, dtype=F32) / n_freq)
    ang = jnp.concatenate([rows[:, None] * inv_freq, cols[:, None] * inv_freq], axis=-1)
    half = rot_dim // 2
    cos_l, sin_l = jnp.cos(ang), jnp.sin(ang)
    cos = jnp.ones((n_lat, LANE), F32)
    sa = jnp.zeros((n_lat, LANE), F32)
    sb = jnp.zeros((n_lat, LANE), F32)
    for s in starts:
        cos = cos.at[:, s:s + half].set(cos_l).at[:, s + half:s + rot_dim].set(cos_l)
        sa = sa.at[:, s:s + half].set(-sin_l)
        sb = sb.at[:, s + half:s + rot_dim].set(sin_l)
    pad = lambda a, v: jnp.concatenate([a, jnp.full((n_ctx, LANE), v, F32)], axis=0)
    return pad(cos, 1.0), pad(sa, 0.0), pad(sb, 0.0)


def _pad_heads(w, heads, width):
    k = w.shape[0]
    w = w.reshape(k, heads, width)
    return jnp.pad(w, ((0, 0), (0, 0), (0, LANE - width))).reshape(k, heads * LANE)


def _mla_weights(w_in, q_norm_g, w_uq, kv_norm_g, w_ukv, qn_g, kn_g):
    d = w_in.shape[0]
    lat = MLA_Q_LORA + MLA_KV_LORA
    rope_block = jnp.concatenate([jnp.zeros((d, MLA_NOPE), F32), w_in[:, lat:],
                                  jnp.zeros((d, LANE - MLA_QK), F32)], axis=1)
    ukv = w_ukv.reshape(MLA_KV_LORA, MLA_HEADS, MLA_NOPE + MLA_V)
    pad_g = lambda g: jnp.pad(g, (0, LANE - MLA_QK)).reshape(1, LANE)
    return {
        "win": jnp.concatenate([w_in[:, :lat], rope_block], axis=1).astype(BF16),
        "qng": q_norm_g.reshape(1, -1),
        "wuq": _pad_heads(w_uq, MLA_HEADS, MLA_QK).astype(BF16),
        "kvng": kv_norm_g.reshape(1, -1),
        "wuk": _pad_heads(ukv[:, :, :MLA_NOPE].reshape(MLA_KV_LORA, -1), MLA_HEADS, MLA_NOPE).astype(BF16),
        "wuv": _pad_heads(ukv[:, :, MLA_NOPE:].reshape(MLA_KV_LORA, -1), MLA_HEADS, MLA_V).astype(BF16),
        "qn": pad_g(qn_g),
        "kn": pad_g(kn_g),
    }


def kernel(x, c, ctx, c_ctx, ada_w, ada_b, norm_mix_g, norm_ffn_g, mla_w_in, mla_q_norm_g, mla_w_uq, mla_kv_norm_g, mla_w_ukv, mla_qn_g, mla_kn_g, mla_w_out, diff_w_in, diff_qn_g, diff_kn_g, diff_lambda_q1, diff_lambda_k1, diff_lambda_q2, diff_lambda_k2, diff_sub_g, diff_w_out, sg_w_in, sg_ln_g, sg_ln_b, sg_w_s, sg_b_s, sg_w_out, moe_router, moe_w_gate, moe_w_up, moe_w_down):
    b, n_lat, d = x.shape
    n_ctx = ctx.shape[1]
    depth = ada_w.shape[0]
    assert n_lat % ROW_TILE == 0 and n_ctx == ROW_TILE and n_lat % n_ctx == 0

    rows = -(-(b + 1) // 8) * 8
    cc = jnp.concatenate([c, c_ctx[None], jnp.zeros((rows - b - 1, d), F32)], axis=0)
    mods = _ada(cc, ada_w, ada_b).reshape(depth, rows, 6, d)
    xs = jnp.concatenate([x, ctx], axis=1)

    tabs_a = _rope_tables(n_lat, n_ctx, MLA_ROPE, (MLA_NOPE,))
    tabs_b = _rope_tables(n_lat, n_ctx, DIFF_HEAD_DIM, (0, DIFF_HEAD_DIM))

    for i in range(depth):
        kind, j = i % N_MIXERS, i // N_MIXERS
        last = i == depth - 1
        with_ctx = not last
        q_tiles = (n_lat + (n_ctx if with_ctx else 0)) // ROW_TILE
        mod = jnp.stack([mods[i, :b], jnp.broadcast_to(mods[i, b], (b, 6, d))], axis=1)
        g_mix = norm_mix_g[i].reshape(1, d)
        if kind == 0:
            w = _mla_weights(mla_w_in[j], mla_q_norm_g[j], mla_w_uq[j], mla_kv_norm_g[j], mla_w_ukv[j],
                             mla_qn_g[j], mla_kn_g[j])
            q, k, v = _mla_in(xs, g_mix, mod, w, tabs_a, n_lat)
            a = _mla_attn(q, k, v, n_lat, q_tiles)
            xs = _out_proj(xs, a, mla_w_out[j].astype(BF16), mod, n_lat, q_tiles)
        elif kind == 1:
            lam_init = 0.8 - 0.6 * math.exp(-0.3 * i)
            tile2 = lambda g: jnp.tile(g, 2).reshape(1, LANE)
            w = {"win": diff_w_in[j].astype(BF16), "qn": tile2(diff_qn_g[j]), "kn": tile2(diff_kn_g[j])}
            q0, q1, k, v = _diff_in(xs, g_mix, mod, w, tabs_b, n_lat)
            wa = {"lq1": diff_lambda_q1[j].reshape(1, -1), "lk1": diff_lambda_k1[j].reshape(1, -1),
                  "lq2": diff_lambda_q2[j].reshape(1, -1), "lk2": diff_lambda_k2[j].reshape(1, -1),
                  "sub": diff_sub_g[j].reshape(1, -1)}
            a = _diff_attn(q0, q1, k, v, wa, n_lat, q_tiles, lam_init)
            xs = _out_proj(xs, a, diff_w_out[j].astype(BF16), mod, n_lat, q_tiles)
        else:
            w = {"win": sg_w_in[j].astype(BF16), "lng": sg_ln_g[j].reshape(1, -1), "lnb": sg_ln_b[j].reshape(1, -1),
                 "ws": sg_w_s[j].astype(BF16), "bs": sg_b_s[j].T, "wout": sg_w_out[j].astype(BF16)}
            xs = _sg(xs, g_mix, mod, w, n_lat)
        xs = _moe(xs, norm_ffn_g[i].reshape(1, d), mod, moe_router[i], moe_w_gate[i].astype(BF16),
                  moe_w_up[i].astype(BF16), moe_w_down[i].astype(BF16), n_lat, with_ctx, last)
    return xs
```

```python
import functools
import math

import jax
import jax.numpy as jnp
from jax import lax
from jax.experimental import pallas as pl
from jax.experimental.pallas import tpu as pltpu

F32 = jnp.float32
BF16 = jnp.bfloat16

GRID_W = 64
ROPE_BASE = 10000.0
EPS = 1e-6
N_MIXERS = 3

MLA_HEADS = 16
MLA_Q_LORA = 512
MLA_KV_LORA = 256
MLA_NOPE = 64
MLA_ROPE = 32
MLA_V = 64
MLA_QK = MLA_NOPE + MLA_ROPE

DIFF_HEADS = 8
DIFF_HEAD_DIM = 64
DIFF_V_DIM = 2 * DIFF_HEAD_DIM

SG_CHUNK = 128
SG_GROUPS = 8

N_EXPERTS = 16
CAPACITY_FACTOR = 2

LANE = 128
ROW_TILE = 256
PREFIX_CHUNK = 256
GATHER_CHUNK = 1024
MLA_HEADS_PER_STEP = 4
DIFF_HEADS_PER_STEP = 2
LOG2E = math.log2(math.e)
VMEM_LIMIT = 56 * 1024 * 1024


def _cparams(n_axes):
    return pltpu.CompilerParams(dimension_semantics=("arbitrary",) * n_axes, vmem_limit_bytes=VMEM_LIMIT)


def _const_spec(shape):
    nd = len(shape)
    return pl.BlockSpec(shape, lambda *_: (0,) * nd)


def _dot(a, b):
    return jnp.dot(a, b, preferred_element_type=F32)


def _dot_nt(a, b):
    return lax.dot_general(a, b, (((1,), (1,)), ((), ())), preferred_element_type=F32)


def _rms(x, n):
    return x * lax.rsqrt(jnp.sum(x * x, axis=-1, keepdims=True) * (1.0 / n) + EPS)


def _norm_mod(x, g, shift, scale):
    return _rms(x, x.shape[-1]) * g * (1.0 + scale) + shift


def _silu(x):
    return x / (1.0 + jnp.exp(-x))


def _gelu_tanh(x):
    c = math.sqrt(2.0 / math.pi)
    return 0.5 * x * (1.0 + jnp.tanh(c * (x + 0.044715 * (x * x * x))))


def _rope(x, cos, sin_a, sin_b, half):
    return x * cos + pltpu.roll(x, LANE - half, 1) * sin_a + pltpu.roll(x, half, 1) * sin_b


def _ada_kernel(c_ref, w_ref, b_ref, o_ref):
    s = _silu(c_ref[...])
    o_ref[0] = jnp.dot(s, w_ref[0], preferred_element_type=F32, precision=lax.Precision.HIGHEST) + b_ref[0]


def _ada(cc, ada_w, ada_b):
    depth, d, six_d = ada_w.shape
    rows = cc.shape[0]
    tn = 1536
    return pl.pallas_call(
        _ada_kernel,
        grid=(depth, six_d // tn),
        in_specs=[
            _const_spec((rows, d)),
            pl.BlockSpec((1, d, tn), lambda i, j: (i, 0, j)),
            pl.BlockSpec((1, 1, tn), lambda i, j: (i, 0, j)),
        ],
        out_specs=pl.BlockSpec((1, rows, tn), lambda i, j: (i, 0, j)),
        out_shape=jax.ShapeDtypeStruct((depth, rows, six_d), F32),
        compiler_params=_cparams(2),
        name="ada",
    )(cc, ada_w, ada_b.reshape(depth, 1, six_d))


def _mod_spec(n_lat_tiles, d):
    return pl.BlockSpec((1, 1, 6, d), lambda b, i: (b, jnp.minimum(i // n_lat_tiles, 1), 0, 0))


def _mla_in_kernel(x_ref, g_ref, mod_ref, win_ref, qng_ref, wuq_ref, wuqp_ref, kvng_ref, wuk_ref, wuv_ref,
                   aq_ref, bq_ref, ak_ref, bk_ref, q_ref, k_ref, v_ref):
    mod = mod_ref[0, 0]
    h = _norm_mod(x_ref[0], g_ref[...], mod[0:1], mod[1:2])
    down = _dot(h.astype(BF16), win_ref[...])
    cq = (_rms(down[:, :MLA_Q_LORA], MLA_Q_LORA) * qng_ref[...]).astype(BF16)
    ckv = (_rms(down[:, MLA_Q_LORA:MLA_Q_LORA + MLA_KV_LORA], MLA_KV_LORA) * kvng_ref[...]).astype(BF16)
    q = _dot(cq, wuq_ref[...])
    qp = _dot(cq, wuqp_ref[...])
    kn = _dot(ckv, wuk_ref[...])
    v = _dot(ckv, wuv_ref[...])
    is_v = (lax.broadcasted_iota(jnp.int32, v.shape, 1) & (LANE - 1)) < MLA_V
    v_ref[0] = jnp.where(is_v, v, 1.0).astype(BF16)
    lat = MLA_Q_LORA + MLA_KV_LORA
    kr, krp = down[:, lat:lat + LANE], down[:, lat + LANE:]
    aq, bq, ak, bk = aq_ref[...], bq_ref[...], ak_ref[...], bk_ref[...]
    krot = krp * bk
    inv_n = 1.0 / MLA_QK
    for hd in range(MLA_HEADS):
        sl = slice(hd * LANE, (hd + 1) * LANE)
        qh = q[:, sl]
        rq = lax.rsqrt(jnp.sum(qh * qh, axis=-1, keepdims=True) * inv_n + EPS)
        q_ref[0, :, sl] = ((qh * aq + qp[:, sl] * bq) * rq).astype(BF16)
        kh = kn[:, sl] + kr
        rk = lax.rsqrt(jnp.sum(kh * kh, axis=-1, keepdims=True) * inv_n + EPS)
        k_ref[0, :, sl] = ((kh * ak + krot) * rk).astype(BF16)


def _mla_in(x, g, mod, w, tabs, n_lat):
    b, t, d = x.shape
    tm = ROW_TILE
    hw = MLA_HEADS * LANE
    row = lambda bb, i: (bb, i, 0)
    tab = pl.BlockSpec((tm, LANE), lambda bb, i: (i, 0))
    consts = [w["win"], w["qng"], w["wuq"], w["wuqp"], w["kvng"], w["wuk"], w["wuv"]]
    return pl.pallas_call(
        _mla_in_kernel,
        grid=(b, t // tm),
        in_specs=[pl.BlockSpec((1, tm, d), row), _const_spec((1, d)), _mod_spec(n_lat // tm, d)]
        + [_const_spec(a.shape) for a in consts] + [tab] * 4,
        out_specs=[pl.BlockSpec((1, tm, hw), row)] * 3,
        out_shape=[jax.ShapeDtypeStruct((b, t, hw), BF16)] * 3,
        compiler_params=_cparams(2),
        name="mla_in",
    )(x, g, mod, *consts, *tabs)


def _diff_in_kernel(x_ref, g_ref, mod_ref, win_ref, qn_ref, kn_ref, cos_ref, sa_ref, sb_ref,
                    q0_ref, q1_ref, k_ref, v_ref):
    mod = mod_ref[0, 0]
    d = x_ref.shape[-1]
    h = _norm_mod(x_ref[0], g_ref[...], mod[0:1], mod[1:2])
    qkv = _dot(h.astype(BF16), win_ref[...])
    v_ref[0] = qkv[:, 2 * d:].astype(BF16)
    cos, sa, sb = cos_ref[...], sa_ref[...], sb_ref[...]
    lo = lax.broadcasted_iota(jnp.int32, (x_ref.shape[1], LANE), 1) < DIFF_HEAD_DIM
    scale = DIFF_HEAD_DIM ** -0.5 * LOG2E

    def norm2(xh, gains):
        sq = xh * xh
        s_lo = jnp.sum(jnp.where(lo, sq, 0.0), axis=-1, keepdims=True)
        s_hi = jnp.sum(jnp.where(lo, 0.0, sq), axis=-1, keepdims=True)
        ms = jnp.where(lo, s_lo, s_hi) * (1.0 / DIFF_HEAD_DIM)
        return xh * lax.rsqrt(ms + EPS) * gains

    for hd in range(DIFF_HEADS):
        sl = slice(hd * LANE, (hd + 1) * LANE)
        qh = _rope(norm2(qkv[:, sl], qn_ref[...]), cos, sa, sb, DIFF_HEAD_DIM // 2) * scale
        q0_ref[0, :, sl] = jnp.where(lo, qh, 0.0).astype(BF16)
        q1_ref[0, :, sl] = jnp.where(lo, 0.0, qh).astype(BF16)
        kh = _rope(norm2(qkv[:, d + hd * LANE:d + (hd + 1) * LANE], kn_ref[...]), cos, sa, sb, DIFF_HEAD_DIM // 2)
        k_ref[0, :, sl] = kh.astype(BF16)


def _diff_in(x, g, mod, w, tabs, n_lat):
    b, t, d = x.shape
    tm = ROW_TILE
    row = lambda bb, i: (bb, i, 0)
    tab = pl.BlockSpec((tm, LANE), lambda bb, i: (i, 0))
    consts = [w["win"], w["qn"], w["kn"]]
    out = jax.ShapeDtypeStruct((b, t, d), BF16)
    return pl.pallas_call(
        _diff_in_kernel,
        grid=(b, t // tm),
        in_specs=[pl.BlockSpec((1, tm, d), row), _const_spec((1, d)), _mod_spec(n_lat // tm, d)]
        + [_const_spec(a.shape) for a in consts] + [tab, tab, tab],
        out_specs=[pl.BlockSpec((1, tm, d), row)] * 4,
        out_shape=[out] * 4,
        compiler_params=_cparams(2),
        name="diff_in",
    )(x, g, mod, *consts, *tabs)


def _exp2_scores(q, k):
    s = _dot_nt(q, k)
    return jnp.exp2(s - jnp.max(s, axis=-1, keepdims=True)).astype(BF16)


def _mla_attn_kernel(q_ref, k_ref, v_ref, o_ref, *, n_lat):
    tq = q_ref.shape[1]
    low = lax.broadcasted_iota(jnp.int32, (tq, LANE), 1) < MLA_V

    def run(ks):
        for pr in range(MLA_HEADS_PER_STEP // 2):
            halves = []
            for hd in (2 * pr, 2 * pr + 1):
                sl = slice(hd * LANE, (hd + 1) * LANE)
                r = _dot(_exp2_scores(q_ref[0, :, sl], k_ref[0, ks, sl]), v_ref[0, ks, sl])
                halves.append(r / pltpu.roll(r, LANE - MLA_V, 1))
            pair = jnp.where(low, halves[0], pltpu.roll(halves[1], MLA_V, 1))
            o_ref[0, :, pr * LANE:(pr + 1) * LANE] = pair.astype(o_ref.dtype)

    is_lat = pl.program_id(2) * tq < n_lat
    pl.when(is_lat)(lambda: run(slice(None)))
    pl.when(jnp.logical_not(is_lat))(lambda: run(slice(n_lat, None)))


def _mla_attn(q, k, v, n_lat, n_q_tiles):
    b, t, _ = q.shape
    tq = ROW_TILE
    hs = MLA_HEADS_PER_STEP
    kv = pl.BlockSpec((1, t, hs * LANE), lambda bb, hg, i: (bb, 0, hg))
    return pl.pallas_call(
        functools.partial(_mla_attn_kernel, n_lat=n_lat),
        grid=(b, MLA_HEADS // hs, n_q_tiles),
        in_specs=[pl.BlockSpec((1, tq, hs * LANE), lambda bb, hg, i: (bb, i, hg)), kv, kv],
        out_specs=pl.BlockSpec((1, tq, hs * MLA_V), lambda bb, hg, i: (bb, i, hg)),
        out_shape=jax.ShapeDtypeStruct((b, t, MLA_HEADS * MLA_V), BF16),
        compiler_params=_cparams(3),
        name="mla_attn",
    )(q, k, v)


def _diff_attn_kernel(q0_ref, q1_ref, k_ref, v_ref, lq1_ref, lk1_ref, lq2_ref, lk2_ref, sub_ref, o_ref,
                      *, n_lat, lam_init):
    tq = q0_ref.shape[1]
    lam = (jnp.exp(jnp.sum(lq1_ref[...] * lk1_ref[...], axis=-1, keepdims=True))
           - jnp.exp(jnp.sum(lq2_ref[...] * lk2_ref[...], axis=-1, keepdims=True)) + lam_init)

    def run(k0):
        for a in range(DIFF_HEADS_PER_STEP):
            sl = slice(a * LANE, (a + 1) * LANE)
            k = k_ref[0, k0:, sl]
            v = v_ref[0, k0:, sl]
            outs = []
            for q_ref in (q0_ref, q1_ref):
                s = _dot_nt(q_ref[0, :, sl], k)
                p = jnp.exp2(s - jnp.max(s, axis=-1, keepdims=True))
                outs.append(_dot(p.astype(BF16), v) / jnp.sum(p, axis=-1, keepdims=True))
            o = outs[0] - lam * outs[1]
            o_ref[0, :, sl] = (_rms(o, DIFF_V_DIM) * sub_ref[...] * (1.0 - lam_init)).astype(o_ref.dtype)

    is_lat = pl.program_id(2) * tq < n_lat
    pl.when(is_lat)(lambda: run(0))
    pl.when(jnp.logical_not(is_lat))(lambda: run(n_lat))


def _diff_attn(q0, q1, k, v, w, n_lat, n_q_tiles, lam_init):
    b, t, d = k.shape
    tq = ROW_TILE
    hs = DIFF_HEADS_PER_STEP
    kv = pl.BlockSpec((1, t, hs * LANE), lambda bb, hg, i: (bb, 0, hg))
    qo = pl.BlockSpec((1, tq, hs * LANE), lambda bb, hg, i: (bb, i, hg))
    consts = [w["lq1"], w["lk1"], w["lq2"], w["lk2"], w["sub"]]
    return pl.pallas_call(
        functools.partial(_diff_attn_kernel, n_lat=n_lat, lam_init=lam_init),
        grid=(b, DIFF_HEADS // hs, n_q_tiles),
        in_specs=[qo, qo, kv, kv] + [_const_spec(a.shape) for a in consts],
        out_specs=qo,
        out_shape=jax.ShapeDtypeStruct((b, t, d), BF16),
        compiler_params=_cparams(3),
        name="diff_attn",
    )(q0, q1, k, v, *consts)


def _out_proj_kernel(x_ref, a_ref, w_ref, mod_ref, o_ref):
    gate = mod_ref[0, 0][2:3]
    o_ref[0] = x_ref[0] + gate * _dot(a_ref[0], w_ref[...])


def _out_proj(x, a, w, mod, n_lat, n_tiles):
    b, t, d = x.shape
    tm = ROW_TILE
    row = lambda bb, i: (bb, i, 0)
    return pl.pallas_call(
        _out_proj_kernel,
        grid=(b, n_tiles),
        in_specs=[pl.BlockSpec((1, tm, d), row), pl.BlockSpec((1, tm, a.shape[-1]), row), _const_spec(w.shape),
                  _mod_spec(n_lat // tm, d)],
        out_specs=pl.BlockSpec((1, tm, d), row),
        out_shape=jax.ShapeDtypeStruct(x.shape, F32),
        input_output_aliases={0: 0},
        compiler_params=_cparams(2),
        name="out_proj",
    )(x, a, w, mod)


def _sg_kernel(x_ref, g_ref, mod_ref, win_ref, lng_ref, lnb_ref, ws_ref, bs_ref, wout_ref, o_ref, gated_ref):
    mod = mod_ref[0, 0]
    x = x_ref[0]
    tm = x.shape[0]
    width = lng_ref.shape[-1]
    gdim = width // SG_GROUPS
    h = _norm_mod(x, g_ref[...], mod[0:1], mod[1:2])
    z = _gelu_tanh(_dot(h.astype(BF16), win_ref[...]))
    u, v = z[:, :width], z[:, width:]
    mu = jnp.mean(v, axis=-1, keepdims=True)
    vc = v - mu
    var = jnp.mean(vc * vc, axis=-1, keepdims=True)
    vn = (vc * lax.rsqrt(var + EPS) * lng_ref[...] + lnb_ref[...]).astype(BF16)
    for c in range(tm // SG_CHUNK):
        rows = slice(c * SG_CHUNK, (c + 1) * SG_CHUNK)
        for gi in range(SG_GROUPS):
            cols = slice(gi * gdim, (gi + 1) * gdim)
            mixed = _dot(ws_ref[gi], vn[rows, cols]) + bs_ref[:, gi:gi + 1]
            gated_ref[rows, cols] = (u[rows, cols] * mixed).astype(BF16)
    o_ref[0] = x + mod[2:3] * _dot(gated_ref[...], wout_ref[...])


def _sg(x, g, mod, w, n_lat):
    b, t, d = x.shape
    tm = ROW_TILE
    row = lambda bb, i: (bb, i, 0)
    consts = [w["win"], w["lng"], w["lnb"], w["ws"], w["bs"], w["wout"]]
    return pl.pallas_call(
        _sg_kernel,
        grid=(b, t // tm),
        in_specs=[pl.BlockSpec((1, tm, d), row), _const_spec((1, d)), _mod_spec(n_lat // tm, d)]
        + [_const_spec(a.shape) for a in consts],
        out_specs=pl.BlockSpec((1, tm, d), row),
        out_shape=jax.ShapeDtypeStruct(x.shape, F32),
        scratch_shapes=[pltpu.VMEM((tm, w["lng"].shape[-1]), BF16)],
        input_output_aliases={0: 0},
        compiler_params=_cparams(2),
        name="chunk_mlp",
    )(x, g, mod, *consts)


def _router_kernel(x_ref, g_ref, mod_ref, rhi_ref, rlo_ref, h_ref, aff_ref):
    mod = mod_ref[0, 0]
    h = _norm_mod(x_ref[0], g_ref[...], mod[3:4], mod[4:5])
    h_hi = h.astype(BF16)
    h_lo = (h - h_hi.astype(F32)).astype(BF16)
    h_ref[0] = h_hi
    logits = _dot(h_hi, rhi_ref[...]) + (_dot(h_lo, rhi_ref[...]) + _dot(h_hi, rlo_ref[...]))
    e = jnp.exp(logits - jnp.max(logits, axis=-1, keepdims=True))
    aff_ref[0] = e / jnp.sum(e, axis=-1, keepdims=True)


def _router(x, g, mod, router, n_lat, n_tiles):
    b, t, d = x.shape
    tm = ROW_TILE
    ne = router.shape[-1]
    row = lambda bb, i: (bb, i, 0)
    r_hi = router.astype(BF16)
    r_lo = (router - r_hi.astype(F32)).astype(BF16)
    return pl.pallas_call(
        _router_kernel,
        grid=(b, n_tiles),
        in_specs=[pl.BlockSpec((1, tm, d), row), _const_spec((1, d)), _mod_spec(n_lat // tm, d),
                  _const_spec(router.shape), _const_spec(router.shape)],
        out_specs=[pl.BlockSpec((1, tm, d), row), pl.BlockSpec((1, tm, ne), row)],
        out_shape=[jax.ShapeDtypeStruct((b, n_tiles * tm, d), BF16), jax.ShapeDtypeStruct((b, n_tiles * tm, ne), F32)],
        compiler_params=_cparams(2),
        name="router",
    )(x, g, mod, r_hi, r_lo)


def _prefix_counts(mask, tri):
    out = []
    carry = jnp.zeros((mask.shape[0], 1), F32)
    for c in range(mask.shape[1] // PREFIX_CHUNK):
        m = mask[:, c * PREFIX_CHUNK:(c + 1) * PREFIX_CHUNK]
        out.append(_dot(m.astype(BF16), tri) + carry)
        carry = carry + jnp.sum(m, axis=-1, keepdims=True)
    return out


def _topk_kernel(aff_ref, pos_ref, *, cap):
    a = aff_ref[0]
    ne = a.shape[0]
    bits = pltpu.bitcast(a, jnp.int32)

    def step(i, lo):
        cand = lo | jnp.left_shift(jnp.int32(1), 30 - i)
        cnt = jnp.sum(jnp.where(bits >= cand, 1.0, 0.0), axis=-1, keepdims=True)
        return jnp.where(cnt >= cap, cand, lo)

    thr = lax.fori_loop(0, 31, step, jnp.zeros((ne, 1), jnp.int32))
    gt = jnp.where(bits > thr, 1.0, 0.0)
    eq = jnp.where(bits == thr, 1.0, 0.0)
    room = cap - jnp.sum(gt, axis=-1, keepdims=True)
    ri = lax.broadcasted_iota(jnp.int32, (PREFIX_CHUNK, PREFIX_CHUNK), 0)
    ci = lax.broadcasted_iota(jnp.int32, (PREFIX_CHUNK, PREFIX_CHUNK), 1)
    tri = jnp.where(ri <= ci, 1.0, 0.0).astype(BF16)
    eq_rank = _prefix_counts(eq, tri)
    sel = jnp.concatenate([
        jnp.maximum(gt[:, c * PREFIX_CHUNK:(c + 1) * PREFIX_CHUNK],
                    jnp.where(r <= room, eq[:, c * PREFIX_CHUNK:(c + 1) * PREFIX_CHUNK], 0.0))
        for c, r in enumerate(eq_rank)], axis=-1)
    for c, r in enumerate(_prefix_counts(sel, tri)):
        cols = slice(c * PREFIX_CHUNK, (c + 1) * PREFIX_CHUNK)
        pos_ref[0, :, cols] = jnp.where(sel[:, cols] > 0.0, r - 1.0, -1.0).astype(jnp.int32)


def _topk(aff_t, cap):
    b, ne, n = aff_t.shape
    spec = pl.BlockSpec((1, ne, n), lambda bb: (bb, 0, 0))
    return pl.pallas_call(
        functools.partial(_topk_kernel, cap=cap),
        grid=(b,),
        in_specs=[spec],
        out_specs=spec,
        out_shape=jax.ShapeDtypeStruct((b, ne, n), jnp.int32),
        compiler_params=_cparams(1),
        name="topk",
    )(aff_t)


def _moe_ffn_kernel(h_ref, pos_ref, aff_ref, wg_ref, wu_ref, wd_ref, y_ref, *, cap):
    bg, n, d = h_ref.shape
    kc = min(GATHER_CHUNK, n)
    slot = lax.broadcasted_iota(jnp.int32, (cap, kc), 0)
    xs, gates = [], []
    for bb in range(bg):
        xb, gb = jnp.zeros((cap, d), F32), jnp.zeros((cap, 1), F32)
        for c in range(n // kc):
            cols = slice(c * kc, (c + 1) * kc)
            hit = pos_ref[bb, 0, :, cols] == slot
            xb = xb + _dot(jnp.where(hit, 1.0, 0.0).astype(BF16), h_ref[bb, cols, :])
            gb = gb + jnp.sum(jnp.where(hit, aff_ref[bb, 0, :, cols], 0.0), axis=-1, keepdims=True)
        xs.append(xb)
        gates.append(gb)
    xg = jnp.concatenate(xs, axis=0).astype(BF16)
    gate = jnp.concatenate(gates, axis=0)
    hid = _silu(_dot(xg, wg_ref[0, 0])) * _dot(xg, wu_ref[0, 0])
    y = _dot(hid.astype(BF16), wd_ref[0, 0]) * gate
    for bb in range(bg):
        y_ref[bb, 0] = y[bb * cap:(bb + 1) * cap].astype(BF16)


def _moe_ffn(h, pos, aff_t, wg, wu, wd, layer, set_block, n_set, cap, bg):
    b, _, d = h.shape
    _, ne, _, ff = wg.shape
    sel = pl.BlockSpec((bg, 1, 1, n_set), lambda i, e: (i, e, 0, 0))
    return pl.pallas_call(
        functools.partial(_moe_ffn_kernel, cap=cap),
        grid=(b // bg, ne),
        in_specs=[pl.BlockSpec((bg, n_set, d), lambda i, e: (i, set_block, 0)), sel, sel,
                  pl.BlockSpec((1, 1, d, ff), lambda i, e: (layer, e, 0, 0)),
                  pl.BlockSpec((1, 1, d, ff), lambda i, e: (layer, e, 0, 0)),
                  pl.BlockSpec((1, 1, ff, d), lambda i, e: (layer, e, 0, 0))],
        out_specs=pl.BlockSpec((bg, 1, cap, d), lambda i, e: (i, e, 0, 0)),
        out_shape=jax.ShapeDtypeStruct((b, ne, cap, d), BF16),
        compiler_params=_cparams(2),
        name="moe_ffn",
    )(h, pos.reshape(b, ne, 1, n_set), aff_t.reshape(b, ne, 1, n_set), wg, wu, wd)


def _moe_combine_kernel(x_ref, pos_ref, y_ref, mod_ref, o_ref, *, cap):
    pos = pos_ref[0]
    tn, ne = pos.shape
    lane = lax.broadcasted_iota(jnp.int32, (tn, cap), 1)
    acc = jnp.zeros(x_ref.shape[1:], F32)
    for e in range(ne):
        onehot = jnp.where(pos[:, e:e + 1] == lane, 1.0, 0.0).astype(BF16)
        acc = acc + _dot(onehot, y_ref[0, e])
    o_ref[0] = x_ref[0] + mod_ref[0, 0][5:6] * acc


def _moe_combine(x, pos_t, y, mod, tile_off, n_set, tn, mod_sel, out_rows):
    b, t, d = x.shape
    _, ne, cap, _ = y.shape
    alias = out_rows == t
    out_off = tile_off if alias else 0
    return pl.pallas_call(
        functools.partial(_moe_combine_kernel, cap=cap),
        grid=(b, n_set // tn),
        in_specs=[pl.BlockSpec((1, tn, d), lambda bb, i: (bb, i + tile_off, 0)),
                  pl.BlockSpec((1, tn, ne), lambda bb, i: (bb, i, 0)),
                  pl.BlockSpec((1, ne, cap, d), lambda bb, i: (bb, 0, 0, 0)),
                  pl.BlockSpec((1, 1, 6, d), lambda bb, i: (bb, mod_sel, 0, 0))],
        out_specs=pl.BlockSpec((1, tn, d), lambda bb, i: (bb, i + out_off, 0)),
        out_shape=jax.ShapeDtypeStruct((b, out_rows, d), F32),
        input_output_aliases={0: 0} if alias else {},
        compiler_params=_cparams(2),
        name="moe_combine",
    )(x, pos_t, y, mod)


def _moe(x, g, mod, router, wg, wu, wd, layer, n_lat, with_ctx, final):
    b, t, d = x.shape
    n_ctx = t - n_lat
    tm = ROW_TILE
    h, aff = _router(x, g, mod, router, n_lat, (t if with_ctx else n_lat) // tm)
    aff_t = jnp.swapaxes(aff, 1, 2)
    sets = [(0, n_lat, 0, min(512, n_lat), 1)]
    if with_ctx:
        sets.append((n_lat, n_ctx, 1, n_ctx, b))
    for start, n_set, mod_sel, tn, bg in sets:
        cap = CAPACITY_FACTOR * n_set // N_EXPERTS
        a_set = aff_t[:, :, start:start + n_set]
        pos = _topk(a_set, cap)
        y = _moe_ffn(h, pos, a_set, wg, wu, wd, layer, start // n_set, n_set, cap, bg)
        x = _moe_combine(x, jnp.swapaxes(pos, 1, 2), y, mod, start // tn, n_set, tn, mod_sel,
                         n_lat if final else t)
    return x


def _rope_tables(n_lat, n_ctx, rot_dim, starts):
    n_rows = n_lat // GRID_W
    rows = jnp.repeat(jnp.arange(n_rows, dtype=F32), GRID_W)
    cols = jnp.tile(jnp.arange(GRID_W, dtype=F32), n_rows)
    n_freq = rot_dim // 4
    inv_freq = ROPE_BASE ** (-jnp.arange(n_freq, dtype=F32) / n_freq)
    ang = jnp.concatenate([rows[:, None] * inv_freq, cols[:, None] * inv_freq], axis=-1)
    half = rot_dim // 2
    cos_l, sin_l = jnp.cos(ang), jnp.sin(ang)
    cos = jnp.ones((n_lat, LANE), F32)
    sa = jnp.zeros((n_lat, LANE), F32)
    sb = jnp.zeros((n_lat, LANE), F32)
    for s in starts:
        cos = cos.at[:, s:s + half].set(cos_l).at[:, s + half:s + rot_dim].set(cos_l)
        sa = sa.at[:, s:s + half].set(-sin_l)
        sb = sb.at[:, s + half:s + rot_dim].set(sin_l)
    pad = lambda a, v: jnp.concatenate([a, jnp.full((n_ctx, LANE), v, F32)], axis=0)
    return pad(cos, 1.0), pad(sa, 0.0), pad(sb, 0.0)


def _pad_heads(w, heads, width):
    k = w.shape[0]
    w = w.reshape(k, heads, width)
    return jnp.pad(w, ((0, 0), (0, 0), (0, LANE - width))).reshape(k, heads * LANE)


def _mla_partner(a):
    half = MLA_ROPE // 2
    lane = jnp.arange(LANE)
    first = (lane >= MLA_NOPE) & (lane < MLA_NOPE + half)
    second = (lane >= MLA_NOPE + half) & (lane < MLA_QK)
    return jnp.where(first, jnp.roll(a, -half, axis=-1), jnp.where(second, jnp.roll(a, half, axis=-1), 0.0))


def _mla_weights(w_in, q_norm_g, w_uq, kv_norm_g, w_ukv):
    d = w_in.shape[0]
    lat = MLA_Q_LORA + MLA_KV_LORA
    rope_block = jnp.concatenate([jnp.zeros((d, MLA_NOPE), F32), w_in[:, lat:],
                                  jnp.zeros((d, LANE - MLA_QK), F32)], axis=1)
    ukv = w_ukv.reshape(MLA_KV_LORA, MLA_HEADS, MLA_NOPE + MLA_V)
    wuq = _pad_heads(w_uq, MLA_HEADS, MLA_QK)
    wuqp = _mla_partner(wuq.reshape(-1, MLA_HEADS, LANE)).reshape(wuq.shape)
    return {
        "win": jnp.concatenate([w_in[:, :lat], rope_block, _mla_partner(rope_block)], axis=1).astype(BF16),
        "qng": q_norm_g.reshape(1, -1),
        "wuq": wuq.astype(BF16),
        "wuqp": wuqp.astype(BF16),
        "kvng": kv_norm_g.reshape(1, -1),
        "wuk": _pad_heads(ukv[:, :, :MLA_NOPE].reshape(MLA_KV_LORA, -1), MLA_HEADS, MLA_NOPE).astype(BF16),
        "wuv": _pad_heads(ukv[:, :, MLA_NOPE:].reshape(MLA_KV_LORA, -1), MLA_HEADS, MLA_V).astype(BF16),
    }


def _mla_tables(tabs, qn_g, kn_g):
    cos, sa, sb = tabs
    out = []
    for g, scale in ((qn_g, MLA_QK ** -0.5 * LOG2E), (kn_g, 1.0)):
        gp = jnp.pad(g, (0, LANE - MLA_QK))
        out += [cos * gp * scale, (sa + sb) * _mla_partner(gp) * scale]
    return out


def kernel(x, c, ctx, c_ctx, ada_w, ada_b, norm_mix_g, norm_ffn_g, mla_w_in, mla_q_norm_g, mla_w_uq, mla_kv_norm_g, mla_w_ukv, mla_qn_g, mla_kn_g, mla_w_out, diff_w_in, diff_qn_g, diff_kn_g, diff_lambda_q1, diff_lambda_k1, diff_lambda_q2, diff_lambda_k2, diff_sub_g, diff_w_out, sg_w_in, sg_ln_g, sg_ln_b, sg_w_s, sg_b_s, sg_w_out, moe_router, moe_w_gate, moe_w_up, moe_w_down):
    b, n_lat, d = x.shape
    n_ctx = ctx.shape[1]
    depth = ada_w.shape[0]
    assert n_lat % ROW_TILE == 0 and n_ctx == ROW_TILE and n_lat % n_ctx == 0

    rows = -(-(b + 1) // 8) * 8
    cc = jnp.concatenate([c, c_ctx[None], jnp.zeros((rows - b - 1, d), F32)], axis=0)
    mods = _ada(cc, ada_w, ada_b).reshape(depth, rows, 6, d)
    xs = jnp.concatenate([x, ctx], axis=1)

    w_gate, w_up, w_down = moe_w_gate.astype(BF16), moe_w_up.astype(BF16), moe_w_down.astype(BF16)
    tabs_a = _rope_tables(n_lat, n_ctx, MLA_ROPE, (MLA_NOPE,))
    tabs_b = _rope_tables(n_lat, n_ctx, DIFF_HEAD_DIM, (0, DIFF_HEAD_DIM))

    for i in range(depth):
        kind, j = i % N_MIXERS, i // N_MIXERS
        last = i == depth - 1
        with_ctx = not last
        q_tiles = (n_lat + (n_ctx if with_ctx else 0)) // ROW_TILE
        mod = jnp.stack([mods[i, :b], jnp.broadcast_to(mods[i, b], (b, 6, d))], axis=1)
        g_mix = norm_mix_g[i].reshape(1, d)
        if kind == 0:
            w = _mla_weights(mla_w_in[j], mla_q_norm_g[j], mla_w_uq[j], mla_kv_norm_g[j], mla_w_ukv[j])
            q, k, v = _mla_in(xs, g_mix, mod, w, _mla_tables(tabs_a, mla_qn_g[j], mla_kn_g[j]), n_lat)
            a = _mla_attn(q, k, v, n_lat, q_tiles)
            xs = _out_proj(xs, a, mla_w_out[j].astype(BF16), mod, n_lat, q_tiles)
        elif kind == 1:
            lam_init = 0.8 - 0.6 * math.exp(-0.3 * i)
            tile2 = lambda g: jnp.tile(g, 2).reshape(1, LANE)
            w = {"win": diff_w_in[j].astype(BF16), "qn": tile2(diff_qn_g[j]), "kn": tile2(diff_kn_g[j])}
            q0, q1, k, v = _diff_in(xs, g_mix, mod, w, tabs_b, n_lat)
            wa = {"lq1": diff_lambda_q1[j].reshape(1, -1), "lk1": diff_lambda_k1[j].reshape(1, -1),
                  "lq2": diff_lambda_q2[j].reshape(1, -1), "lk2": diff_lambda_k2[j].reshape(1, -1),
                  "sub": diff_sub_g[j].reshape(1, -1)}
            a = _diff_attn(q0, q1, k, v, wa, n_lat, q_tiles, lam_init)
            xs = _out_proj(xs, a, diff_w_out[j].astype(BF16), mod, n_lat, q_tiles)
        else:
            w = {"win": sg_w_in[j].astype(BF16), "lng": sg_ln_g[j].reshape(1, -1), "lnb": sg_ln_b[j].reshape(1, -1),
                 "ws": sg_w_s[j].astype(BF16), "bs": sg_b_s[j].T, "wout": sg_w_out[j].astype(BF16)}
            xs = _sg(xs, g_mix, mod, w, n_lat)
        xs = _moe(xs, norm_ffn_g[i].reshape(1, d), mod, moe_router[i], w_gate, w_up, w_down, i, n_lat, with_ctx, last)
    return xs
```

```python
import functools
import math

import jax
import jax.numpy as jnp
from jax import lax
from jax.experimental import pallas as pl
from jax.experimental.pallas import tpu as pltpu

F32 = jnp.float32
BF16 = jnp.bfloat16

GRID_W = 64
ROPE_BASE = 10000.0
EPS = 1e-6
N_MIXERS = 3

MLA_HEADS = 16
MLA_Q_LORA = 512
MLA_KV_LORA = 256
MLA_NOPE = 64
MLA_ROPE = 32
MLA_V = 64
MLA_QK = MLA_NOPE + MLA_ROPE

DIFF_HEADS = 8
DIFF_HEAD_DIM = 64
DIFF_V_DIM = 2 * DIFF_HEAD_DIM

SG_CHUNK = 128
SG_GROUPS = 8

N_EXPERTS = 16
CAPACITY_FACTOR = 2

LANE = 128
BF16_ROWS = 16
ROW_TILE = 256
PREFIX_CHUNK = 256
GATHER_CHUNK = 1024
MLA_HEADS_PER_STEP = 4
DIFF_HEADS_PER_STEP = 4
ATTN_Q_TILE = 256
LOG2E = math.log2(math.e)
VMEM_LIMIT = 56 * 1024 * 1024


def _cparams(n_axes):
    return pltpu.CompilerParams(dimension_semantics=("arbitrary",) * n_axes, vmem_limit_bytes=VMEM_LIMIT)


def _const_spec(shape):
    nd = len(shape)
    return pl.BlockSpec(shape, lambda *_: (0,) * nd)


def _dot(a, b):
    return jnp.dot(a, b, preferred_element_type=F32)


def _dot_nt(a, b):
    return lax.dot_general(a, b, (((1,), (1,)), ((), ())), preferred_element_type=F32)


def _rms(x, n):
    return x * lax.rsqrt(jnp.sum(x * x, axis=-1, keepdims=True) * (1.0 / n) + EPS)


def _norm_mod(x, g, shift, scale):
    return _rms(x, x.shape[-1]) * g * (1.0 + scale) + shift


def _silu(x):
    return x / (1.0 + jnp.exp(-x))


def _gelu_tanh(x):
    c = math.sqrt(2.0 / math.pi)
    return 0.5 * x * (1.0 + jnp.tanh(c * (x + 0.044715 * (x * x * x))))


def _rope(x, cos, sin_a, sin_b, half):
    return x * cos + pltpu.roll(x, LANE - half, 1) * sin_a + pltpu.roll(x, half, 1) * sin_b


def _ada_kernel(c_ref, w_ref, b_ref, o_ref):
    s = _silu(c_ref[...])
    o_ref[0] = jnp.dot(s, w_ref[0], preferred_element_type=F32, precision=lax.Precision.HIGHEST) + b_ref[0]


def _ada(cc, ada_w, ada_b):
    depth, d, six_d = ada_w.shape
    rows = cc.shape[0]
    tn = 1536
    return pl.pallas_call(
        _ada_kernel,
        grid=(depth, six_d // tn),
        in_specs=[
            _const_spec((rows, d)),
            pl.BlockSpec((1, d, tn), lambda i, j: (i, 0, j)),
            pl.BlockSpec((1, 1, tn), lambda i, j: (i, 0, j)),
        ],
        out_specs=pl.BlockSpec((1, rows, tn), lambda i, j: (i, 0, j)),
        out_shape=jax.ShapeDtypeStruct((depth, rows, six_d), F32),
        compiler_params=_cparams(2),
        name="ada",
    )(cc, ada_w, ada_b.reshape(depth, 1, six_d))


def _mod_spec(n_lat_tiles, d):
    return pl.BlockSpec((1, 1, 6, d), lambda b, i: (b, jnp.minimum(i // n_lat_tiles, 1), 0, 0))


def _mla_in_kernel(x_ref, g_ref, mod_ref, win_ref, qng_ref, wuq_ref, wuqp_ref, kvng_ref, wuk_ref, wuv_ref,
                   aq_ref, bq_ref, ak_ref, bk_ref, q_ref, k_ref, v_ref):
    mod = mod_ref[0, 0]
    h = _norm_mod(x_ref[0], g_ref[...], mod[0:1], mod[1:2])
    down = _dot(h.astype(BF16), win_ref[...])
    cq = (_rms(down[:, :MLA_Q_LORA], MLA_Q_LORA) * qng_ref[...]).astype(BF16)
    ckv = (_rms(down[:, MLA_Q_LORA:MLA_Q_LORA + MLA_KV_LORA], MLA_KV_LORA) * kvng_ref[...]).astype(BF16)
    q = _dot(cq, wuq_ref[...])
    qp = _dot(cq, wuqp_ref[...])
    kn = _dot(ckv, wuk_ref[...])
    v = _dot(ckv, wuv_ref[...])
    is_v = (lax.broadcasted_iota(jnp.int32, v.shape, 1) & (LANE - 1)) < MLA_V
    v_ref[0] = jnp.where(is_v, v, 1.0).astype(BF16)
    lat = MLA_Q_LORA + MLA_KV_LORA
    kr, krp = down[:, lat:lat + LANE], down[:, lat + LANE:]
    aq, bq, ak, bk = aq_ref[...], bq_ref[...], ak_ref[...], bk_ref[...]
    krot = krp * bk
    inv_n = 1.0 / MLA_QK
    for hd in range(MLA_HEADS):
        sl = slice(hd * LANE, (hd + 1) * LANE)
        qh = q[:, sl]
        rq = lax.rsqrt(jnp.sum(qh * qh, axis=-1, keepdims=True) * inv_n + EPS)
        q_ref[0, :, sl] = ((qh * aq + qp[:, sl] * bq) * rq).astype(BF16)
        kh = kn[:, sl] + kr
        rk = lax.rsqrt(jnp.sum(kh * kh, axis=-1, keepdims=True) * inv_n + EPS)
        k_ref[0, :, sl] = ((kh * ak + krot) * rk).astype(BF16)


def _mla_in(x, g, mod, w, tabs, n_lat):
    b, t, d = x.shape
    tm = ROW_TILE
    hw = MLA_HEADS * LANE
    row = lambda bb, i: (bb, i, 0)
    tab = pl.BlockSpec((tm, LANE), lambda bb, i: (i, 0))
    consts = [w["win"], w["qng"], w["wuq"], w["wuqp"], w["kvng"], w["wuk"], w["wuv"]]
    return pl.pallas_call(
        _mla_in_kernel,
        grid=(b, t // tm),
        in_specs=[pl.BlockSpec((1, tm, d), row), _const_spec((1, d)), _mod_spec(n_lat // tm, d)]
        + [_const_spec(a.shape) for a in consts] + [tab] * 4,
        out_specs=[pl.BlockSpec((1, tm, hw), row)] * 3,
        out_shape=[jax.ShapeDtypeStruct((b, t, hw), BF16)] * 3,
        compiler_params=_cparams(2),
        name="mla_in",
    )(x, g, mod, *consts, *tabs)


def _diff_in_kernel(x_ref, g_ref, mod_ref, win_ref, qn_ref, kn_ref, cos_ref, sa_ref, sb_ref,
                    q0_ref, q1_ref, k_ref, v_ref):
    mod = mod_ref[0, 0]
    d = x_ref.shape[-1]
    h = _norm_mod(x_ref[0], g_ref[...], mod[0:1], mod[1:2])
    qkv = _dot(h.astype(BF16), win_ref[...])
    v_ref[0] = qkv[:, 2 * d:].astype(BF16)
    cos, sa, sb = cos_ref[...], sa_ref[...], sb_ref[...]
    lo = lax.broadcasted_iota(jnp.int32, (x_ref.shape[1], LANE), 1) < DIFF_HEAD_DIM
    scale = DIFF_HEAD_DIM ** -0.5 * LOG2E

    def norm2(xh, gains):
        sq = xh * xh
        s_lo = jnp.sum(jnp.where(lo, sq, 0.0), axis=-1, keepdims=True)
        s_hi = jnp.sum(jnp.where(lo, 0.0, sq), axis=-1, keepdims=True)
        ms = jnp.where(lo, s_lo, s_hi) * (1.0 / DIFF_HEAD_DIM)
        return xh * lax.rsqrt(ms + EPS) * gains

    for hd in range(DIFF_HEADS):
        sl = slice(hd * LANE, (hd + 1) * LANE)
        qh = _rope(norm2(qkv[:, sl], qn_ref[...]), cos, sa, sb, DIFF_HEAD_DIM // 2) * scale
        q0_ref[0, :, sl] = jnp.where(lo, qh, 0.0).astype(BF16)
        q1_ref[0, :, sl] = jnp.where(lo, 0.0, qh).astype(BF16)
        kh = _rope(norm2(qkv[:, d + hd * LANE:d + (hd + 1) * LANE], kn_ref[...]), cos, sa, sb, DIFF_HEAD_DIM // 2)
        k_ref[0, :, sl] = kh.astype(BF16)


def _diff_in(x, g, mod, w, tabs, n_lat):
    b, t, d = x.shape
    tm = ROW_TILE
    row = lambda bb, i: (bb, i, 0)
    tab = pl.BlockSpec((tm, LANE), lambda bb, i: (i, 0))
    consts = [w["win"], w["qn"], w["kn"]]
    out = jax.ShapeDtypeStruct((b, t, d), BF16)
    return pl.pallas_call(
        _diff_in_kernel,
        grid=(b, t // tm),
        in_specs=[pl.BlockSpec((1, tm, d), row), _const_spec((1, d)), _mod_spec(n_lat // tm, d)]
        + [_const_spec(a.shape) for a in consts] + [tab, tab, tab],
        out_specs=[pl.BlockSpec((1, tm, d), row)] * 4,
        out_shape=[out] * 4,
        compiler_params=_cparams(2),
        name="diff_in",
    )(x, g, mod, *consts, *tabs)


def _exp2_scores(q, k):
    s = _dot_nt(q, k)
    return jnp.exp2(s - jnp.max(s, axis=-1, keepdims=True)).astype(BF16)


def _mla_attn_kernel(q_ref, k_ref, v_ref, *rest):
    o_ref = rest[-1]
    tq = q_ref.shape[1]
    low = lax.broadcasted_iota(jnp.int32, (tq, LANE), 1) < MLA_V
    for pr in range(MLA_HEADS_PER_STEP // 2):
        halves = []
        for hd in (2 * pr, 2 * pr + 1):
            sl = slice(hd * LANE, (hd + 1) * LANE)
            r = _dot(_exp2_scores(q_ref[0, :, sl], k_ref[0, :, sl]), v_ref[0, :, sl])
            halves.append(r / pltpu.roll(r, LANE - MLA_V, 1))
        pair = jnp.where(low, halves[0], pltpu.roll(halves[1], MLA_V, 1))
        o_ref[0, :, pr * LANE:(pr + 1) * LANE] = pair.astype(o_ref.dtype)


def _attn_calls(kernel, name, qs, kvs, consts, n_lat, with_ctx, heads, hs, out_width):
    b, t, _ = kvs[0].shape
    n_ctx = t - n_lat
    ctx_blk = n_lat // n_ctx
    out_shape = jax.ShapeDtypeStruct((b, t, heads // hs * out_width), BF16)
    cspecs = [_const_spec(a.shape) for a in consts]

    def call(tq, q_map, kv_rows, kv_map, n_tiles, prev):
        q_specs = [pl.BlockSpec((1, tq, a.shape[-1] // (heads // hs)), q_map) for a in qs]
        kv_specs = [pl.BlockSpec((1, kv_rows, a.shape[-1] // (heads // hs)), kv_map) for a in kvs]
        extra = [] if prev is None else [pl.BlockSpec(memory_space=pl.ANY)]
        n_in = len(qs) + len(kvs) + len(consts)
        return pl.pallas_call(
            kernel,
            grid=(b, heads // hs, n_tiles),
            in_specs=q_specs + kv_specs + cspecs + extra,
            out_specs=pl.BlockSpec((1, tq, out_width), q_map),
            out_shape=out_shape,
            input_output_aliases={} if prev is None else {n_in: 0},
            compiler_params=_cparams(3),
            name=name,
        )(*qs, *kvs, *consts, *([] if prev is None else [prev]))

    tq = min(ATTN_Q_TILE, n_lat)
    out = call(tq, lambda bb, hg, i: (bb, i, hg), t, lambda bb, hg, i: (bb, 0, hg), n_lat // tq, None)
    if with_ctx:
        ctx_map = lambda bb, hg, i: (bb, ctx_blk, hg)
        out = call(n_ctx, ctx_map, n_ctx, ctx_map, 1, out)
    return out


def _diff_attn_kernel(q0_ref, q1_ref, k_ref, v_ref, lq1_ref, lk1_ref, lq2_ref, lk2_ref, sub_ref, *rest, lam_init):
    o_ref = rest[-1]
    lam = (jnp.exp(jnp.sum(lq1_ref[...] * lk1_ref[...], axis=-1, keepdims=True))
           - jnp.exp(jnp.sum(lq2_ref[...] * lk2_ref[...], axis=-1, keepdims=True)) + lam_init)
    for a in range(DIFF_HEADS_PER_STEP):
        sl = slice(a * LANE, (a + 1) * LANE)
        k, v = k_ref[0, :, sl], v_ref[0, :, sl]
        outs = []
        for q_ref in (q0_ref, q1_ref):
            s = _dot_nt(q_ref[0, :, sl], k)
            p = jnp.exp2(s - jnp.max(s, axis=-1, keepdims=True))
            outs.append(_dot(p.astype(BF16), v) / jnp.sum(p, axis=-1, keepdims=True))
        o = outs[0] - lam * outs[1]
        o_ref[0, :, sl] = (_rms(o, DIFF_V_DIM) * sub_ref[...] * (1.0 - lam_init)).astype(o_ref.dtype)


def _out_proj_kernel(x_ref, a_ref, w_ref, mod_ref, o_ref):
    gate = mod_ref[0, 0][2:3]
    o_ref[0] = x_ref[0] + gate * _dot(a_ref[0], w_ref[...])


def _out_proj(x, a, w, mod, n_lat, n_tiles):
    b, t, d = x.shape
    tm = ROW_TILE
    row = lambda bb, i: (bb, i, 0)
    return pl.pallas_call(
        _out_proj_kernel,
        grid=(b, n_tiles),
        in_specs=[pl.BlockSpec((1, tm, d), row), pl.BlockSpec((1, tm, a.shape[-1]), row), _const_spec(w.shape),
                  _mod_spec(n_lat // tm, d)],
        out_specs=pl.BlockSpec((1, tm, d), row),
        out_shape=jax.ShapeDtypeStruct(x.shape, F32),
        input_output_aliases={0: 0},
        compiler_params=_cparams(2),
        name="out_proj",
    )(x, a, w, mod)


def _sg_kernel(x_ref, g_ref, mod_ref, win_ref, lng_ref, lnb_ref, ws_ref, bs_ref, wout_ref, o_ref, gated_ref):
    mod = mod_ref[0, 0]
    x = x_ref[0]
    tm = x.shape[0]
    width = lng_ref.shape[-1]
    gdim = width // SG_GROUPS
    h = _norm_mod(x, g_ref[...], mod[0:1], mod[1:2])
    z = _gelu_tanh(_dot(h.astype(BF16), win_ref[...]))
    u, v = z[:, :width], z[:, width:]
    mu = jnp.mean(v, axis=-1, keepdims=True)
    vc = v - mu
    var = jnp.mean(vc * vc, axis=-1, keepdims=True)
    vn = (vc * lax.rsqrt(var + EPS) * lng_ref[...] + lnb_ref[...]).astype(BF16)
    for c in range(tm // SG_CHUNK):
        rows = slice(c * SG_CHUNK, (c + 1) * SG_CHUNK)
        for gi in range(SG_GROUPS):
            cols = slice(gi * gdim, (gi + 1) * gdim)
            mixed = _dot(ws_ref[gi], vn[rows, cols]) + bs_ref[:, gi:gi + 1]
            gated_ref[rows, cols] = (u[rows, cols] * mixed).astype(BF16)
    o_ref[0] = x + mod[2:3] * _dot(gated_ref[...], wout_ref[...])


def _sg(x, g, mod, w, n_lat):
    b, t, d = x.shape
    tm = ROW_TILE
    row = lambda bb, i: (bb, i, 0)
    consts = [w["win"], w["lng"], w["lnb"], w["ws"], w["bs"], w["wout"]]
    return pl.pallas_call(
        _sg_kernel,
        grid=(b, t // tm),
        in_specs=[pl.BlockSpec((1, tm, d), row), _const_spec((1, d)), _mod_spec(n_lat // tm, d)]
        + [_const_spec(a.shape) for a in consts],
        out_specs=pl.BlockSpec((1, tm, d), row),
        out_shape=jax.ShapeDtypeStruct(x.shape, F32),
        scratch_shapes=[pltpu.VMEM((tm, w["lng"].shape[-1]), BF16)],
        input_output_aliases={0: 0},
        compiler_params=_cparams(2),
        name="chunk_mlp",
    )(x, g, mod, *consts)


def _router_kernel(x_ref, g_ref, mod_ref, rhi_ref, rlo_ref, h_ref, aff_ref):
    mod = mod_ref[0, 0]
    h = _norm_mod(x_ref[0], g_ref[...], mod[3:4], mod[4:5])
    h_hi = h.astype(BF16)
    h_lo = (h - h_hi.astype(F32)).astype(BF16)
    h_ref[0] = h_hi
    logits = _dot(h_hi, rhi_ref[...]) + (_dot(h_lo, rhi_ref[...]) + _dot(h_hi, rlo_ref[...]))
    e = jnp.exp(logits - jnp.max(logits, axis=-1, keepdims=True))
    aff_ref[0] = e / jnp.sum(e, axis=-1, keepdims=True)


def _router(x, g, mod, router, n_lat, n_tiles):
    b, t, d = x.shape
    tm = ROW_TILE
    ne = router.shape[-1]
    row = lambda bb, i: (bb, i, 0)
    r_hi = router.astype(BF16)
    r_lo = (router - r_hi.astype(F32)).astype(BF16)
    return pl.pallas_call(
        _router_kernel,
        grid=(b, n_tiles),
        in_specs=[pl.BlockSpec((1, tm, d), row), _const_spec((1, d)), _mod_spec(n_lat // tm, d),
                  _const_spec(router.shape), _const_spec(router.shape)],
        out_specs=[pl.BlockSpec((1, tm, d), row), pl.BlockSpec((1, tm, ne), row)],
        out_shape=[jax.ShapeDtypeStruct((b, n_tiles * tm, d), BF16), jax.ShapeDtypeStruct((b, n_tiles * tm, ne), F32)],
        compiler_params=_cparams(2),
        name="router",
    )(x, g, mod, r_hi, r_lo)


def _prefix_counts(mask, tri):
    out = []
    carry = jnp.zeros((mask.shape[0], 1), F32)
    before = [carry]
    for c in range(mask.shape[1] // PREFIX_CHUNK):
        m = mask[:, c * PREFIX_CHUNK:(c + 1) * PREFIX_CHUNK]
        out.append(_dot(m.astype(BF16), tri) + carry)
        carry = carry + jnp.sum(m, axis=-1, keepdims=True)
        before.append(carry)
    return out, before


def _topk_kernel(aff_ref, pos_ref, cnt_ref, *, cap):
    a = aff_ref[0]
    ne = a.shape[0]
    bits = pltpu.bitcast(a, jnp.int32)

    def step(i, lo):
        cand = lo | jnp.left_shift(jnp.int32(1), 30 - i)
        cnt = jnp.sum(jnp.where(bits >= cand, 1.0, 0.0), axis=-1, keepdims=True)
        return jnp.where(cnt >= cap, cand, lo)

    thr = lax.fori_loop(0, 31, step, jnp.zeros((ne, 1), jnp.int32))
    gt = jnp.where(bits > thr, 1.0, 0.0)
    eq = jnp.where(bits == thr, 1.0, 0.0)
    room = cap - jnp.sum(gt, axis=-1, keepdims=True)
    ri = lax.broadcasted_iota(jnp.int32, (PREFIX_CHUNK, PREFIX_CHUNK), 0)
    ci = lax.broadcasted_iota(jnp.int32, (PREFIX_CHUNK, PREFIX_CHUNK), 1)
    tri = jnp.where(ri <= ci, 1.0, 0.0).astype(BF16)
    eq_rank, _ = _prefix_counts(eq, tri)
    sel = jnp.concatenate([
        jnp.maximum(gt[:, c * PREFIX_CHUNK:(c + 1) * PREFIX_CHUNK],
                    jnp.where(r <= room, eq[:, c * PREFIX_CHUNK:(c + 1) * PREFIX_CHUNK], 0.0))
        for c, r in enumerate(eq_rank)], axis=-1)
    sel_rank, before = _prefix_counts(sel, tri)
    for c, r in enumerate(sel_rank):
        cols = slice(c * PREFIX_CHUNK, (c + 1) * PREFIX_CHUNK)
        pos_ref[0, :, cols] = jnp.where(sel[:, cols] > 0.0, r - 1.0, -1.0).astype(jnp.int32)
    for c, cnt in enumerate(before):
        cnt_ref[0, :, c:c + 1] = cnt.astype(jnp.int32)


def _topk(aff_t, cap):
    b, ne, n = aff_t.shape
    spec = pl.BlockSpec((1, ne, n), lambda bb: (bb, 0, 0))
    n_cnt = n // PREFIX_CHUNK + 1
    return pl.pallas_call(
        functools.partial(_topk_kernel, cap=cap),
        grid=(b,),
        in_specs=[spec],
        out_specs=[spec, pl.BlockSpec((1, ne, n_cnt), lambda bb: (bb, 0, 0))],
        out_shape=[jax.ShapeDtypeStruct((b, ne, n), jnp.int32), jax.ShapeDtypeStruct((b, ne, n_cnt), jnp.int32)],
        compiler_params=_cparams(1),
        name="topk",
    )(aff_t)


def _moe_ffn_kernel(h_ref, pos_ref, aff_ref, wg_ref, wu_ref, wd_ref, y_ref, *, cap):
    bg, n, d = h_ref.shape
    kc = min(GATHER_CHUNK, n)
    slot = lax.broadcasted_iota(jnp.int32, (cap, kc), 0)
    xs, gates = [], []
    for bb in range(bg):
        xb, gb = jnp.zeros((cap, d), F32), jnp.zeros((cap, 1), F32)
        for c in range(n // kc):
            cols = slice(c * kc, (c + 1) * kc)
            hit = pos_ref[bb, 0, :, cols] == slot
            xb = xb + _dot(jnp.where(hit, 1.0, 0.0).astype(BF16), h_ref[bb, cols, :])
            gb = gb + jnp.sum(jnp.where(hit, aff_ref[bb, 0, :, cols], 0.0), axis=-1, keepdims=True)
        xs.append(xb)
        gates.append(gb)
    xg = jnp.concatenate(xs, axis=0).astype(BF16)
    gate = jnp.concatenate(gates, axis=0)
    hid = _silu(_dot(xg, wg_ref[0, 0])) * _dot(xg, wu_ref[0, 0])
    y = _dot(hid.astype(BF16), wd_ref[0, 0]) * gate
    for bb in range(bg):
        y_ref[bb, 0] = y[bb * cap:(bb + 1) * cap].astype(BF16)


def _moe_ffn(h, pos, aff_t, wg, wu, wd, layer, set_block, n_set, cap, bg):
    b, _, d = h.shape
    _, ne, _, ff = wg.shape
    sel = pl.BlockSpec((bg, 1, 1, n_set), lambda i, e: (i, e, 0, 0))
    return pl.pallas_call(
        functools.partial(_moe_ffn_kernel, cap=cap),
        grid=(b // bg, ne),
        in_specs=[pl.BlockSpec((bg, n_set, d), lambda i, e: (i, set_block, 0)), sel, sel,
                  pl.BlockSpec((1, 1, d, ff), lambda i, e: (layer, e, 0, 0)),
                  pl.BlockSpec((1, 1, d, ff), lambda i, e: (layer, e, 0, 0)),
                  pl.BlockSpec((1, 1, ff, d), lambda i, e: (layer, e, 0, 0))],
        out_specs=pl.BlockSpec((bg, 1, cap, d), lambda i, e: (i, e, 0, 0)),
        out_shape=jax.ShapeDtypeStruct((b, ne, cap, d), BF16),
        compiler_params=_cparams(2),
        name="moe_ffn",
    )(h, pos.reshape(b, ne, 1, n_set), aff_t.reshape(b, ne, 1, n_set), wg, wu, wd)


def _moe_combine_kernel(cnt_ref, x_ref, pos_ref, y_ref, mod_ref, o_ref, *, cap, n_cnt):
    pos = pos_ref[0]
    tn, ne = pos.shape
    win = min(tn, cap)
    gate = mod_ref[0, 0][5:6]
    row = (pl.program_id(0) * ne) * n_cnt + pl.program_id(1)
    lane = lax.broadcasted_iota(jnp.int32, (tn, win), 1)
    acc = jnp.zeros(x_ref.shape[1:], F32)
    overflow = False
    for e in range(ne):
        lo, hi = cnt_ref[row + e * n_cnt], cnt_ref[row + e * n_cnt + 1]
        start = pl.multiple_of(jnp.minimum((lo // BF16_ROWS) * BF16_ROWS, cap - win), BF16_ROWS)
        overflow = jnp.logical_or(overflow, hi > start + win)
        onehot = jnp.where(pos[:, e:e + 1] - start == lane, 1.0, 0.0).astype(BF16)
        acc = acc + _dot(onehot, y_ref[0, e, pl.ds(start, win), :])
    o_ref[0] = x_ref[0] + gate * acc

    @pl.when(overflow)
    def _():
        lane_all = lax.broadcasted_iota(jnp.int32, (tn, cap), 1)
        full = jnp.zeros(x_ref.shape[1:], F32)
        for e in range(ne):
            onehot = jnp.where(pos[:, e:e + 1] == lane_all, 1.0, 0.0).astype(BF16)
            full = full + _dot(onehot, y_ref[0, e])
        o_ref[0] = x_ref[0] + gate * full


def _moe_combine(x, pos_t, cnt, y, mod, tile_off, n_set, mod_sel, out_rows):
    b, t, d = x.shape
    _, ne, cap, _ = y.shape
    tn = PREFIX_CHUNK
    n_cnt = cnt.shape[-1]
    alias = out_rows == t
    out_off = tile_off if alias else 0
    return pl.pallas_call(
        functools.partial(_moe_combine_kernel, cap=cap, n_cnt=n_cnt),
        grid_spec=pltpu.PrefetchScalarGridSpec(
            num_scalar_prefetch=1,
            grid=(b, n_set // tn),
            in_specs=[pl.BlockSpec((1, tn, d), lambda bb, i, c: (bb, i + tile_off, 0)),
                      pl.BlockSpec((1, tn, ne), lambda bb, i, c: (bb, i, 0)),
                      pl.BlockSpec((1, ne, cap, d), lambda bb, i, c: (bb, 0, 0, 0)),
                      pl.BlockSpec((1, 1, 6, d), lambda bb, i, c: (bb, mod_sel, 0, 0))],
            out_specs=pl.BlockSpec((1, tn, d), lambda bb, i, c: (bb, i + out_off, 0))),
        out_shape=jax.ShapeDtypeStruct((b, out_rows, d), F32),
        input_output_aliases={1: 0} if alias else {},
        compiler_params=_cparams(2),
        name="moe_combine",
    )(cnt.reshape(-1), x, pos_t, y, mod)


def _moe(x, g, mod, router, wg, wu, wd, layer, n_lat, with_ctx, final):
    b, t, d = x.shape
    n_ctx = t - n_lat
    tm = ROW_TILE
    h, aff = _router(x, g, mod, router, n_lat, (t if with_ctx else n_lat) // tm)
    aff_t = jnp.swapaxes(aff, 1, 2)
    sets = [(0, n_lat, 0, 1)]
    if with_ctx:
        sets.append((n_lat, n_ctx, 1, b))
    for start, n_set, mod_sel, bg in sets:
        cap = CAPACITY_FACTOR * n_set // N_EXPERTS
        a_set = aff_t[:, :, start:start + n_set]
        pos, cnt = _topk(a_set, cap)
        y = _moe_ffn(h, pos, a_set, wg, wu, wd, layer, start // n_set, n_set, cap, bg)
        x = _moe_combine(x, jnp.swapaxes(pos, 1, 2), cnt, y, mod, start // PREFIX_CHUNK, n_set, mod_sel,
                         n_lat if final else t)
    return x


def _rope_tables(n_lat, n_ctx, rot_dim, starts):
    n_rows = n_lat // GRID_W
    rows = jnp.repeat(jnp.arange(n_rows, dtype=F32), GRID_W)
    cols = jnp.tile(jnp.arange(GRID_W, dtype=F32), n_rows)
    n_freq = rot_dim // 4
    inv_freq = ROPE_BASE ** (-jnp.arange(n_freq, dtype=F32) / n_freq)
    ang = jnp.concatenate([rows[:, None] * inv_freq, cols[:, None] * inv_freq], axis=-1)
    half = rot_dim // 2
    cos_l, sin_l = jnp.cos(ang), jnp.sin(ang)
    cos = jnp.ones((n_lat, LANE), F32)
    sa = jnp.zeros((n_lat, LANE), F32)
    sb = jnp.zeros((n_lat, LANE), F32)
    for s in starts:
        cos = cos.at[:, s:s + half].set(cos_l).at[:, s + half:s + rot_dim].set(cos_l)
        sa = sa.at[:, s:s + half].set(-sin_l)
        sb = sb.at[:, s + half:s + rot_dim].set(sin_l)
    pad = lambda a, v: jnp.concatenate([a, jnp.full((n_ctx, LANE), v, F32)], axis=0)
    return pad(cos, 1.0), pad(sa, 0.0), pad(sb, 0.0)


def _pad_heads(w, heads, width):
    k = w.shape[0]
    w = w.reshape(k, heads, width)
    return jnp.pad(w, ((0, 0), (0, 0), (0, LANE - width))).reshape(k, heads * LANE)


def _mla_partner(a):
    half = MLA_ROPE // 2
    lane = jnp.arange(LANE)
    first = (lane >= MLA_NOPE) & (lane < MLA_NOPE + half)
    second = (lane >= MLA_NOPE + half) & (lane < MLA_QK)
    return jnp.where(first, jnp.roll(a, -half, axis=-1), jnp.where(second, jnp.roll(a, half, axis=-1), 0.0))


def _mla_weights(w_in, q_norm_g, w_uq, kv_norm_g, w_ukv):
    d = w_in.shape[0]
    lat = MLA_Q_LORA + MLA_KV_LORA
    rope_block = jnp.concatenate([jnp.zeros((d, MLA_NOPE), F32), w_in[:, lat:],
                                  jnp.zeros((d, LANE - MLA_QK), F32)], axis=1)
    ukv = w_ukv.reshape(MLA_KV_LORA, MLA_HEADS, MLA_NOPE + MLA_V)
    wuq = _pad_heads(w_uq, MLA_HEADS, MLA_QK)
    wuqp = _mla_partner(wuq.reshape(-1, MLA_HEADS, LANE)).reshape(wuq.shape)
    return {
        "win": jnp.concatenate([w_in[:, :lat], rope_block, _mla_partner(rope_block)], axis=1).astype(BF16),
        "qng": q_norm_g.reshape(1, -1),
        "wuq": wuq.astype(BF16),
        "wuqp": wuqp.astype(BF16),
        "kvng": kv_norm_g.reshape(1, -1),
        "wuk": _pad_heads(ukv[:, :, :MLA_NOPE].reshape(MLA_KV_LORA, -1), MLA_HEADS, MLA_NOPE).astype(BF16),
        "wuv": _pad_heads(ukv[:, :, MLA_NOPE:].reshape(MLA_KV_LORA, -1), MLA_HEADS, MLA_V).astype(BF16),
    }


def _mla_tables(tabs, qn_g, kn_g):
    cos, sa, sb = tabs
    out = []
    for g, scale in ((qn_g, MLA_QK ** -0.5 * LOG2E), (kn_g, 1.0)):
        gp = jnp.pad(g, (0, LANE - MLA_QK))
        out += [cos * gp * scale, (sa + sb) * _mla_partner(gp) * scale]
    return out


def kernel(x, c, ctx, c_ctx, ada_w, ada_b, norm_mix_g, norm_ffn_g, mla_w_in, mla_q_norm_g, mla_w_uq, mla_kv_norm_g, mla_w_ukv, mla_qn_g, mla_kn_g, mla_w_out, diff_w_in, diff_qn_g, diff_kn_g, diff_lambda_q1, diff_lambda_k1, diff_lambda_q2, diff_lambda_k2, diff_sub_g, diff_w_out, sg_w_in, sg_ln_g, sg_ln_b, sg_w_s, sg_b_s, sg_w_out, moe_router, moe_w_gate, moe_w_up, moe_w_down):
    b, n_lat, d = x.shape
    n_ctx = ctx.shape[1]
    depth = ada_w.shape[0]
    assert n_lat % ROW_TILE == 0 and n_ctx == ROW_TILE and n_lat % n_ctx == 0

    rows = -(-(b + 1) // 8) * 8
    cc = jnp.concatenate([c, c_ctx[None], jnp.zeros((rows - b - 1, d), F32)], axis=0)
    mods = _ada(cc, ada_w, ada_b).reshape(depth, rows, 6, d)
    xs = jnp.concatenate([x, ctx], axis=1)

    w_gate, w_up, w_down = moe_w_gate.astype(BF16), moe_w_up.astype(BF16), moe_w_down.astype(BF16)
    tabs_a = _rope_tables(n_lat, n_ctx, MLA_ROPE, (MLA_NOPE,))
    tabs_b = _rope_tables(n_lat, n_ctx, DIFF_HEAD_DIM, (0, DIFF_HEAD_DIM))

    for i in range(depth):
        kind, j = i % N_MIXERS, i // N_MIXERS
        last = i == depth - 1
        with_ctx = not last
        q_tiles = (n_lat + (n_ctx if with_ctx else 0)) // ROW_TILE
        mod = jnp.stack([mods[i, :b], jnp.broadcast_to(mods[i, b], (b, 6, d))], axis=1)
        g_mix = norm_mix_g[i].reshape(1, d)
        if kind == 0:
            w = _mla_weights(mla_w_in[j], mla_q_norm_g[j], mla_w_uq[j], mla_kv_norm_g[j], mla_w_ukv[j])
            q, k, v = _mla_in(xs, g_mix, mod, w, _mla_tables(tabs_a, mla_qn_g[j], mla_kn_g[j]), n_lat)
            a = _attn_calls(_mla_attn_kernel, "mla_attn", [q], [k, v], [], n_lat, with_ctx,
                            MLA_HEADS, MLA_HEADS_PER_STEP, MLA_HEADS_PER_STEP * MLA_V)
            xs = _out_proj(xs, a, mla_w_out[j].astype(BF16), mod, n_lat, q_tiles)
        elif kind == 1:
            lam_init = 0.8 - 0.6 * math.exp(-0.3 * i)
            tile2 = lambda g: jnp.tile(g, 2).reshape(1, LANE)
            w = {"win": diff_w_in[j].astype(BF16), "qn": tile2(diff_qn_g[j]), "kn": tile2(diff_kn_g[j])}
            q0, q1, k, v = _diff_in(xs, g_mix, mod, w, tabs_b, n_lat)
            wa = [a.reshape(1, -1) for a in (diff_lambda_q1[j], diff_lambda_k1[j], diff_lambda_q2[j],
                                             diff_lambda_k2[j], diff_sub_g[j])]
            a = _attn_calls(functools.partial(_diff_attn_kernel, lam_init=lam_init), "diff_attn", [q0, q1], [k, v],
                            wa, n_lat, with_ctx, DIFF_HEADS, DIFF_HEADS_PER_STEP, DIFF_HEADS_PER_STEP * LANE)
            xs = _out_proj(xs, a, diff_w_out[j].astype(BF16), mod, n_lat, q_tiles)
        else:
            w = {"win": sg_w_in[j].astype(BF16), "lng": sg_ln_g[j].reshape(1, -1), "lnb": sg_ln_b[j].reshape(1, -1),
                 "ws": sg_w_s[j].astype(BF16), "bs": sg_b_s[j].T, "wout": sg_w_out[j].astype(BF16)}
            xs = _sg(xs, g_mix, mod, w, n_lat)
        xs = _moe(xs, norm_ffn_g[i].reshape(1, d), mod, moe_router[i], w_gate, w_up, w_down, i, n_lat, with_ctx, last)
    return xs
```

```python
import functools
import math

import jax
import jax.numpy as jnp
from jax import lax
from jax.experimental import pallas as pl
from jax.experimental.pallas import tpu as pltpu

F32 = jnp.float32
BF16 = jnp.bfloat16

GRID_W = 64
ROPE_BASE = 10000.0
EPS = 1e-6
N_MIXERS = 3

MLA_HEADS = 16
MLA_Q_LORA = 512
MLA_KV_LORA = 256
MLA_NOPE = 64
MLA_ROPE = 32
MLA_V = 64
MLA_QK = MLA_NOPE + MLA_ROPE

DIFF_HEADS = 8
DIFF_HEAD_DIM = 64
DIFF_V_DIM = 2 * DIFF_HEAD_DIM

SG_CHUNK = 128
SG_GROUPS = 8

N_EXPERTS = 16
CAPACITY_FACTOR = 2

LANE = 128
BF16_ROWS = 16
ROW_TILE = 256
PREFIX_CHUNK = 256
GATHER_CHUNK = 1024
GATHER_WINDOW = 64
MLA_HEADS_PER_STEP = 4
DIFF_HEADS_PER_STEP = 4
ATTN_Q_TILE = 256
LOG2E = math.log2(math.e)
VMEM_LIMIT = 56 * 1024 * 1024


def _cparams(n_axes):
    return pltpu.CompilerParams(dimension_semantics=("arbitrary",) * n_axes, vmem_limit_bytes=VMEM_LIMIT)


def _const_spec(shape):
    nd = len(shape)
    return pl.BlockSpec(shape, lambda *_: (0,) * nd)


def _dot(a, b):
    return jnp.dot(a, b, preferred_element_type=F32)


def _dot_nt(a, b):
    return lax.dot_general(a, b, (((1,), (1,)), ((), ())), preferred_element_type=F32)


def _rms(x, n):
    return x * lax.rsqrt(jnp.sum(x * x, axis=-1, keepdims=True) * (1.0 / n) + EPS)


def _norm_mod(x, g, shift, scale):
    return _rms(x, x.shape[-1]) * g * (1.0 + scale) + shift


def _silu(x):
    return x / (1.0 + jnp.exp(-x))


def _gelu_tanh(x):
    c = math.sqrt(2.0 / math.pi)
    return 0.5 * x * (1.0 + jnp.tanh(c * (x + 0.044715 * (x * x * x))))


def _rope(x, cos, sin_a, sin_b, half):
    return x * cos + pltpu.roll(x, LANE - half, 1) * sin_a + pltpu.roll(x, half, 1) * sin_b


def _ada_kernel(c_ref, w_ref, b_ref, o_ref):
    s = _silu(c_ref[...])
    o_ref[0] = jnp.dot(s, w_ref[0], preferred_element_type=F32, precision=lax.Precision.HIGHEST) + b_ref[0]


def _ada(cc, ada_w, ada_b):
    depth, d, six_d = ada_w.shape
    rows = cc.shape[0]
    tn = 1536
    return pl.pallas_call(
        _ada_kernel,
        grid=(depth, six_d // tn),
        in_specs=[
            _const_spec((rows, d)),
            pl.BlockSpec((1, d, tn), lambda i, j: (i, 0, j)),
            pl.BlockSpec((1, 1, tn), lambda i, j: (i, 0, j)),
        ],
        out_specs=pl.BlockSpec((1, rows, tn), lambda i, j: (i, 0, j)),
        out_shape=jax.ShapeDtypeStruct((depth, rows, six_d), F32),
        compiler_params=_cparams(2),
        name="ada",
    )(cc, ada_w, ada_b.reshape(depth, 1, six_d))


def _mod_spec(n_lat_tiles, d):
    return pl.BlockSpec((1, 1, 6, d), lambda b, i: (b, jnp.minimum(i // n_lat_tiles, 1), 0, 0))


def _mla_in_kernel(x_ref, g_ref, mod_ref, win_ref, qng_ref, wuq_ref, wuqp_ref, kvng_ref, wuk_ref, wuv_ref,
                   aq_ref, bq_ref, ak_ref, bk_ref, q_ref, k_ref, v_ref):
    mod = mod_ref[0, 0]
    h = _norm_mod(x_ref[0], g_ref[...], mod[0:1], mod[1:2])
    down = _dot(h.astype(BF16), win_ref[...])
    cq = (_rms(down[:, :MLA_Q_LORA], MLA_Q_LORA) * qng_ref[...]).astype(BF16)
    ckv = (_rms(down[:, MLA_Q_LORA:MLA_Q_LORA + MLA_KV_LORA], MLA_KV_LORA) * kvng_ref[...]).astype(BF16)
    q = _dot(cq, wuq_ref[...])
    qp = _dot(cq, wuqp_ref[...])
    kn = _dot(ckv, wuk_ref[...])
    v = _dot(ckv, wuv_ref[...])
    is_v = (lax.broadcasted_iota(jnp.int32, v.shape, 1) & (LANE - 1)) < MLA_V
    v_ref[0] = jnp.where(is_v, v, 1.0).astype(BF16)
    lat = MLA_Q_LORA + MLA_KV_LORA
    kr, krp = down[:, lat:lat + LANE], down[:, lat + LANE:]
    aq, bq, ak, bk = aq_ref[...], bq_ref[...], ak_ref[...], bk_ref[...]
    krot = krp * bk
    inv_n = 1.0 / MLA_QK
    for hd in range(MLA_HEADS):
        sl = slice(hd * LANE, (hd + 1) * LANE)
        qh = q[:, sl]
        rq = lax.rsqrt(jnp.sum(qh * qh, axis=-1, keepdims=True) * inv_n + EPS)
        q_ref[0, :, sl] = ((qh * aq + qp[:, sl] * bq) * rq).astype(BF16)
        kh = kn[:, sl] + kr
        rk = lax.rsqrt(jnp.sum(kh * kh, axis=-1, keepdims=True) * inv_n + EPS)
        k_ref[0, :, sl] = ((kh * ak + krot) * rk).astype(BF16)


def _mla_in(x, g, mod, w, tabs, n_lat):
    b, t, d = x.shape
    tm = ROW_TILE
    hw = MLA_HEADS * LANE
    row = lambda bb, i: (bb, i, 0)
    tab = pl.BlockSpec((tm, LANE), lambda bb, i: (i, 0))
    consts = [w["win"], w["qng"], w["wuq"], w["wuqp"], w["kvng"], w["wuk"], w["wuv"]]
    return pl.pallas_call(
        _mla_in_kernel,
        grid=(b, t // tm),
        in_specs=[pl.BlockSpec((1, tm, d), row), _const_spec((1, d)), _mod_spec(n_lat // tm, d)]
        + [_const_spec(a.shape) for a in consts] + [tab] * 4,
        out_specs=[pl.BlockSpec((1, tm, hw), row)] * 3,
        out_shape=[jax.ShapeDtypeStruct((b, t, hw), BF16)] * 3,
        compiler_params=_cparams(2),
        name="mla_in",
    )(x, g, mod, *consts, *tabs)


def _diff_in_kernel(x_ref, g_ref, mod_ref, win_ref, qn_ref, kn_ref, cos_ref, sa_ref, sb_ref,
                    q0_ref, q1_ref, k_ref, v_ref):
    mod = mod_ref[0, 0]
    d = x_ref.shape[-1]
    h = _norm_mod(x_ref[0], g_ref[...], mod[0:1], mod[1:2])
    qkv = _dot(h.astype(BF16), win_ref[...])
    v_ref[0] = qkv[:, 2 * d:].astype(BF16)
    cos, sa, sb = cos_ref[...], sa_ref[...], sb_ref[...]
    lo = lax.broadcasted_iota(jnp.int32, (x_ref.shape[1], LANE), 1) < DIFF_HEAD_DIM
    scale = DIFF_HEAD_DIM ** -0.5 * LOG2E

    def norm2(xh, gains):
        sq = xh * xh
        s_lo = jnp.sum(jnp.where(lo, sq, 0.0), axis=-1, keepdims=True)
        s_hi = jnp.sum(jnp.where(lo, 0.0, sq), axis=-1, keepdims=True)
        ms = jnp.where(lo, s_lo, s_hi) * (1.0 / DIFF_HEAD_DIM)
        return xh * lax.rsqrt(ms + EPS) * gains

    for hd in range(DIFF_HEADS):
        sl = slice(hd * LANE, (hd + 1) * LANE)
        qh = _rope(norm2(qkv[:, sl], qn_ref[...]), cos, sa, sb, DIFF_HEAD_DIM // 2) * scale
        q0_ref[0, :, sl] = jnp.where(lo, qh, 0.0).astype(BF16)
        q1_ref[0, :, sl] = jnp.where(lo, 0.0, qh).astype(BF16)
        kh = _rope(norm2(qkv[:, d + hd * LANE:d + (hd + 1) * LANE], kn_ref[...]), cos, sa, sb, DIFF_HEAD_DIM // 2)
        k_ref[0, :, sl] = kh.astype(BF16)


def _diff_in(x, g, mod, w, tabs, n_lat):
    b, t, d = x.shape
    tm = ROW_TILE
    row = lambda bb, i: (bb, i, 0)
    tab = pl.BlockSpec((tm, LANE), lambda bb, i: (i, 0))
    consts = [w["win"], w["qn"], w["kn"]]
    out = jax.ShapeDtypeStruct((b, t, d), BF16)
    return pl.pallas_call(
        _diff_in_kernel,
        grid=(b, t // tm),
        in_specs=[pl.BlockSpec((1, tm, d), row), _const_spec((1, d)), _mod_spec(n_lat // tm, d)]
        + [_const_spec(a.shape) for a in consts] + [tab, tab, tab],
        out_specs=[pl.BlockSpec((1, tm, d), row)] * 4,
        out_shape=[out] * 4,
        compiler_params=_cparams(2),
        name="diff_in",
    )(x, g, mod, *consts, *tabs)


def _exp2_scores(q, k):
    s = _dot_nt(q, k)
    return jnp.exp2(s - jnp.max(s, axis=-1, keepdims=True)).astype(BF16)


def _mla_attn_kernel(q_ref, k_ref, v_ref, *rest):
    o_ref = rest[-1]
    tq = q_ref.shape[1]
    low = lax.broadcasted_iota(jnp.int32, (tq, LANE), 1) < MLA_V
    for pr in range(MLA_HEADS_PER_STEP // 2):
        halves = []
        for hd in (2 * pr, 2 * pr + 1):
            sl = slice(hd * LANE, (hd + 1) * LANE)
            r = _dot(_exp2_scores(q_ref[0, :, sl], k_ref[0, :, sl]), v_ref[0, :, sl])
            halves.append(r / pltpu.roll(r, LANE - MLA_V, 1))
        pair = jnp.where(low, halves[0], pltpu.roll(halves[1], MLA_V, 1))
        o_ref[0, :, pr * LANE:(pr + 1) * LANE] = pair.astype(o_ref.dtype)


def _attn_calls(kernel, name, qs, kvs, consts, n_lat, with_ctx, heads, hs, out_width):
    b, t, _ = kvs[0].shape
    n_ctx = t - n_lat
    ctx_blk = n_lat // n_ctx
    out_shape = jax.ShapeDtypeStruct((b, t, heads // hs * out_width), BF16)
    cspecs = [_const_spec(a.shape) for a in consts]

    def call(tq, q_map, kv_rows, kv_map, n_tiles, prev):
        q_specs = [pl.BlockSpec((1, tq, a.shape[-1] // (heads // hs)), q_map) for a in qs]
        kv_specs = [pl.BlockSpec((1, kv_rows, a.shape[-1] // (heads // hs)), kv_map) for a in kvs]
        extra = [] if prev is None else [pl.BlockSpec(memory_space=pl.ANY)]
        n_in = len(qs) + len(kvs) + len(consts)
        return pl.pallas_call(
            kernel,
            grid=(b, heads // hs, n_tiles),
            in_specs=q_specs + kv_specs + cspecs + extra,
            out_specs=pl.BlockSpec((1, tq, out_width), q_map),
            out_shape=out_shape,
            input_output_aliases={} if prev is None else {n_in: 0},
            compiler_params=_cparams(3),
            name=name,
        )(*qs, *kvs, *consts, *([] if prev is None else [prev]))

    tq = min(ATTN_Q_TILE, n_lat)
    out = call(tq, lambda bb, hg, i: (bb, i, hg), t, lambda bb, hg, i: (bb, 0, hg), n_lat // tq, None)
    if with_ctx:
        ctx_map = lambda bb, hg, i: (bb, ctx_blk, hg)
        out = call(n_ctx, ctx_map, n_ctx, ctx_map, 1, out)
    return out


def _diff_attn_kernel(q0_ref, q1_ref, k_ref, v_ref, lq1_ref, lk1_ref, lq2_ref, lk2_ref, sub_ref, *rest, lam_init):
    o_ref = rest[-1]
    lam = (jnp.exp(jnp.sum(lq1_ref[...] * lk1_ref[...], axis=-1, keepdims=True))
           - jnp.exp(jnp.sum(lq2_ref[...] * lk2_ref[...], axis=-1, keepdims=True)) + lam_init)
    for a in range(DIFF_HEADS_PER_STEP):
        sl = slice(a * LANE, (a + 1) * LANE)
        k, v = k_ref[0, :, sl], v_ref[0, :, sl]
        outs = []
        for q_ref in (q0_ref, q1_ref):
            s = _dot_nt(q_ref[0, :, sl], k)
            p = jnp.exp2(s - jnp.max(s, axis=-1, keepdims=True))
            outs.append(_dot(p.astype(BF16), v) / jnp.sum(p, axis=-1, keepdims=True))
        o = outs[0] - lam * outs[1]
        o_ref[0, :, sl] = (_rms(o, DIFF_V_DIM) * sub_ref[...] * (1.0 - lam_init)).astype(o_ref.dtype)


def _out_proj_kernel(x_ref, a_ref, w_ref, mod_ref, o_ref):
    gate = mod_ref[0, 0][2:3]
    o_ref[0] = x_ref[0] + gate * _dot(a_ref[0], w_ref[...])


def _out_proj(x, a, w, mod, n_lat, n_tiles):
    b, t, d = x.shape
    tm = ROW_TILE
    row = lambda bb, i: (bb, i, 0)
    return pl.pallas_call(
        _out_proj_kernel,
        grid=(b, n_tiles),
        in_specs=[pl.BlockSpec((1, tm, d), row), pl.BlockSpec((1, tm, a.shape[-1]), row), _const_spec(w.shape),
                  _mod_spec(n_lat // tm, d)],
        out_specs=pl.BlockSpec((1, tm, d), row),
        out_shape=jax.ShapeDtypeStruct(x.shape, F32),
        input_output_aliases={0: 0},
        compiler_params=_cparams(2),
        name="out_proj",
    )(x, a, w, mod)


def _sg_kernel(x_ref, g_ref, mod_ref, win_ref, lng_ref, lnb_ref, ws_ref, bs_ref, wout_ref, o_ref, gated_ref):
    mod = mod_ref[0, 0]
    x = x_ref[0]
    tm = x.shape[0]
    width = lng_ref.shape[-1]
    gdim = width // SG_GROUPS
    h = _norm_mod(x, g_ref[...], mod[0:1], mod[1:2])
    z = _gelu_tanh(_dot(h.astype(BF16), win_ref[...]))
    u, v = z[:, :width], z[:, width:]
    mu = jnp.mean(v, axis=-1, keepdims=True)
    vc = v - mu
    var = jnp.mean(vc * vc, axis=-1, keepdims=True)
    vn = (vc * lax.rsqrt(var + EPS) * lng_ref[...] + lnb_ref[...]).astype(BF16)
    for c in range(tm // SG_CHUNK):
        rows = slice(c * SG_CHUNK, (c + 1) * SG_CHUNK)
        for gi in range(SG_GROUPS):
            cols = slice(gi * gdim, (gi + 1) * gdim)
            mixed = _dot(ws_ref[gi], vn[rows, cols]) + bs_ref[:, gi:gi + 1]
            gated_ref[rows, cols] = (u[rows, cols] * mixed).astype(BF16)
    o_ref[0] = x + mod[2:3] * _dot(gated_ref[...], wout_ref[...])


def _sg(x, g, mod, w, n_lat):
    b, t, d = x.shape
    tm = ROW_TILE
    row = lambda bb, i: (bb, i, 0)
    consts = [w["win"], w["lng"], w["lnb"], w["ws"], w["bs"], w["wout"]]
    return pl.pallas_call(
        _sg_kernel,
        grid=(b, t // tm),
        in_specs=[pl.BlockSpec((1, tm, d), row), _const_spec((1, d)), _mod_spec(n_lat // tm, d)]
        + [_const_spec(a.shape) for a in consts],
        out_specs=pl.BlockSpec((1, tm, d), row),
        out_shape=jax.ShapeDtypeStruct(x.shape, F32),
        scratch_shapes=[pltpu.VMEM((tm, w["lng"].shape[-1]), BF16)],
        input_output_aliases={0: 0},
        compiler_params=_cparams(2),
        name="chunk_mlp",
    )(x, g, mod, *consts)


def _router_kernel(x_ref, g_ref, mod_ref, rhi_ref, rlo_ref, h_ref, aff_ref):
    mod = mod_ref[0, 0]
    h = _norm_mod(x_ref[0], g_ref[...], mod[3:4], mod[4:5])
    h_hi = h.astype(BF16)
    h_lo = (h - h_hi.astype(F32)).astype(BF16)
    h_ref[0] = h_hi
    logits = _dot(h_hi, rhi_ref[...]) + (_dot(h_lo, rhi_ref[...]) + _dot(h_hi, rlo_ref[...]))
    e = jnp.exp(logits - jnp.max(logits, axis=-1, keepdims=True))
    aff_ref[0] = e / jnp.sum(e, axis=-1, keepdims=True)


def _router(x, g, mod, router, n_lat, n_tiles):
    b, t, d = x.shape
    tm = ROW_TILE
    ne = router.shape[-1]
    row = lambda bb, i: (bb, i, 0)
    r_hi = router.astype(BF16)
    r_lo = (router - r_hi.astype(F32)).astype(BF16)
    return pl.pallas_call(
        _router_kernel,
        grid=(b, n_tiles),
        in_specs=[pl.BlockSpec((1, tm, d), row), _const_spec((1, d)), _mod_spec(n_lat // tm, d),
                  _const_spec(router.shape), _const_spec(router.shape)],
        out_specs=[pl.BlockSpec((1, tm, d), row), pl.BlockSpec((1, tm, ne), row)],
        out_shape=[jax.ShapeDtypeStruct((b, n_tiles * tm, d), BF16), jax.ShapeDtypeStruct((b, n_tiles * tm, ne), F32)],
        compiler_params=_cparams(2),
        name="router",
    )(x, g, mod, r_hi, r_lo)


def _prefix_counts(mask, tri):
    out = []
    carry = jnp.zeros((mask.shape[0], 1), F32)
    before = [carry]
    for c in range(mask.shape[1] // PREFIX_CHUNK):
        m = mask[:, c * PREFIX_CHUNK:(c + 1) * PREFIX_CHUNK]
        out.append(_dot(m.astype(BF16), tri) + carry)
        carry = carry + jnp.sum(m, axis=-1, keepdims=True)
        before.append(carry)
    return out, before


def _topk_kernel(aff_ref, pos_ref, cnt_ref, *, cap):
    a = aff_ref[0]
    ne = a.shape[0]
    bits = pltpu.bitcast(a, jnp.int32)

    def step(i, lo):
        cand = lo | jnp.left_shift(jnp.int32(1), 30 - i)
        cnt = jnp.sum(jnp.where(bits >= cand, 1.0, 0.0), axis=-1, keepdims=True)
        return jnp.where(cnt >= cap, cand, lo)

    thr = lax.fori_loop(0, 31, step, jnp.zeros((ne, 1), jnp.int32))
    gt = jnp.where(bits > thr, 1.0, 0.0)
    eq = jnp.where(bits == thr, 1.0, 0.0)
    room = cap - jnp.sum(gt, axis=-1, keepdims=True)
    ri = lax.broadcasted_iota(jnp.int32, (PREFIX_CHUNK, PREFIX_CHUNK), 0)
    ci = lax.broadcasted_iota(jnp.int32, (PREFIX_CHUNK, PREFIX_CHUNK), 1)
    tri = jnp.where(ri <= ci, 1.0, 0.0).astype(BF16)
    eq_rank, _ = _prefix_counts(eq, tri)
    sel = jnp.concatenate([
        jnp.maximum(gt[:, c * PREFIX_CHUNK:(c + 1) * PREFIX_CHUNK],
                    jnp.where(r <= room, eq[:, c * PREFIX_CHUNK:(c + 1) * PREFIX_CHUNK], 0.0))
        for c, r in enumerate(eq_rank)], axis=-1)
    sel_rank, before = _prefix_counts(sel, tri)
    for c, r in enumerate(sel_rank):
        cols = slice(c * PREFIX_CHUNK, (c + 1) * PREFIX_CHUNK)
        pos_ref[0, :, cols] = jnp.where(sel[:, cols] > 0.0, r - 1.0, -1.0).astype(jnp.int32)
    for c, cnt in enumerate(before):
        cnt_ref[0, :, c:c + 1] = cnt.astype(jnp.int32)


def _topk(aff_t, cap):
    b, ne, n = aff_t.shape
    spec = pl.BlockSpec((1, ne, n), lambda bb: (bb, 0, 0))
    n_cnt = n // PREFIX_CHUNK + 1
    return pl.pallas_call(
        functools.partial(_topk_kernel, cap=cap),
        grid=(b,),
        in_specs=[spec],
        out_specs=[spec, pl.BlockSpec((1, ne, n_cnt), lambda bb: (bb, 0, 0))],
        out_shape=[jax.ShapeDtypeStruct((b, ne, n), jnp.int32), jax.ShapeDtypeStruct((b, ne, n_cnt), jnp.int32)],
        compiler_params=_cparams(1),
        name="topk",
    )(aff_t)


def _moe_gather_kernel(cnt_ref, h_ref, pos_ref, xg_ref, *, cap, n_cnt):
    i = pl.program_id(1)
    ne, tn = pos_ref.shape[1:]
    win = min(GATHER_WINDOW, cap)
    row = (pl.program_id(0) * ne) * n_cnt + i

    @pl.when(i == 0)
    def _():
        xg_ref[...] = jnp.zeros(xg_ref.shape, xg_ref.dtype)

    h = h_ref[0]
    slot = lax.broadcasted_iota(jnp.int32, (win, tn), 0)

    def add_window(e, start, res):
        rows = pl.ds(start, win)
        xg_ref[0, e, rows, :] = (xg_ref[0, e, rows, :].astype(F32) + res).astype(BF16)

    starts, his, onehots = [], [], []
    for e in range(ne):
        lo, hi = cnt_ref[row + e * n_cnt], cnt_ref[row + e * n_cnt + 1]
        start = pl.multiple_of(jnp.minimum((lo // BF16_ROWS) * BF16_ROWS, cap - win), BF16_ROWS)
        starts.append(start)
        his.append(hi)
        onehots.append(jnp.where(pos_ref[0, e:e + 1, :] - start == slot, 1.0, 0.0).astype(BF16))
    res = _dot(jnp.concatenate(onehots, axis=0), h)
    for e in range(ne):
        add_window(e, starts[e], res[e * win:(e + 1) * win])

    for e in range(ne):
        first = starts[e] + win
        n_more = jnp.maximum(his[e] - first + (win - 1), 0) // win

        def more(k, carry, e=e, first=first):
            lower = first + k * win
            start = pl.multiple_of(jnp.minimum(lower, cap - win), BF16_ROWS)
            p = pos_ref[0, e:e + 1, :]
            hit = jnp.logical_and(p - start == slot, p >= lower)
            add_window(e, start, _dot(jnp.where(hit, 1.0, 0.0).astype(BF16), h))
            return carry

        lax.fori_loop(0, n_more, more, 0)


def _moe_gather(h, pos, cnt, set_block, n_set, cap):
    b, _, d = h.shape
    ne = pos.shape[1]
    tn = PREFIX_CHUNK
    tiles = n_set // tn
    return pl.pallas_call(
        functools.partial(_moe_gather_kernel, cap=cap, n_cnt=cnt.shape[-1]),
        grid_spec=pltpu.PrefetchScalarGridSpec(
            num_scalar_prefetch=1,
            grid=(b, tiles),
            in_specs=[pl.BlockSpec((1, tn, d), lambda bb, i, c: (bb, set_block * tiles + i, 0)),
                      pl.BlockSpec((1, ne, tn), lambda bb, i, c: (bb, 0, i))],
            out_specs=pl.BlockSpec((1, ne, cap, d), lambda bb, i, c: (bb, 0, 0, 0))),
        out_shape=jax.ShapeDtypeStruct((b, ne, cap, d), BF16),
        compiler_params=_cparams(2),
        name="moe_gather",
    )(cnt.reshape(-1), h, pos)


def _moe_ffn_kernel(xg_ref, pos_ref, aff_ref, wg_ref, wu_ref, wd_ref, y_ref, *, cap):
    bg, n = pos_ref.shape[0], pos_ref.shape[-1]
    kc = min(GATHER_CHUNK, n)
    slot = lax.broadcasted_iota(jnp.int32, (cap, kc), 0)
    gates = []
    for bb in range(bg):
        gb = jnp.zeros((cap, 1), F32)
        for c in range(n // kc):
            cols = slice(c * kc, (c + 1) * kc)
            hit = pos_ref[bb, 0, :, cols] == slot
            gb = gb + jnp.sum(jnp.where(hit, aff_ref[bb, 0, :, cols], 0.0), axis=-1, keepdims=True)
        gates.append(gb)
    xg = xg_ref[0, 0] if bg == 1 else jnp.concatenate([xg_ref[bb, 0] for bb in range(bg)], axis=0)
    gate = jnp.concatenate(gates, axis=0)
    hid = _silu(_dot(xg, wg_ref[0, 0])) * _dot(xg, wu_ref[0, 0])
    y = _dot(hid.astype(BF16), wd_ref[0, 0]) * gate
    for bb in range(bg):
        y_ref[bb, 0] = y[bb * cap:(bb + 1) * cap].astype(BF16)


def _moe_ffn(xg, pos, aff_t, wg, wu, wd, layer, bg):
    b, ne, cap, d = xg.shape
    n_set = pos.shape[-1]
    ff = wg.shape[-1]
    sel = pl.BlockSpec((bg, 1, 1, n_set), lambda i, e: (i, e, 0, 0))
    tok = pl.BlockSpec((bg, 1, cap, d), lambda i, e: (i, e, 0, 0))
    return pl.pallas_call(
        functools.partial(_moe_ffn_kernel, cap=cap),
        grid=(b // bg, ne),
        in_specs=[tok, sel, sel,
                  pl.BlockSpec((1, 1, d, ff), lambda i, e: (layer, e, 0, 0)),
                  pl.BlockSpec((1, 1, d, ff), lambda i, e: (layer, e, 0, 0)),
                  pl.BlockSpec((1, 1, ff, d), lambda i, e: (layer, e, 0, 0))],
        out_specs=tok,
        out_shape=jax.ShapeDtypeStruct((b, ne, cap, d), BF16),
        compiler_params=_cparams(2),
        name="moe_ffn",
    )(xg, pos.reshape(b, ne, 1, n_set), aff_t.reshape(b, ne, 1, n_set), wg, wu, wd)


def _moe_combine_kernel(cnt_ref, x_ref, pos_ref, y_ref, mod_ref, o_ref, *, cap, n_cnt):
    pos = pos_ref[0]
    tn, ne = pos.shape
    win = min(tn, cap)
    gate = mod_ref[0, 0][5:6]
    row = (pl.program_id(0) * ne) * n_cnt + pl.program_id(1)
    lane = lax.broadcasted_iota(jnp.int32, (tn, win), 1)
    acc = jnp.zeros(x_ref.shape[1:], F32)
    overflow = False
    for e in range(ne):
        lo, hi = cnt_ref[row + e * n_cnt], cnt_ref[row + e * n_cnt + 1]
        start = pl.multiple_of(jnp.minimum((lo // BF16_ROWS) * BF16_ROWS, cap - win), BF16_ROWS)
        overflow = jnp.logical_or(overflow, hi > start + win)
        onehot = jnp.where(pos[:, e:e + 1] - start == lane, 1.0, 0.0).astype(BF16)
        acc = acc + _dot(onehot, y_ref[0, e, pl.ds(start, win), :])
    o_ref[0] = x_ref[0] + gate * acc

    @pl.when(overflow)
    def _():
        lane_all = lax.broadcasted_iota(jnp.int32, (tn, cap), 1)
        full = jnp.zeros(x_ref.shape[1:], F32)
        for e in range(ne):
            onehot = jnp.where(pos[:, e:e + 1] == lane_all, 1.0, 0.0).astype(BF16)
            full = full + _dot(onehot, y_ref[0, e])
        o_ref[0] = x_ref[0] + gate * full


def _moe_combine(x, pos_t, cnt, y, mod, tile_off, n_set, mod_sel, out_rows):
    b, t, d = x.shape
    _, ne, cap, _ = y.shape
    tn = PREFIX_CHUNK
    n_cnt = cnt.shape[-1]
    alias = out_rows == t
    out_off = tile_off if alias else 0
    return pl.pallas_call(
        functools.partial(_moe_combine_kernel, cap=cap, n_cnt=n_cnt),
        grid_spec=pltpu.PrefetchScalarGridSpec(
            num_scalar_prefetch=1,
            grid=(b, n_set // tn),
            in_specs=[pl.BlockSpec((1, tn, d), lambda bb, i, c: (bb, i + tile_off, 0)),
                      pl.BlockSpec((1, tn, ne), lambda bb, i, c: (bb, i, 0)),
                      pl.BlockSpec((1, ne, cap, d), lambda bb, i, c: (bb, 0, 0, 0)),
                      pl.BlockSpec((1, 1, 6, d), lambda bb, i, c: (bb, mod_sel, 0, 0))],
            out_specs=pl.BlockSpec((1, tn, d), lambda bb, i, c: (bb, i + out_off, 0))),
        out_shape=jax.ShapeDtypeStruct((b, out_rows, d), F32),
        input_output_aliases={1: 0} if alias else {},
        compiler_params=_cparams(2),
        name="moe_combine",
    )(cnt.reshape(-1), x, pos_t, y, mod)


def _moe(x, g, mod, router, wg, wu, wd, layer, n_lat, with_ctx, final):
    b, t, d = x.shape
    n_ctx = t - n_lat
    tm = ROW_TILE
    h, aff = _router(x, g, mod, router, n_lat, (t if with_ctx else n_lat) // tm)
    aff_t = jnp.swapaxes(aff, 1, 2)
    sets = [(0, n_lat, 0, 1)]
    if with_ctx:
        sets.append((n_lat, n_ctx, 1, b))
    for start, n_set, mod_sel, bg in sets:
        cap = CAPACITY_FACTOR * n_set // N_EXPERTS
        a_set = aff_t[:, :, start:start + n_set]
        pos, cnt = _topk(a_set, cap)
        xg = _moe_gather(h, pos, cnt, start // n_set, n_set, cap)
        y = _moe_ffn(xg, pos, a_set, wg, wu, wd, layer, bg)
        x = _moe_combine(x, jnp.swapaxes(pos, 1, 2), cnt, y, mod, start // PREFIX_CHUNK, n_set, mod_sel,
                         n_lat if final else t)
    return x


def _rope_tables(n_lat, n_ctx, rot_dim, starts):
    n_rows = n_lat // GRID_W
    rows = jnp.repeat(jnp.arange(n_rows, dtype=F32), GRID_W)
    cols = jnp.tile(jnp.arange(GRID_W, dtype=F32), n_rows)
    n_freq = rot_dim // 4
    inv_freq = ROPE_BASE ** (-jnp.arange(n_freq, dtype=F32) / n_freq)
    ang = jnp.concatenate([rows[:, None] * inv_freq, cols[:, None] * inv_freq], axis=-1)
    half = rot_dim // 2
    cos_l, sin_l = jnp.cos(ang), jnp.sin(ang)
    cos = jnp.ones((n_lat, LANE), F32)
    sa = jnp.zeros((n_lat, LANE), F32)
    sb = jnp.zeros((n_lat, LANE), F32)
    for s in starts:
        cos = cos.at[:, s:s + half].set(cos_l).at[:, s + half:s + rot_dim].set(cos_l)
        sa = sa.at[:, s:s + half].set(-sin_l)
        sb = sb.at[:, s + half:s + rot_dim].set(sin_l)
    pad = lambda a, v: jnp.concatenate([a, jnp.full((n_ctx, LANE), v, F32)], axis=0)
    return pad(cos, 1.0), pad(sa, 0.0), pad(sb, 0.0)


def _pad_heads(w, heads, width):
    k = w.shape[0]
    w = w.reshape(k, heads, width)
    return jnp.pad(w, ((0, 0), (0, 0), (0, LANE - width))).reshape(k, heads * LANE)


def _mla_partner(a):
    half = MLA_ROPE // 2
    lane = jnp.arange(LANE)
    first = (lane >= MLA_NOPE) & (lane < MLA_NOPE + half)
    second = (lane >= MLA_NOPE + half) & (lane < MLA_QK)
    return jnp.where(first, jnp.roll(a, -half, axis=-1), jnp.where(second, jnp.roll(a, half, axis=-1), 0.0))


def _mla_weights(w_in, q_norm_g, w_uq, kv_norm_g, w_ukv):
    d = w_in.shape[0]
    lat = MLA_Q_LORA + MLA_KV_LORA
    rope_block = jnp.concatenate([jnp.zeros((d, MLA_NOPE), F32), w_in[:, lat:],
                                  jnp.zeros((d, LANE - MLA_QK), F32)], axis=1)
    ukv = w_ukv.reshape(MLA_KV_LORA, MLA_HEADS, MLA_NOPE + MLA_V)
    wuq = _pad_heads(w_uq, MLA_HEADS, MLA_QK)
    wuqp = _mla_partner(wuq.reshape(-1, MLA_HEADS, LANE)).reshape(wuq.shape)
    return {
        "win": jnp.concatenate([w_in[:, :lat], rope_block, _mla_partner(rope_block)], axis=1).astype(BF16),
        "qng": q_norm_g.reshape(1, -1),
        "wuq": wuq.astype(BF16),
        "wuqp": wuqp.astype(BF16),
        "kvng": kv_norm_g.reshape(1, -1),
        "wuk": _pad_heads(ukv[:, :, :MLA_NOPE].reshape(MLA_KV_LORA, -1), MLA_HEADS, MLA_NOPE).astype(BF16),
        "wuv": _pad_heads(ukv[:, :, MLA_NOPE:].reshape(MLA_KV_LORA, -1), MLA_HEADS, MLA_V).astype(BF16),
    }


def _mla_tables(tabs, qn_g, kn_g):
    cos, sa, sb = tabs
    out = []
    for g, scale in ((qn_g, MLA_QK ** -0.5 * LOG2E), (kn_g, 1.0)):
        gp = jnp.pad(g, (0, LANE - MLA_QK))
        out += [cos * gp * scale, (sa + sb) * _mla_partner(gp) * scale]
    return out


def kernel(x, c, ctx, c_ctx, ada_w, ada_b, norm_mix_g, norm_ffn_g, mla_w_in, mla_q_norm_g, mla_w_uq, mla_kv_norm_g, mla_w_ukv, mla_qn_g, mla_kn_g, mla_w_out, diff_w_in, diff_qn_g, diff_kn_g, diff_lambda_q1, diff_lambda_k1, diff_lambda_q2, diff_lambda_k2, diff_sub_g, diff_w_out, sg_w_in, sg_ln_g, sg_ln_b, sg_w_s, sg_b_s, sg_w_out, moe_router, moe_w_gate, moe_w_up, moe_w_down):
    b, n_lat, d = x.shape
    n_ctx = ctx.shape[1]
    depth = ada_w.shape[0]
    assert n_lat % ROW_TILE == 0 and n_ctx == ROW_TILE and n_lat % n_ctx == 0

    rows = -(-(b + 1) // 8) * 8
    cc = jnp.concatenate([c, c_ctx[None], jnp.zeros((rows - b - 1, d), F32)], axis=0)
    mods = _ada(cc, ada_w, ada_b).reshape(depth, rows, 6, d)
    xs = jnp.concatenate([x, ctx], axis=1)

    w_gate, w_up, w_down = moe_w_gate.astype(BF16), moe_w_up.astype(BF16), moe_w_down.astype(BF16)
    tabs_a = _rope_tables(n_lat, n_ctx, MLA_ROPE, (MLA_NOPE,))
    tabs_b = _rope_tables(n_lat, n_ctx, DIFF_HEAD_DIM, (0, DIFF_HEAD_DIM))

    for i in range(depth):
        kind, j = i % N_MIXERS, i // N_MIXERS
        last = i == depth - 1
        with_ctx = not last
        q_tiles = (n_lat + (n_ctx if with_ctx else 0)) // ROW_TILE
        mod = jnp.stack([mods[i, :b], jnp.broadcast_to(mods[i, b], (b, 6, d))], axis=1)
        g_mix = norm_mix_g[i].reshape(1, d)
        if kind == 0:
            w = _mla_weights(mla_w_in[j], mla_q_norm_g[j], mla_w_uq[j], mla_kv_norm_g[j], mla_w_ukv[j])
            q, k, v = _mla_in(xs, g_mix, mod, w, _mla_tables(tabs_a, mla_qn_g[j], mla_kn_g[j]), n_lat)
            a = _attn_calls(_mla_attn_kernel, "mla_attn", [q], [k, v], [], n_lat, with_ctx,
                            MLA_HEADS, MLA_HEADS_PER_STEP, MLA_HEADS_PER_STEP * MLA_V)
            xs = _out_proj(xs, a, mla_w_out[j].astype(BF16), mod, n_lat, q_tiles)
        elif kind == 1:
            lam_init = 0.8 - 0.6 * math.exp(-0.3 * i)
            tile2 = lambda g: jnp.tile(g, 2).reshape(1, LANE)
            w = {"win": diff_w_in[j].astype(BF16), "qn": tile2(diff_qn_g[j]), "kn": tile2(diff_kn_g[j])}
            q0, q1, k, v = _diff_in(xs, g_mix, mod, w, tabs_b, n_lat)
            wa = [a.reshape(1, -1) for a in (diff_lambda_q1[j], diff_lambda_k1[j], diff_lambda_q2[j],
                                             diff_lambda_k2[j], diff_sub_g[j])]
            a = _attn_calls(functools.partial(_diff_attn_kernel, lam_init=lam_init), "diff_attn", [q0, q1], [k, v],
                            wa, n_lat, with_ctx, DIFF_HEADS, DIFF_HEADS_PER_STEP, DIFF_HEADS_PER_STEP * LANE)
            xs = _out_proj(xs, a, diff_w_out[j].astype(BF16), mod, n_lat, q_tiles)
        else:
            w = {"win": sg_w_in[j].astype(BF16), "lng": sg_ln_g[j].reshape(1, -1), "lnb": sg_ln_b[j].reshape(1, -1),
                 "ws": sg_w_s[j].astype(BF16), "bs": sg_b_s[j].T, "wout": sg_w_out[j].astype(BF16)}
            xs = _sg(xs, g_mix, mod, w, n_lat)
        xs = _moe(xs, norm_ffn_g[i].reshape(1, d), mod, moe_router[i], w_gate, w_up, w_down, i, n_lat, with_ctx, last)
    return xs
```

```python
import functools
import math

import jax
import jax.numpy as jnp
from jax import lax
from jax.experimental import pallas as pl
from jax.experimental.pallas import tpu as pltpu

F32 = jnp.float32
BF16 = jnp.bfloat16

GRID_W = 64
ROPE_BASE = 10000.0
EPS = 1e-6
N_MIXERS = 3

MLA_HEADS = 16
MLA_Q_LORA = 512
MLA_KV_LORA = 256
MLA_NOPE = 64
MLA_ROPE = 32
MLA_V = 64
MLA_QK = MLA_NOPE + MLA_ROPE

DIFF_HEADS = 8
DIFF_HEAD_DIM = 64
DIFF_V_DIM = 2 * DIFF_HEAD_DIM

SG_CHUNK = 128
SG_GROUPS = 8

N_EXPERTS = 16
CAPACITY_FACTOR = 2

LANE = 128
BF16_ROWS = 16
ROW_TILE = 256
PREFIX_CHUNK = 256
GATHER_CHUNK = 1024
GATHER_WINDOW = 64
COMBINE_EXPERTS_PER_MATMUL = 4
MLA_HEADS_PER_STEP = 4
DIFF_HEADS_PER_STEP = 4
ATTN_Q_TILE = 256
LOG2E = math.log2(math.e)
VMEM_LIMIT = 56 * 1024 * 1024


def _cparams(n_axes):
    return pltpu.CompilerParams(dimension_semantics=("arbitrary",) * n_axes, vmem_limit_bytes=VMEM_LIMIT)


def _const_spec(shape):
    nd = len(shape)
    return pl.BlockSpec(shape, lambda *_: (0,) * nd)


def _dot(a, b):
    return jnp.dot(a, b, preferred_element_type=F32)


def _dot_nt(a, b):
    return lax.dot_general(a, b, (((1,), (1,)), ((), ())), preferred_element_type=F32)


def _rms(x, n):
    return x * lax.rsqrt(jnp.sum(x * x, axis=-1, keepdims=True) * (1.0 / n) + EPS)


def _norm_mod(x, g, shift, scale):
    return _rms(x, x.shape[-1]) * g * (1.0 + scale) + shift


def _silu(x):
    return x / (1.0 + jnp.exp(-x))


def _gelu_tanh(x):
    c = math.sqrt(2.0 / math.pi)
    return 0.5 * x * (1.0 + jnp.tanh(c * (x + 0.044715 * (x * x * x))))


def _rope(x, cos, sin_a, sin_b, half):
    return x * cos + pltpu.roll(x, LANE - half, 1) * sin_a + pltpu.roll(x, half, 1) * sin_b


def _ada_kernel(c_ref, w_ref, b_ref, o_ref):
    s = _silu(c_ref[...])
    o_ref[0] = jnp.dot(s, w_ref[0], preferred_element_type=F32, precision=lax.Precision.HIGHEST) + b_ref[0]


def _ada(cc, ada_w, ada_b):
    depth, d, six_d = ada_w.shape
    rows = cc.shape[0]
    tn = 1536
    return pl.pallas_call(
        _ada_kernel,
        grid=(depth, six_d // tn),
        in_specs=[
            _const_spec((rows, d)),
            pl.BlockSpec((1, d, tn), lambda i, j: (i, 0, j)),
            pl.BlockSpec((1, 1, tn), lambda i, j: (i, 0, j)),
        ],
        out_specs=pl.BlockSpec((1, rows, tn), lambda i, j: (i, 0, j)),
        out_shape=jax.ShapeDtypeStruct((depth, rows, six_d), F32),
        compiler_params=_cparams(2),
        name="ada",
    )(cc, ada_w, ada_b.reshape(depth, 1, six_d))


def _mod_spec(n_lat_tiles, d):
    return pl.BlockSpec((1, 1, 6, d), lambda b, i: (b, jnp.minimum(i // n_lat_tiles, 1), 0, 0))


def _mla_in_kernel(x_ref, g_ref, mod_ref, win_ref, qng_ref, wuq_ref, wuqp_ref, kvng_ref, wuk_ref, wuv_ref,
                   aq_ref, bq_ref, ak_ref, bk_ref, q_ref, k_ref, v_ref):
    mod = mod_ref[0, 0]
    h = _norm_mod(x_ref[0], g_ref[...], mod[0:1], mod[1:2])
    down = _dot(h.astype(BF16), win_ref[...])
    cq = (_rms(down[:, :MLA_Q_LORA], MLA_Q_LORA) * qng_ref[...]).astype(BF16)
    ckv = (_rms(down[:, MLA_Q_LORA:MLA_Q_LORA + MLA_KV_LORA], MLA_KV_LORA) * kvng_ref[...]).astype(BF16)
    q = _dot(cq, wuq_ref[...])
    qp = _dot(cq, wuqp_ref[...])
    kn = _dot(ckv, wuk_ref[...])
    v = _dot(ckv, wuv_ref[...])
    is_v = (lax.broadcasted_iota(jnp.int32, v.shape, 1) & (LANE - 1)) < MLA_V
    v_ref[0] = jnp.where(is_v, v, 1.0).astype(BF16)
    lat = MLA_Q_LORA + MLA_KV_LORA
    kr, krp = down[:, lat:lat + LANE], down[:, lat + LANE:]
    aq, bq, ak, bk = aq_ref[...], bq_ref[...], ak_ref[...], bk_ref[...]
    krot = krp * bk
    inv_n = 1.0 / MLA_QK
    for hd in range(MLA_HEADS):
        sl = slice(hd * LANE, (hd + 1) * LANE)
        qh = q[:, sl]
        rq = lax.rsqrt(jnp.sum(qh * qh, axis=-1, keepdims=True) * inv_n + EPS)
        q_ref[0, :, sl] = ((qh * aq + qp[:, sl] * bq) * rq).astype(BF16)
        kh = kn[:, sl] + kr
        rk = lax.rsqrt(jnp.sum(kh * kh, axis=-1, keepdims=True) * inv_n + EPS)
        k_ref[0, :, sl] = ((kh * ak + krot) * rk).astype(BF16)


def _mla_in(x, g, mod, w, tabs, n_lat):
    b, t, d = x.shape
    tm = ROW_TILE
    hw = MLA_HEADS * LANE
    row = lambda bb, i: (bb, i, 0)
    tab = pl.BlockSpec((tm, LANE), lambda bb, i: (i, 0))
    consts = [w["win"], w["qng"], w["wuq"], w["wuqp"], w["kvng"], w["wuk"], w["wuv"]]
    return pl.pallas_call(
        _mla_in_kernel,
        grid=(b, t // tm),
        in_specs=[pl.BlockSpec((1, tm, d), row), _const_spec((1, d)), _mod_spec(n_lat // tm, d)]
        + [_const_spec(a.shape) for a in consts] + [tab] * 4,
        out_specs=[pl.BlockSpec((1, tm, hw), row)] * 3,
        out_shape=[jax.ShapeDtypeStruct((b, t, hw), BF16)] * 3,
        compiler_params=_cparams(2),
        name="mla_in",
    )(x, g, mod, *consts, *tabs)


def _diff_in_kernel(x_ref, g_ref, mod_ref, win_ref, qn_ref, kn_ref, cos_ref, sa_ref, sb_ref,
                    q0_ref, q1_ref, k_ref, v_ref):
    mod = mod_ref[0, 0]
    d = x_ref.shape[-1]
    h = _norm_mod(x_ref[0], g_ref[...], mod[0:1], mod[1:2])
    qkv = _dot(h.astype(BF16), win_ref[...])
    v_ref[0] = qkv[:, 2 * d:].astype(BF16)
    cos, sa, sb = cos_ref[...], sa_ref[...], sb_ref[...]
    lo = lax.broadcasted_iota(jnp.int32, (x_ref.shape[1], LANE), 1) < DIFF_HEAD_DIM
    scale = DIFF_HEAD_DIM ** -0.5 * LOG2E

    def norm2(xh, gains):
        sq = xh * xh
        s_lo = jnp.sum(jnp.where(lo, sq, 0.0), axis=-1, keepdims=True)
        s_hi = jnp.sum(jnp.where(lo, 0.0, sq), axis=-1, keepdims=True)
        ms = jnp.where(lo, s_lo, s_hi) * (1.0 / DIFF_HEAD_DIM)
        return xh * lax.rsqrt(ms + EPS) * gains

    for hd in range(DIFF_HEADS):
        sl = slice(hd * LANE, (hd + 1) * LANE)
        qh = _rope(norm2(qkv[:, sl], qn_ref[...]), cos, sa, sb, DIFF_HEAD_DIM // 2) * scale
        q0_ref[0, :, sl] = jnp.where(lo, qh, 0.0).astype(BF16)
        q1_ref[0, :, sl] = jnp.where(lo, 0.0, qh).astype(BF16)
        kh = _rope(norm2(qkv[:, d + hd * LANE:d + (hd + 1) * LANE], kn_ref[...]), cos, sa, sb, DIFF_HEAD_DIM // 2)
        k_ref[0, :, sl] = kh.astype(BF16)


def _diff_in(x, g, mod, w, tabs, n_lat):
    b, t, d = x.shape
    tm = ROW_TILE
    row = lambda bb, i: (bb, i, 0)
    tab = pl.BlockSpec((tm, LANE), lambda bb, i: (i, 0))
    consts = [w["win"], w["qn"], w["kn"]]
    out = jax.ShapeDtypeStruct((b, t, d), BF16)
    return pl.pallas_call(
        _diff_in_kernel,
        grid=(b, t // tm),
        in_specs=[pl.BlockSpec((1, tm, d), row), _const_spec((1, d)), _mod_spec(n_lat // tm, d)]
        + [_const_spec(a.shape) for a in consts] + [tab, tab, tab],
        out_specs=[pl.BlockSpec((1, tm, d), row)] * 4,
        out_shape=[out] * 4,
        compiler_params=_cparams(2),
        name="diff_in",
    )(x, g, mod, *consts, *tabs)


def _exp2_scores(q, k):
    s = _dot_nt(q, k)
    return jnp.exp2(s - jnp.max(s, axis=-1, keepdims=True)).astype(BF16)


def _mla_attn_kernel(q_ref, k_ref, v_ref, *rest):
    o_ref = rest[-1]
    tq = q_ref.shape[1]
    low = lax.broadcasted_iota(jnp.int32, (tq, LANE), 1) < MLA_V
    for pr in range(MLA_HEADS_PER_STEP // 2):
        halves = []
        for hd in (2 * pr, 2 * pr + 1):
            sl = slice(hd * LANE, (hd + 1) * LANE)
            r = _dot(_exp2_scores(q_ref[0, :, sl], k_ref[0, :, sl]), v_ref[0, :, sl])
            halves.append(r / pltpu.roll(r, LANE - MLA_V, 1))
        pair = jnp.where(low, halves[0], pltpu.roll(halves[1], MLA_V, 1))
        o_ref[0, :, pr * LANE:(pr + 1) * LANE] = pair.astype(o_ref.dtype)


def _attn_calls(kernel, name, qs, kvs, consts, n_lat, with_ctx, heads, hs, out_width):
    b, t, _ = kvs[0].shape
    n_ctx = t - n_lat
    ctx_blk = n_lat // n_ctx
    out_shape = jax.ShapeDtypeStruct((b, t, heads // hs * out_width), BF16)
    cspecs = [_const_spec(a.shape) for a in consts]

    def call(tq, q_map, kv_rows, kv_map, n_tiles, prev):
        q_specs = [pl.BlockSpec((1, tq, a.shape[-1] // (heads // hs)), q_map) for a in qs]
        kv_specs = [pl.BlockSpec((1, kv_rows, a.shape[-1] // (heads // hs)), kv_map) for a in kvs]
        extra = [] if prev is None else [pl.BlockSpec(memory_space=pl.ANY)]
        n_in = len(qs) + len(kvs) + len(consts)
        return pl.pallas_call(
            kernel,
            grid=(b, heads // hs, n_tiles),
            in_specs=q_specs + kv_specs + cspecs + extra,
            out_specs=pl.BlockSpec((1, tq, out_width), q_map),
            out_shape=out_shape,
            input_output_aliases={} if prev is None else {n_in: 0},
            compiler_params=_cparams(3),
            name=name,
        )(*qs, *kvs, *consts, *([] if prev is None else [prev]))

    tq = min(ATTN_Q_TILE, n_lat)
    out = call(tq, lambda bb, hg, i: (bb, i, hg), t, lambda bb, hg, i: (bb, 0, hg), n_lat // tq, None)
    if with_ctx:
        ctx_map = lambda bb, hg, i: (bb, ctx_blk, hg)
        out = call(n_ctx, ctx_map, n_ctx, ctx_map, 1, out)
    return out


def _diff_attn_kernel(q0_ref, q1_ref, k_ref, v_ref, lq1_ref, lk1_ref, lq2_ref, lk2_ref, sub_ref, *rest, lam_init):
    o_ref = rest[-1]
    lam = (jnp.exp(jnp.sum(lq1_ref[...] * lk1_ref[...], axis=-1, keepdims=True))
           - jnp.exp(jnp.sum(lq2_ref[...] * lk2_ref[...], axis=-1, keepdims=True)) + lam_init)
    for a in range(DIFF_HEADS_PER_STEP):
        sl = slice(a * LANE, (a + 1) * LANE)
        k, v = k_ref[0, :, sl], v_ref[0, :, sl]
        outs = []
        for q_ref in (q0_ref, q1_ref):
            s = _dot_nt(q_ref[0, :, sl], k)
            p = jnp.exp2(s - jnp.max(s, axis=-1, keepdims=True))
            outs.append(_dot(p.astype(BF16), v) / jnp.sum(p, axis=-1, keepdims=True))
        o = outs[0] - lam * outs[1]
        o_ref[0, :, sl] = (_rms(o, DIFF_V_DIM) * sub_ref[...] * (1.0 - lam_init)).astype(o_ref.dtype)


def _out_proj_kernel(x_ref, a_ref, w_ref, mod_ref, o_ref):
    gate = mod_ref[0, 0][2:3]
    o_ref[0] = x_ref[0] + gate * _dot(a_ref[0], w_ref[...])


def _out_proj(x, a, w, mod, n_lat, n_tiles):
    b, t, d = x.shape
    tm = ROW_TILE
    row = lambda bb, i: (bb, i, 0)
    return pl.pallas_call(
        _out_proj_kernel,
        grid=(b, n_tiles),
        in_specs=[pl.BlockSpec((1, tm, d), row), pl.BlockSpec((1, tm, a.shape[-1]), row), _const_spec(w.shape),
                  _mod_spec(n_lat // tm, d)],
        out_specs=pl.BlockSpec((1, tm, d), row),
        out_shape=jax.ShapeDtypeStruct(x.shape, F32),
        input_output_aliases={0: 0},
        compiler_params=_cparams(2),
        name="out_proj",
    )(x, a, w, mod)


def _sg_kernel(x_ref, g_ref, mod_ref, win_ref, lng_ref, lnb_ref, ws_ref, bs_ref, wout_ref, o_ref, gated_ref):
    mod = mod_ref[0, 0]
    x = x_ref[0]
    tm = x.shape[0]
    width = lng_ref.shape[-1]
    gdim = width // SG_GROUPS
    h = _norm_mod(x, g_ref[...], mod[0:1], mod[1:2])
    z = _gelu_tanh(_dot(h.astype(BF16), win_ref[...]))
    u, v = z[:, :width], z[:, width:]
    mu = jnp.mean(v, axis=-1, keepdims=True)
    vc = v - mu
    var = jnp.mean(vc * vc, axis=-1, keepdims=True)
    vn = (vc * lax.rsqrt(var + EPS) * lng_ref[...] + lnb_ref[...]).astype(BF16)
    for c in range(tm // SG_CHUNK):
        rows = slice(c * SG_CHUNK, (c + 1) * SG_CHUNK)
        for gi in range(SG_GROUPS):
            cols = slice(gi * gdim, (gi + 1) * gdim)
            mixed = _dot(ws_ref[gi], vn[rows, cols]) + bs_ref[:, gi:gi + 1]
            gated_ref[rows, cols] = (u[rows, cols] * mixed).astype(BF16)
    o_ref[0] = x + mod[2:3] * _dot(gated_ref[...], wout_ref[...])


def _sg(x, g, mod, w, n_lat):
    b, t, d = x.shape
    tm = ROW_TILE
    row = lambda bb, i: (bb, i, 0)
    consts = [w["win"], w["lng"], w["lnb"], w["ws"], w["bs"], w["wout"]]
    return pl.pallas_call(
        _sg_kernel,
        grid=(b, t // tm),
        in_specs=[pl.BlockSpec((1, tm, d), row), _const_spec((1, d)), _mod_spec(n_lat // tm, d)]
        + [_const_spec(a.shape) for a in consts],
        out_specs=pl.BlockSpec((1, tm, d), row),
        out_shape=jax.ShapeDtypeStruct(x.shape, F32),
        scratch_shapes=[pltpu.VMEM((tm, w["lng"].shape[-1]), BF16)],
        input_output_aliases={0: 0},
        compiler_params=_cparams(2),
        name="chunk_mlp",
    )(x, g, mod, *consts)


def _router_kernel(x_ref, g_ref, mod_ref, rhi_ref, rlo_ref, h_ref, aff_ref):
    mod = mod_ref[0, 0]
    h = _norm_mod(x_ref[0], g_ref[...], mod[3:4], mod[4:5])
    h_hi = h.astype(BF16)
    h_lo = (h - h_hi.astype(F32)).astype(BF16)
    h_ref[0] = h_hi
    logits = _dot(h_hi, rhi_ref[...]) + (_dot(h_lo, rhi_ref[...]) + _dot(h_hi, rlo_ref[...]))
    e = jnp.exp(logits - jnp.max(logits, axis=-1, keepdims=True))
    aff_ref[0] = e / jnp.sum(e, axis=-1, keepdims=True)


def _router(x, g, mod, router, n_lat, n_tiles):
    b, t, d = x.shape
    tm = ROW_TILE
    ne = router.shape[-1]
    row = lambda bb, i: (bb, i, 0)
    r_hi = router.astype(BF16)
    r_lo = (router - r_hi.astype(F32)).astype(BF16)
    return pl.pallas_call(
        _router_kernel,
        grid=(b, n_tiles),
        in_specs=[pl.BlockSpec((1, tm, d), row), _const_spec((1, d)), _mod_spec(n_lat // tm, d),
                  _const_spec(router.shape), _const_spec(router.shape)],
        out_specs=[pl.BlockSpec((1, tm, d), row), pl.BlockSpec((1, tm, ne), row)],
        out_shape=[jax.ShapeDtypeStruct((b, n_tiles * tm, d), BF16), jax.ShapeDtypeStruct((b, n_tiles * tm, ne), F32)],
        compiler_params=_cparams(2),
        name="router",
    )(x, g, mod, r_hi, r_lo)


def _prefix_counts(mask, tri):
    out = []
    carry = jnp.zeros((mask.shape[0], 1), F32)
    before = [carry]
    for c in range(mask.shape[1] // PREFIX_CHUNK):
        m = mask[:, c * PREFIX_CHUNK:(c + 1) * PREFIX_CHUNK]
        out.append(_dot(m.astype(BF16), tri) + carry)
        carry = carry + jnp.sum(m, axis=-1, keepdims=True)
        before.append(carry)
    return out, before


def _topk_kernel(aff_ref, pos_ref, cnt_ref, *, cap):
    a = aff_ref[0]
    ne = a.shape[0]
    bits = pltpu.bitcast(a, jnp.int32)

    def step(i, lo):
        cand = lo | jnp.left_shift(jnp.int32(1), 30 - i)
        cnt = jnp.sum(jnp.where(bits >= cand, 1.0, 0.0), axis=-1, keepdims=True)
        return jnp.where(cnt >= cap, cand, lo)

    thr = lax.fori_loop(0, 31, step, jnp.zeros((ne, 1), jnp.int32))
    gt = jnp.where(bits > thr, 1.0, 0.0)
    eq = jnp.where(bits == thr, 1.0, 0.0)
    room = cap - jnp.sum(gt, axis=-1, keepdims=True)
    ri = lax.broadcasted_iota(jnp.int32, (PREFIX_CHUNK, PREFIX_CHUNK), 0)
    ci = lax.broadcasted_iota(jnp.int32, (PREFIX_CHUNK, PREFIX_CHUNK), 1)
    tri = jnp.where(ri <= ci, 1.0, 0.0).astype(BF16)
    eq_rank, _ = _prefix_counts(eq, tri)
    sel = jnp.concatenate([
        jnp.maximum(gt[:, c * PREFIX_CHUNK:(c + 1) * PREFIX_CHUNK],
                    jnp.where(r <= room, eq[:, c * PREFIX_CHUNK:(c + 1) * PREFIX_CHUNK], 0.0))
        for c, r in enumerate(eq_rank)], axis=-1)
    sel_rank, before = _prefix_counts(sel, tri)
    for c, r in enumerate(sel_rank):
        cols = slice(c * PREFIX_CHUNK, (c + 1) * PREFIX_CHUNK)
        pos_ref[0, :, cols] = jnp.where(sel[:, cols] > 0.0, r - 1.0, -1.0).astype(jnp.int32)
    for c, cnt in enumerate(before):
        cnt_ref[0, :, c:c + 1] = cnt.astype(jnp.int32)


def _topk(aff_t, cap):
    b, ne, n = aff_t.shape
    spec = pl.BlockSpec((1, ne, n), lambda bb: (bb, 0, 0))
    n_cnt = n // PREFIX_CHUNK + 1
    return pl.pallas_call(
        functools.partial(_topk_kernel, cap=cap),
        grid=(b,),
        in_specs=[spec],
        out_specs=[spec, pl.BlockSpec((1, ne, n_cnt), lambda bb: (bb, 0, 0))],
        out_shape=[jax.ShapeDtypeStruct((b, ne, n), jnp.int32), jax.ShapeDtypeStruct((b, ne, n_cnt), jnp.int32)],
        compiler_params=_cparams(1),
        name="topk",
    )(aff_t)


def _moe_gather_kernel(cnt_ref, h_ref, pos_ref, xg_ref, *, cap, n_cnt):
    i = pl.program_id(1)
    ne, tn = pos_ref.shape[1:]
    win = min(GATHER_WINDOW, cap)
    row = (pl.program_id(0) * ne) * n_cnt + i

    @pl.when(i == 0)
    def _():
        xg_ref[...] = jnp.zeros(xg_ref.shape, xg_ref.dtype)

    h = h_ref[0]
    slot = lax.broadcasted_iota(jnp.int32, (win, tn), 0)

    def add_window(e, start, res):
        rows = pl.ds(start, win)
        xg_ref[0, e, rows, :] = (xg_ref[0, e, rows, :].astype(F32) + res).astype(BF16)

    starts, his, onehots = [], [], []
    for e in range(ne):
        lo, hi = cnt_ref[row + e * n_cnt], cnt_ref[row + e * n_cnt + 1]
        start = pl.multiple_of(jnp.minimum((lo // BF16_ROWS) * BF16_ROWS, cap - win), BF16_ROWS)
        starts.append(start)
        his.append(hi)
        onehots.append(jnp.where(pos_ref[0, e:e + 1, :] - start == slot, 1.0, 0.0).astype(BF16))
    res = _dot(jnp.concatenate(onehots, axis=0), h)
    for e in range(ne):
        add_window(e, starts[e], res[e * win:(e + 1) * win])

    for e in range(ne):
        first = starts[e] + win
        n_more = jnp.maximum(his[e] - first + (win - 1), 0) // win

        def more(k, carry, e=e, first=first):
            lower = first + k * win
            start = pl.multiple_of(jnp.minimum(lower, cap - win), BF16_ROWS)
            p = pos_ref[0, e:e + 1, :]
            hit = jnp.logical_and(p - start == slot, p >= lower)
            add_window(e, start, _dot(jnp.where(hit, 1.0, 0.0).astype(BF16), h))
            return carry

        lax.fori_loop(0, n_more, more, 0)


def _moe_gather(h, pos, cnt, set_block, n_set, cap):
    b, _, d = h.shape
    ne = pos.shape[1]
    tn = PREFIX_CHUNK
    tiles = n_set // tn
    return pl.pallas_call(
        functools.partial(_moe_gather_kernel, cap=cap, n_cnt=cnt.shape[-1]),
        grid_spec=pltpu.PrefetchScalarGridSpec(
            num_scalar_prefetch=1,
            grid=(b, tiles),
            in_specs=[pl.BlockSpec((1, tn, d), lambda bb, i, c: (bb, set_block * tiles + i, 0)),
                      pl.BlockSpec((1, ne, tn), lambda bb, i, c: (bb, 0, i))],
            out_specs=pl.BlockSpec((1, ne, cap, d), lambda bb, i, c: (bb, 0, 0, 0))),
        out_shape=jax.ShapeDtypeStruct((b, ne, cap, d), BF16),
        compiler_params=_cparams(2),
        name="moe_gather",
    )(cnt.reshape(-1), h, pos)


def _moe_ffn_kernel(xg_ref, pos_ref, aff_ref, wg32_ref, wu32_ref, wd32_ref, y_ref, wg_ref, wu_ref, wd_ref, *, cap):
    @pl.when(pl.program_id(1) == 0)
    def _():
        wg_ref[...] = wg32_ref[0, 0].astype(BF16)
        wu_ref[...] = wu32_ref[0, 0].astype(BF16)
        wd_ref[...] = wd32_ref[0, 0].astype(BF16)

    bg, n = pos_ref.shape[0], pos_ref.shape[-1]
    kc = min(GATHER_CHUNK, n)
    slot = lax.broadcasted_iota(jnp.int32, (cap, kc), 0)
    gates = []
    for bb in range(bg):
        gb = jnp.zeros((cap, 1), F32)
        for c in range(n // kc):
            cols = slice(c * kc, (c + 1) * kc)
            hit = pos_ref[bb, 0, :, cols] == slot
            gb = gb + jnp.sum(jnp.where(hit, aff_ref[bb, 0, :, cols], 0.0), axis=-1, keepdims=True)
        gates.append(gb)
    xg = xg_ref[0, 0] if bg == 1 else jnp.concatenate([xg_ref[bb, 0] for bb in range(bg)], axis=0)
    gate = jnp.concatenate(gates, axis=0)
    hid = _silu(_dot(xg, wg_ref[...])) * _dot(xg, wu_ref[...])
    y = _dot(hid.astype(BF16), wd_ref[...]) * gate
    for bb in range(bg):
        y_ref[bb, 0] = y[bb * cap:(bb + 1) * cap].astype(BF16)


def _moe_ffn(xg, pos, aff_t, wg, wu, wd, layer, bg):
    b, ne, cap, d = xg.shape
    n_set = pos.shape[-1]
    ff = wg.shape[-1]
    sel = pl.BlockSpec((bg, 1, 1, n_set), lambda e, i: (i, e, 0, 0))
    tok = pl.BlockSpec((bg, 1, cap, d), lambda e, i: (i, e, 0, 0))
    return pl.pallas_call(
        functools.partial(_moe_ffn_kernel, cap=cap),
        grid=(ne, b // bg),
        in_specs=[tok, sel, sel,
                  pl.BlockSpec((1, 1, d, ff), lambda e, i: (layer, e, 0, 0)),
                  pl.BlockSpec((1, 1, d, ff), lambda e, i: (layer, e, 0, 0)),
                  pl.BlockSpec((1, 1, ff, d), lambda e, i: (layer, e, 0, 0))],
        out_specs=tok,
        out_shape=jax.ShapeDtypeStruct((b, ne, cap, d), BF16),
        scratch_shapes=[pltpu.VMEM((d, ff), BF16), pltpu.VMEM((d, ff), BF16), pltpu.VMEM((ff, d), BF16)],
        compiler_params=_cparams(2),
        name="moe_ffn",
    )(xg, pos.reshape(b, ne, 1, n_set), aff_t.reshape(b, ne, 1, n_set), wg, wu, wd)


def _moe_combine_kernel(cnt_ref, x_ref, pos_ref, y_ref, mod_ref, o_ref, *, cap, n_cnt):
    pos = pos_ref[0]
    tn, ne = pos.shape
    win = min(GATHER_WINDOW, cap)
    per = COMBINE_EXPERTS_PER_MATMUL
    gate = mod_ref[0, 0][5:6]
    row = (pl.program_id(0) * ne) * n_cnt + pl.program_id(1)
    lane = lax.broadcasted_iota(jnp.int32, (tn, per * win), 1)
    acc = jnp.zeros(x_ref.shape[1:], F32)
    overflow = False
    for g in range(ne // per):
        onehot = jnp.zeros((tn, per * win), F32)
        parts = []
        for j in range(per):
            e = g * per + j
            lo, hi = cnt_ref[row + e * n_cnt], cnt_ref[row + e * n_cnt + 1]
            start = pl.multiple_of(jnp.minimum((lo // BF16_ROWS) * BF16_ROWS, cap - win), BF16_ROWS)
            overflow = jnp.logical_or(overflow, hi > start + win)
            rel = pos[:, e:e + 1] - start
            target = jnp.where(jnp.logical_and(rel >= 0, rel < win), rel + j * win, -1)
            onehot = jnp.where(target == lane, 1.0, onehot)
            parts.append(y_ref[0, e, pl.ds(start, win), :])
        acc = acc + _dot(onehot.astype(BF16), jnp.concatenate(parts, axis=0))
    o_ref[0] = x_ref[0] + gate * acc

    @pl.when(overflow)
    def _():
        lane_all = lax.broadcasted_iota(jnp.int32, (tn, cap), 1)
        full = jnp.zeros(x_ref.shape[1:], F32)
        for e in range(ne):
            onehot = jnp.where(pos[:, e:e + 1] == lane_all, 1.0, 0.0).astype(BF16)
            full = full + _dot(onehot, y_ref[0, e])
        o_ref[0] = x_ref[0] + gate * full


def _moe_combine(x, pos_t, cnt, y, mod, tile_off, n_set, mod_sel, out_rows):
    b, t, d = x.shape
    _, ne, cap, _ = y.shape
    tn = PREFIX_CHUNK
    n_cnt = cnt.shape[-1]
    alias = out_rows == t
    out_off = tile_off if alias else 0
    return pl.pallas_call(
        functools.partial(_moe_combine_kernel, cap=cap, n_cnt=n_cnt),
        grid_spec=pltpu.PrefetchScalarGridSpec(
            num_scalar_prefetch=1,
            grid=(b, n_set // tn),
            in_specs=[pl.BlockSpec((1, tn, d), lambda bb, i, c: (bb, i + tile_off, 0)),
                      pl.BlockSpec((1, tn, ne), lambda bb, i, c: (bb, i, 0)),
                      pl.BlockSpec((1, ne, cap, d), lambda bb, i, c: (bb, 0, 0, 0)),
                      pl.BlockSpec((1, 1, 6, d), lambda bb, i, c: (bb, mod_sel, 0, 0))],
            out_specs=pl.BlockSpec((1, tn, d), lambda bb, i, c: (bb, i + out_off, 0))),
        out_shape=jax.ShapeDtypeStruct((b, out_rows, d), F32),
        input_output_aliases={1: 0} if alias else {},
        compiler_params=_cparams(2),
        name="moe_combine",
    )(cnt.reshape(-1), x, pos_t, y, mod)


def _moe(x, g, mod, router, wg, wu, wd, layer, n_lat, with_ctx, final):
    b, t, d = x.shape
    n_ctx = t - n_lat
    tm = ROW_TILE
    h, aff = _router(x, g, mod, router, n_lat, (t if with_ctx else n_lat) // tm)
    aff_t = jnp.swapaxes(aff, 1, 2)
    sets = [(0, n_lat, 0, 1)]
    if with_ctx:
        sets.append((n_lat, n_ctx, 1, b))
    for start, n_set, mod_sel, bg in sets:
        cap = CAPACITY_FACTOR * n_set // N_EXPERTS
        a_set = aff_t[:, :, start:start + n_set]
        pos, cnt = _topk(a_set, cap)
        xg = _moe_gather(h, pos, cnt, start // n_set, n_set, cap)
        y = _moe_ffn(xg, pos, a_set, wg, wu, wd, layer, bg)
        x = _moe_combine(x, jnp.swapaxes(pos, 1, 2), cnt, y, mod, start // PREFIX_CHUNK, n_set, mod_sel,
                         n_lat if final else t)
    return x


def _rope_tables(n_lat, n_ctx, rot_dim, starts):
    n_rows = n_lat // GRID_W
    rows = jnp.repeat(jnp.arange(n_rows, dtype=F32), GRID_W)
    cols = jnp.tile(jnp.arange(GRID_W, dtype=F32), n_rows)
    n_freq = rot_dim // 4
    inv_freq = ROPE_BASE ** (-jnp.arange(n_freq, dtype=F32) / n_freq)
    ang = jnp.concatenate([rows[:, None] * inv_freq, cols[:, None] * inv_freq], axis=-1)
    half = rot_dim // 2
    cos_l, sin_l = jnp.cos(ang), jnp.sin(ang)
    cos = jnp.ones((n_lat, LANE), F32)
    sa = jnp.zeros((n_lat, LANE), F32)
    sb = jnp.zeros((n_lat, LANE), F32)
    for s in starts:
        cos = cos.at[:, s:s + half].set(cos_l).at[:, s + half:s + rot_dim].set(cos_l)
        sa = sa.at[:, s:s + half].set(-sin_l)
        sb = sb.at[:, s + half:s + rot_dim].set(sin_l)
    pad = lambda a, v: jnp.concatenate([a, jnp.full((n_ctx, LANE), v, F32)], axis=0)
    return pad(cos, 1.0), pad(sa, 0.0), pad(sb, 0.0)


def _pad_heads(w, heads, width):
    k = w.shape[0]
    w = w.reshape(k, heads, width)
    return jnp.pad(w, ((0, 0), (0, 0), (0, LANE - width))).reshape(k, heads * LANE)


def _mla_partner(a):
    half = MLA_ROPE // 2
    lane = jnp.arange(LANE)
    first = (lane >= MLA_NOPE) & (lane < MLA_NOPE + half)
    second = (lane >= MLA_NOPE + half) & (lane < MLA_QK)
    return jnp.where(first, jnp.roll(a, -half, axis=-1), jnp.where(second, jnp.roll(a, half, axis=-1), 0.0))


def _mla_weights(w_in, q_norm_g, w_uq, kv_norm_g, w_ukv):
    d = w_in.shape[0]
    lat = MLA_Q_LORA + MLA_KV_LORA
    rope_block = jnp.concatenate([jnp.zeros((d, MLA_NOPE), F32), w_in[:, lat:],
                                  jnp.zeros((d, LANE - MLA_QK), F32)], axis=1)
    ukv = w_ukv.reshape(MLA_KV_LORA, MLA_HEADS, MLA_NOPE + MLA_V)
    wuq = _pad_heads(w_uq, MLA_HEADS, MLA_QK)
    wuqp = _mla_partner(wuq.reshape(-1, MLA_HEADS, LANE)).reshape(wuq.shape)
    return {
        "win": jnp.concatenate([w_in[:, :lat], rope_block, _mla_partner(rope_block)], axis=1).astype(BF16),
        "qng": q_norm_g.reshape(1, -1),
        "wuq": wuq.astype(BF16),
        "wuqp": wuqp.astype(BF16),
        "kvng": kv_norm_g.reshape(1, -1),
        "wuk": _pad_heads(ukv[:, :, :MLA_NOPE].reshape(MLA_KV_LORA, -1), MLA_HEADS, MLA_NOPE).astype(BF16),
        "wuv": _pad_heads(ukv[:, :, MLA_NOPE:].reshape(MLA_KV_LORA, -1), MLA_HEADS, MLA_V).astype(BF16),
    }


def _mla_tables(tabs, qn_g, kn_g):
    cos, sa, sb = tabs
    out = []
    for g, scale in ((qn_g, MLA_QK ** -0.5 * LOG2E), (kn_g, 1.0)):
        gp = jnp.pad(g, (0, LANE - MLA_QK))
        out += [cos * gp * scale, (sa + sb) * _mla_partner(gp) * scale]
    return out


def kernel(x, c, ctx, c_ctx, ada_w, ada_b, norm_mix_g, norm_ffn_g, mla_w_in, mla_q_norm_g, mla_w_uq, mla_kv_norm_g, mla_w_ukv, mla_qn_g, mla_kn_g, mla_w_out, diff_w_in, diff_qn_g, diff_kn_g, diff_lambda_q1, diff_lambda_k1, diff_lambda_q2, diff_lambda_k2, diff_sub_g, diff_w_out, sg_w_in, sg_ln_g, sg_ln_b, sg_w_s, sg_b_s, sg_w_out, moe_router, moe_w_gate, moe_w_up, moe_w_down):
    b, n_lat, d = x.shape
    n_ctx = ctx.shape[1]
    depth = ada_w.shape[0]
    assert n_lat % ROW_TILE == 0 and n_ctx == ROW_TILE and n_lat % n_ctx == 0

    rows = -(-(b + 1) // 8) * 8
    cc = jnp.concatenate([c, c_ctx[None], jnp.zeros((rows - b - 1, d), F32)], axis=0)
    mods = _ada(cc, ada_w, ada_b).reshape(depth, rows, 6, d)
    xs = jnp.concatenate([x, ctx], axis=1)

    tabs_a = _rope_tables(n_lat, n_ctx, MLA_ROPE, (MLA_NOPE,))
    tabs_b = _rope_tables(n_lat, n_ctx, DIFF_HEAD_DIM, (0, DIFF_HEAD_DIM))

    for i in range(depth):
        kind, j = i % N_MIXERS, i // N_MIXERS
        last = i == depth - 1
        with_ctx = not last
        q_tiles = (n_lat + (n_ctx if with_ctx else 0)) // ROW_TILE
        mod = jnp.stack([mods[i, :b], jnp.broadcast_to(mods[i, b], (b, 6, d))], axis=1)
        g_mix = norm_mix_g[i].reshape(1, d)
        if kind == 0:
            w = _mla_weights(mla_w_in[j], mla_q_norm_g[j], mla_w_uq[j], mla_kv_norm_g[j], mla_w_ukv[j])
            q, k, v = _mla_in(xs, g_mix, mod, w, _mla_tables(tabs_a, mla_qn_g[j], mla_kn_g[j]), n_lat)
            a = _attn_calls(_mla_attn_kernel, "mla_attn", [q], [k, v], [], n_lat, with_ctx,
                            MLA_HEADS, MLA_HEADS_PER_STEP, MLA_HEADS_PER_STEP * MLA_V)
            xs = _out_proj(xs, a, mla_w_out[j].astype(BF16), mod, n_lat, q_tiles)
        elif kind == 1:
            lam_init = 0.8 - 0.6 * math.exp(-0.3 * i)
            tile2 = lambda g: jnp.tile(g, 2).reshape(1, LANE)
            w = {"win": diff_w_in[j].astype(BF16), "qn": tile2(diff_qn_g[j]), "kn": tile2(diff_kn_g[j])}
            q0, q1, k, v = _diff_in(xs, g_mix, mod, w, tabs_b, n_lat)
            wa = [a.reshape(1, -1) for a in (diff_lambda_q1[j], diff_lambda_k1[j], diff_lambda_q2[j],
                                             diff_lambda_k2[j], diff_sub_g[j])]
            a = _attn_calls(functools.partial(_diff_attn_kernel, lam_init=lam_init), "diff_attn", [q0, q1], [k, v],
                            wa, n_lat, with_ctx, DIFF_HEADS, DIFF_HEADS_PER_STEP, DIFF_HEADS_PER_STEP * LANE)
            xs = _out_proj(xs, a, diff_w_out[j].astype(BF16), mod, n_lat, q_tiles)
        else:
            w = {"win": sg_w_in[j].astype(BF16), "lng": sg_ln_g[j].reshape(1, -1), "lnb": sg_ln_b[j].reshape(1, -1),
                 "ws": sg_w_s[j].astype(BF16), "bs": sg_b_s[j].T, "wout": sg_w_out[j].astype(BF16)}
            xs = _sg(xs, g_mix, mod, w, n_lat)
        xs = _moe(xs, norm_ffn_g[i].reshape(1, d), mod, moe_router[i], moe_w_gate, moe_w_up, moe_w_down, i,
                  n_lat, with_ctx, last)
    return xs
```

```python
import functools
import math

import jax
import jax.numpy as jnp
import numpy as np
from jax import lax
from jax.experimental import pallas as pl
from jax.experimental.pallas import tpu as pltpu

F32 = jnp.float32
BF16 = jnp.bfloat16

GRID_W = 64
ROPE_BASE = 10000.0
EPS = 1e-6
N_MIXERS = 3

MLA_HEADS = 16
MLA_Q_LORA = 512
MLA_KV_LORA = 256
MLA_NOPE = 64
MLA_ROPE = 32
MLA_V = 64
MLA_QK = MLA_NOPE + MLA_ROPE

DIFF_HEADS = 8
DIFF_HEAD_DIM = 64
DIFF_V_DIM = 2 * DIFF_HEAD_DIM

SG_CHUNK = 128
SG_GROUPS = 8

N_EXPERTS = 16
CAPACITY_FACTOR = 2

LANE = 128
BF16_ROWS = 16
ROW_TILE = 256
PREFIX_CHUNK = 256
GATHER_WINDOW = 64
COMBINE_EXPERTS_PER_MATMUL = 4
MLA_HEADS_PER_STEP = 4
DIFF_HEADS_PER_STEP = 4
ATTN_Q_TILE = 256
LOG2E = math.log2(math.e)
VMEM_LIMIT = 56 * 1024 * 1024


def _cparams(n_axes):
    return pltpu.CompilerParams(dimension_semantics=("arbitrary",) * n_axes, vmem_limit_bytes=VMEM_LIMIT)


def _const_spec(shape):
    nd = len(shape)
    return pl.BlockSpec(shape, lambda *_: (0,) * nd)


def _dot(a, b):
    return jnp.dot(a, b, preferred_element_type=F32)


def _dot_nt(a, b):
    return lax.dot_general(a, b, (((1,), (1,)), ((), ())), preferred_element_type=F32)


def _rms(x, n):
    return x * lax.rsqrt(jnp.sum(x * x, axis=-1, keepdims=True) * (1.0 / n) + EPS)


def _norm_mod(x, g, shift, scale):
    return _rms(x, x.shape[-1]) * g * (1.0 + scale) + shift


def _silu(x):
    return x / (1.0 + jnp.exp(-x))


def _gelu_tanh(x):
    c = math.sqrt(2.0 / math.pi)
    return 0.5 * x * (1.0 + jnp.tanh(c * (x + 0.044715 * (x * x * x))))


def _ada_kernel(c_ref, w_ref, b_ref, o_ref):
    s = _silu(c_ref[...])
    o_ref[0] = jnp.dot(s, w_ref[0], preferred_element_type=F32, precision=lax.Precision.HIGHEST) + b_ref[0]


def _ada(cc, ada_w, ada_b):
    depth, d, six_d = ada_w.shape
    rows = cc.shape[0]
    tn = 1536
    return pl.pallas_call(
        _ada_kernel,
        grid=(depth, six_d // tn),
        in_specs=[
            _const_spec((rows, d)),
            pl.BlockSpec((1, d, tn), lambda i, j: (i, 0, j)),
            pl.BlockSpec((1, 1, tn), lambda i, j: (i, 0, j)),
        ],
        out_specs=pl.BlockSpec((1, rows, tn), lambda i, j: (i, 0, j)),
        out_shape=jax.ShapeDtypeStruct((depth, rows, six_d), F32),
        compiler_params=_cparams(2),
        name="ada",
    )(cc, ada_w, ada_b.reshape(depth, 1, six_d))


def _mod_spec(n_lat_tiles, d):
    return pl.BlockSpec((1, 1, 6, d), lambda b, i: (b, jnp.minimum(i // n_lat_tiles, 1), 0, 0))


def _mla_in_kernel(x_ref, g_ref, mod_ref, win_ref, qng_ref, wuq_ref, wuqp_ref, kvng_ref, wuk_ref, wuv_ref,
                   aq_ref, bq_ref, ak_ref, bk_ref, q_ref, k_ref, v_ref):
    mod = mod_ref[0, 0]
    h = _norm_mod(x_ref[0], g_ref[...], mod[0:1], mod[1:2])
    down = _dot(h.astype(BF16), win_ref[...])
    cq = (_rms(down[:, :MLA_Q_LORA], MLA_Q_LORA) * qng_ref[...]).astype(BF16)
    ckv = (_rms(down[:, MLA_Q_LORA:MLA_Q_LORA + MLA_KV_LORA], MLA_KV_LORA) * kvng_ref[...]).astype(BF16)
    q = _dot(cq, wuq_ref[...])
    qp = _dot(cq, wuqp_ref[...])
    kn = _dot(ckv, wuk_ref[...])
    v = _dot(ckv, wuv_ref[...])
    is_v = (lax.broadcasted_iota(jnp.int32, v.shape, 1) & (LANE - 1)) < MLA_V
    v_ref[0] = jnp.where(is_v, v, 1.0).astype(BF16)
    lat = MLA_Q_LORA + MLA_KV_LORA
    kr, krp = down[:, lat:lat + LANE], down[:, lat + LANE:]
    aq, bq, ak, bk = aq_ref[...], bq_ref[...], ak_ref[...], bk_ref[...]
    krot = krp * bk
    inv_n = 1.0 / MLA_QK
    for hd in range(MLA_HEADS):
        sl = slice(hd * LANE, (hd + 1) * LANE)
        qh = q[:, sl]
        rq = lax.rsqrt(jnp.sum(qh * qh, axis=-1, keepdims=True) * inv_n + EPS)
        q_ref[0, :, sl] = ((qh * aq + qp[:, sl] * bq) * rq).astype(BF16)
        kh = kn[:, sl] + kr
        rk = lax.rsqrt(jnp.sum(kh * kh, axis=-1, keepdims=True) * inv_n + EPS)
        k_ref[0, :, sl] = ((kh * ak + krot) * rk).astype(BF16)


def _mla_in(x, g, mod, w, tabs, n_lat):
    b, t, d = x.shape
    tm = ROW_TILE
    hw = MLA_HEADS * LANE
    row = lambda bb, i: (bb, i, 0)
    tab = pl.BlockSpec((tm, LANE), lambda bb, i: (i, 0))
    consts = [w["win"], w["qng"], w["wuq"], w["wuqp"], w["kvng"], w["wuk"], w["wuv"]]
    return pl.pallas_call(
        _mla_in_kernel,
        grid=(b, t // tm),
        in_specs=[pl.BlockSpec((1, tm, d), row), _const_spec((1, d)), _mod_spec(n_lat // tm, d)]
        + [_const_spec(a.shape) for a in consts] + [tab] * 4,
        out_specs=[pl.BlockSpec((1, tm, hw), row)] * 3,
        out_shape=[jax.ShapeDtypeStruct((b, t, hw), BF16)] * 3,
        compiler_params=_cparams(2),
        name="mla_in",
    )(x, g, mod, *consts, *tabs)


def _diff_in_kernel(x_ref, g_ref, mod_ref, win_ref, aq_ref, bq_ref, ak_ref, bk_ref, q0_ref, q1_ref, k_ref, v_ref):
    mod = mod_ref[0, 0]
    d = x_ref.shape[-1]
    h = _norm_mod(x_ref[0], g_ref[...], mod[0:1], mod[1:2])
    qkv = _dot(h.astype(BF16), win_ref[...])
    v_ref[0] = qkv[:, 2 * d:3 * d].astype(BF16)
    lo = lax.broadcasted_iota(jnp.int32, (x_ref.shape[1], LANE), 1) < DIFF_HEAD_DIM
    inv_n = 1.0 / DIFF_HEAD_DIM

    def normed_rope(off, hd, a, b):
        xh = qkv[:, off + hd * LANE:off + (hd + 1) * LANE]
        xp = qkv[:, off + 3 * d + hd * LANE:off + 3 * d + (hd + 1) * LANE]
        sq = xh * xh
        s_lo = jnp.sum(jnp.where(lo, sq, 0.0), axis=-1, keepdims=True)
        s_hi = jnp.sum(jnp.where(lo, 0.0, sq), axis=-1, keepdims=True)
        return (xh * a + xp * b) * lax.rsqrt(jnp.where(lo, s_lo, s_hi) * inv_n + EPS)

    aq, bq, ak, bk = aq_ref[...], bq_ref[...], ak_ref[...], bk_ref[...]
    for hd in range(DIFF_HEADS):
        sl = slice(hd * LANE, (hd + 1) * LANE)
        qh = normed_rope(0, hd, aq, bq)
        q0_ref[0, :, sl] = jnp.where(lo, qh, 0.0).astype(BF16)
        q1_ref[0, :, sl] = jnp.where(lo, 0.0, qh).astype(BF16)
        k_ref[0, :, sl] = normed_rope(d, hd, ak, bk).astype(BF16)


def _diff_in(x, g, mod, win, tabs, n_lat):
    b, t, d = x.shape
    tm = ROW_TILE
    row = lambda bb, i: (bb, i, 0)
    tab = pl.BlockSpec((tm, LANE), lambda bb, i: (i, 0))
    out = jax.ShapeDtypeStruct((b, t, d), BF16)
    return pl.pallas_call(
        _diff_in_kernel,
        grid=(b, t // tm),
        in_specs=[pl.BlockSpec((1, tm, d), row), _const_spec((1, d)), _mod_spec(n_lat // tm, d),
                  _const_spec(win.shape)] + [tab] * 4,
        out_specs=[pl.BlockSpec((1, tm, d), row)] * 4,
        out_shape=[out] * 4,
        compiler_params=_cparams(2),
        name="diff_in",
    )(x, g, mod, win, *tabs)


def _exp2_scores(q, k):
    s = _dot_nt(q, k)
    return jnp.exp2(s - jnp.max(s, axis=-1, keepdims=True)).astype(BF16)


def _mla_attn_kernel(q_ref, k_ref, v_ref, *rest):
    o_ref = rest[-1]
    tq = q_ref.shape[1]
    low = lax.broadcasted_iota(jnp.int32, (tq, LANE), 1) < MLA_V
    for pr in range(MLA_HEADS_PER_STEP // 2):
        halves = []
        for hd in (2 * pr, 2 * pr + 1):
            sl = slice(hd * LANE, (hd + 1) * LANE)
            r = _dot(_exp2_scores(q_ref[0, :, sl], k_ref[0, :, sl]), v_ref[0, :, sl])
            halves.append(r / pltpu.roll(r, LANE - MLA_V, 1))
        pair = jnp.where(low, halves[0], pltpu.roll(halves[1], MLA_V, 1))
        o_ref[0, :, pr * LANE:(pr + 1) * LANE] = pair.astype(o_ref.dtype)


def _attn_calls(kernel, name, qs, kvs, consts, n_lat, with_ctx, heads, hs, out_width):
    b, t, _ = kvs[0].shape
    n_ctx = t - n_lat
    ctx_blk = n_lat // n_ctx
    out_shape = jax.ShapeDtypeStruct((b, t, heads // hs * out_width), BF16)
    cspecs = [_const_spec(a.shape) for a in consts]

    def call(tq, q_map, kv_rows, kv_map, n_tiles, prev):
        q_specs = [pl.BlockSpec((1, tq, a.shape[-1] // (heads // hs)), q_map) for a in qs]
        kv_specs = [pl.BlockSpec((1, kv_rows, a.shape[-1] // (heads // hs)), kv_map) for a in kvs]
        extra = [] if prev is None else [pl.BlockSpec(memory_space=pl.ANY)]
        n_in = len(qs) + len(kvs) + len(consts)
        return pl.pallas_call(
            kernel,
            grid=(b, heads // hs, n_tiles),
            in_specs=q_specs + kv_specs + cspecs + extra,
            out_specs=pl.BlockSpec((1, tq, out_width), q_map),
            out_shape=out_shape,
            input_output_aliases={} if prev is None else {n_in: 0},
            compiler_params=_cparams(3),
            name=name,
        )(*qs, *kvs, *consts, *([] if prev is None else [prev]))

    tq = min(ATTN_Q_TILE, n_lat)
    out = call(tq, lambda bb, hg, i: (bb, i, hg), t, lambda bb, hg, i: (bb, 0, hg), n_lat // tq, None)
    if with_ctx:
        ctx_map = lambda bb, hg, i: (bb, ctx_blk, hg)
        out = call(n_ctx, ctx_map, n_ctx, ctx_map, 1, out)
    return out


def _diff_attn_kernel(q0_ref, q1_ref, k_ref, v_ref, lq1_ref, lk1_ref, lq2_ref, lk2_ref, sub_ref, *rest, lam_init):
    o_ref = rest[-1]
    lam = (jnp.exp(jnp.sum(lq1_ref[...] * lk1_ref[...], axis=-1, keepdims=True))
           - jnp.exp(jnp.sum(lq2_ref[...] * lk2_ref[...], axis=-1, keepdims=True)) + lam_init)
    for a in range(DIFF_HEADS_PER_STEP):
        sl = slice(a * LANE, (a + 1) * LANE)
        k, v = k_ref[0, :, sl], v_ref[0, :, sl]
        outs = []
        for q_ref in (q0_ref, q1_ref):
            s = _dot_nt(q_ref[0, :, sl], k)
            p = jnp.exp2(s - jnp.max(s, axis=-1, keepdims=True))
            outs.append(_dot(p.astype(BF16), v) / jnp.sum(p, axis=-1, keepdims=True))
        o = outs[0] - lam * outs[1]
        o_ref[0, :, sl] = (_rms(o, DIFF_V_DIM) * sub_ref[...] * (1.0 - lam_init)).astype(o_ref.dtype)


def _out_proj_kernel(x_ref, a_ref, w_ref, mod_ref, o_ref):
    gate = mod_ref[0, 0][2:3]
    o_ref[0] = x_ref[0] + gate * _dot(a_ref[0], w_ref[...])


def _out_proj(x, a, w, mod, n_lat, n_tiles):
    b, t, d = x.shape
    tm = ROW_TILE
    row = lambda bb, i: (bb, i, 0)
    return pl.pallas_call(
        _out_proj_kernel,
        grid=(b, n_tiles),
        in_specs=[pl.BlockSpec((1, tm, d), row), pl.BlockSpec((1, tm, a.shape[-1]), row), _const_spec(w.shape),
                  _mod_spec(n_lat // tm, d)],
        out_specs=pl.BlockSpec((1, tm, d), row),
        out_shape=jax.ShapeDtypeStruct(x.shape, F32),
        input_output_aliases={0: 0},
        compiler_params=_cparams(2),
        name="out_proj",
    )(x, a, w, mod)


def _sg_kernel(x_ref, g_ref, mod_ref, win_ref, lng_ref, lnb_ref, ws_ref, bs_ref, wout_ref, o_ref, gated_ref):
    mod = mod_ref[0, 0]
    x = x_ref[0]
    tm = x.shape[0]
    width = lng_ref.shape[-1]
    gdim = width // SG_GROUPS
    h = _norm_mod(x, g_ref[...], mod[0:1], mod[1:2])
    z = _gelu_tanh(_dot(h.astype(BF16), win_ref[...]))
    u, v = z[:, :width], z[:, width:]
    mu = jnp.mean(v, axis=-1, keepdims=True)
    vc = v - mu
    var = jnp.mean(vc * vc, axis=-1, keepdims=True)
    vn = (vc * lax.rsqrt(var + EPS) * lng_ref[...] + lnb_ref[...]).astype(BF16)
    for c in range(tm // SG_CHUNK):
        rows = slice(c * SG_CHUNK, (c + 1) * SG_CHUNK)
        for gi in range(SG_GROUPS):
            cols = slice(gi * gdim, (gi + 1) * gdim)
            mixed = _dot(ws_ref[gi], vn[rows, cols]) + bs_ref[:, gi:gi + 1]
            gated_ref[rows, cols] = (u[rows, cols] * mixed).astype(BF16)
    o_ref[0] = x + mod[2:3] * _dot(gated_ref[...], wout_ref[...])


def _sg(x, g, mod, w, n_lat):
    b, t, d = x.shape
    tm = ROW_TILE
    row = lambda bb, i: (bb, i, 0)
    consts = [w["win"], w["lng"], w["lnb"], w["ws"], w["bs"], w["wout"]]
    return pl.pallas_call(
        _sg_kernel,
        grid=(b, t // tm),
        in_specs=[pl.BlockSpec((1, tm, d), row), _const_spec((1, d)), _mod_spec(n_lat // tm, d)]
        + [_const_spec(a.shape) for a in consts],
        out_specs=pl.BlockSpec((1, tm, d), row),
        out_shape=jax.ShapeDtypeStruct(x.shape, F32),
        scratch_shapes=[pltpu.VMEM((tm, w["lng"].shape[-1]), BF16)],
        input_output_aliases={0: 0},
        compiler_params=_cparams(2),
        name="chunk_mlp",
    )(x, g, mod, *consts)


def _router_kernel(x_ref, g_ref, mod_ref, rhi_ref, rlo_ref, h_ref, aff_ref):
    mod = mod_ref[0, 0]
    h = _norm_mod(x_ref[0], g_ref[...], mod[3:4], mod[4:5])
    h_hi = h.astype(BF16)
    h_lo = (h - h_hi.astype(F32)).astype(BF16)
    h_ref[0] = h_hi
    logits = _dot(h_hi, rhi_ref[...]) + (_dot(h_lo, rhi_ref[...]) + _dot(h_hi, rlo_ref[...]))
    e = jnp.exp(logits - jnp.max(logits, axis=-1, keepdims=True))
    aff_ref[0] = e / jnp.sum(e, axis=-1, keepdims=True)


def _router(x, g, mod, router, n_lat, n_tiles):
    b, t, d = x.shape
    tm = ROW_TILE
    ne = router.shape[-1]
    row = lambda bb, i: (bb, i, 0)
    r_hi = router.astype(BF16)
    r_lo = (router - r_hi.astype(F32)).astype(BF16)
    return pl.pallas_call(
        _router_kernel,
        grid=(b, n_tiles),
        in_specs=[pl.BlockSpec((1, tm, d), row), _const_spec((1, d)), _mod_spec(n_lat // tm, d),
                  _const_spec(router.shape), _const_spec(router.shape)],
        out_specs=[pl.BlockSpec((1, tm, d), row), pl.BlockSpec((1, tm, ne), row)],
        out_shape=[jax.ShapeDtypeStruct((b, n_tiles * tm, d), BF16), jax.ShapeDtypeStruct((b, n_tiles * tm, ne), F32)],
        compiler_params=_cparams(2),
        name="router",
    )(x, g, mod, r_hi, r_lo)


def _prefix_counts(mask, tri):
    out = []
    carry = jnp.zeros((mask.shape[0], 1), F32)
    before = [carry]
    for c in range(mask.shape[1] // PREFIX_CHUNK):
        m = mask[:, c * PREFIX_CHUNK:(c + 1) * PREFIX_CHUNK]
        out.append(_dot(m.astype(BF16), tri) + carry)
        carry = carry + jnp.sum(m, axis=-1, keepdims=True)
        before.append(carry)
    return out, before


def _topk_kernel(aff_ref, pos_ref, cnt_ref, *, cap):
    a = aff_ref[0]
    ne = a.shape[0]
    bits = pltpu.bitcast(a, jnp.int32)

    def step(i, lo):
        cand = lo | jnp.left_shift(jnp.int32(1), 30 - i)
        cnt = jnp.sum(jnp.where(bits >= cand, 1.0, 0.0), axis=-1, keepdims=True)
        return jnp.where(cnt >= cap, cand, lo)

    thr = lax.fori_loop(0, 31, step, jnp.zeros((ne, 1), jnp.int32))
    gt = jnp.where(bits > thr, 1.0, 0.0)
    eq = jnp.where(bits == thr, 1.0, 0.0)
    room = cap - jnp.sum(gt, axis=-1, keepdims=True)
    ri = lax.broadcasted_iota(jnp.int32, (PREFIX_CHUNK, PREFIX_CHUNK), 0)
    ci = lax.broadcasted_iota(jnp.int32, (PREFIX_CHUNK, PREFIX_CHUNK), 1)
    tri = jnp.where(ri <= ci, 1.0, 0.0).astype(BF16)
    eq_rank, _ = _prefix_counts(eq, tri)
    sel = jnp.concatenate([
        jnp.maximum(gt[:, c * PREFIX_CHUNK:(c + 1) * PREFIX_CHUNK],
                    jnp.where(r <= room, eq[:, c * PREFIX_CHUNK:(c + 1) * PREFIX_CHUNK], 0.0))
        for c, r in enumerate(eq_rank)], axis=-1)
    sel_rank, before = _prefix_counts(sel, tri)
    for c, r in enumerate(sel_rank):
        cols = slice(c * PREFIX_CHUNK, (c + 1) * PREFIX_CHUNK)
        pos_ref[0, :, cols] = jnp.where(sel[:, cols] > 0.0, r - 1.0, -1.0).astype(jnp.int32)
    for c, cnt in enumerate(before):
        cnt_ref[0, :, c:c + 1] = cnt.astype(jnp.int32)


def _topk(aff_t, cap):
    b, ne, n = aff_t.shape
    spec = pl.BlockSpec((1, ne, n), lambda bb: (bb, 0, 0))
    n_cnt = n // PREFIX_CHUNK + 1
    return pl.pallas_call(
        functools.partial(_topk_kernel, cap=cap),
        grid=(b,),
        in_specs=[spec],
        out_specs=[spec, pl.BlockSpec((1, ne, n_cnt), lambda bb: (bb, 0, 0))],
        out_shape=[jax.ShapeDtypeStruct((b, ne, n), jnp.int32), jax.ShapeDtypeStruct((b, ne, n_cnt), jnp.int32)],
        compiler_params=_cparams(1),
        name="topk",
    )(aff_t)


def _moe_gather_kernel(cnt_ref, h_ref, pos_ref, aff_ref, xg_ref, gate_ref, *, cap, n_cnt):
    i = pl.program_id(1)
    ne, tn = pos_ref.shape[1:]
    win = min(GATHER_WINDOW, cap)
    row = (pl.program_id(0) * ne) * n_cnt + i

    @pl.when(i == 0)
    def _():
        xg_ref[...] = jnp.zeros(xg_ref.shape, xg_ref.dtype)
        gate_ref[...] = jnp.zeros(gate_ref.shape, gate_ref.dtype)

    h = h_ref[0]
    slot = lax.broadcasted_iota(jnp.int32, (win, tn), 0)

    def add_window(e, start, hit, res):
        rows = pl.ds(start, win)
        xg_ref[0, e, rows, :] = (xg_ref[0, e, rows, :].astype(F32) + res).astype(BF16)
        gate_ref[0, e, rows, :] += jnp.sum(jnp.where(hit, aff_ref[0, e:e + 1, :], 0.0), axis=-1, keepdims=True)

    starts, his, hits = [], [], []
    for e in range(ne):
        lo, hi = cnt_ref[row + e * n_cnt], cnt_ref[row + e * n_cnt + 1]
        start = pl.multiple_of(jnp.minimum((lo // BF16_ROWS) * BF16_ROWS, cap - win), BF16_ROWS)
        starts.append(start)
        his.append(hi)
        hits.append(pos_ref[0, e:e + 1, :] - start == slot)
    onehot = jnp.concatenate([jnp.where(hit, 1.0, 0.0).astype(BF16) for hit in hits], axis=0)
    res = _dot(onehot, h)
    for e in range(ne):
        add_window(e, starts[e], hits[e], res[e * win:(e + 1) * win])

    for e in range(ne):
        first = starts[e] + win
        n_more = jnp.maximum(his[e] - first + (win - 1), 0) // win

        def more(k, carry, e=e, first=first):
            lower = first + k * win
            start = pl.multiple_of(jnp.minimum(lower, cap - win), BF16_ROWS)
            p = pos_ref[0, e:e + 1, :]
            hit = jnp.logical_and(p - start == slot, p >= lower)
            add_window(e, start, hit, _dot(jnp.where(hit, 1.0, 0.0).astype(BF16), h))
            return carry

        lax.fori_loop(0, n_more, more, 0)


def _moe_gather(h, pos, aff_t, cnt, set_block, n_set, cap):
    b, _, d = h.shape
    ne = pos.shape[1]
    tn = PREFIX_CHUNK
    tiles = n_set // tn
    tile = pl.BlockSpec((1, ne, tn), lambda bb, i, c: (bb, 0, i))
    return pl.pallas_call(
        functools.partial(_moe_gather_kernel, cap=cap, n_cnt=cnt.shape[-1]),
        grid_spec=pltpu.PrefetchScalarGridSpec(
            num_scalar_prefetch=1,
            grid=(b, tiles),
            in_specs=[pl.BlockSpec((1, tn, d), lambda bb, i, c: (bb, set_block * tiles + i, 0)), tile, tile],
            out_specs=[pl.BlockSpec((1, ne, cap, d), lambda bb, i, c: (bb, 0, 0, 0)),
                       pl.BlockSpec((1, ne, cap, 1), lambda bb, i, c: (bb, 0, 0, 0))]),
        out_shape=[jax.ShapeDtypeStruct((b, ne, cap, d), BF16), jax.ShapeDtypeStruct((b, ne, cap, 1), F32)],
        compiler_params=_cparams(2),
        name="moe_gather",
    )(cnt.reshape(-1), h, pos, aff_t)


def _moe_ffn_kernel(xg_ref, gate_ref, wg32_ref, wu32_ref, wd32_ref, y_ref, wg_ref, wu_ref, wd_ref):
    @pl.when(pl.program_id(1) == 0)
    def _():
        wg_ref[...] = wg32_ref[0, 0].astype(BF16)
        wu_ref[...] = wu32_ref[0, 0].astype(BF16)
        wd_ref[...] = wd32_ref[0, 0].astype(BF16)

    bg, _, cap, _ = xg_ref.shape
    rows = lambda ref: ref[0, 0] if bg == 1 else jnp.concatenate([ref[bb, 0] for bb in range(bg)], axis=0)
    xg = rows(xg_ref)
    hid = _silu(_dot(xg, wg_ref[...])) * _dot(xg, wu_ref[...])
    y = _dot(hid.astype(BF16), wd_ref[...]) * rows(gate_ref)
    for bb in range(bg):
        y_ref[bb, 0] = y[bb * cap:(bb + 1) * cap].astype(BF16)


def _moe_ffn(xg, gates, wg, wu, wd, layer, bg):
    b, ne, cap, d = xg.shape
    ff = wg.shape[-1]
    tok = pl.BlockSpec((bg, 1, cap, d), lambda e, i: (i, e, 0, 0))
    return pl.pallas_call(
        _moe_ffn_kernel,
        grid=(ne, b // bg),
        in_specs=[tok, pl.BlockSpec((bg, 1, cap, 1), lambda e, i: (i, e, 0, 0)),
                  pl.BlockSpec((1, 1, d, ff), lambda e, i: (layer, e, 0, 0)),
                  pl.BlockSpec((1, 1, d, ff), lambda e, i: (layer, e, 0, 0)),
                  pl.BlockSpec((1, 1, ff, d), lambda e, i: (layer, e, 0, 0))],
        out_specs=tok,
        out_shape=jax.ShapeDtypeStruct((b, ne, cap, d), BF16),
        scratch_shapes=[pltpu.VMEM((d, ff), BF16), pltpu.VMEM((d, ff), BF16), pltpu.VMEM((ff, d), BF16)],
        compiler_params=_cparams(2),
        name="moe_ffn",
    )(xg, gates, wg, wu, wd)


def _moe_combine_kernel(cnt_ref, x_ref, pos_ref, y_ref, mod_ref, o_ref, *, cap, n_cnt):
    pos = pos_ref[0]
    tn, ne = pos.shape
    win = min(GATHER_WINDOW, cap)
    per = COMBINE_EXPERTS_PER_MATMUL
    gate = mod_ref[0, 0][5:6]
    row = (pl.program_id(0) * ne) * n_cnt + pl.program_id(1)
    lane = lax.broadcasted_iota(jnp.int32, (tn, per * win), 1)
    acc = jnp.zeros(x_ref.shape[1:], F32)
    overflow = False
    for g in range(ne // per):
        onehot = jnp.zeros((tn, per * win), F32)
        parts = []
        for j in range(per):
            e = g * per + j
            lo, hi = cnt_ref[row + e * n_cnt], cnt_ref[row + e * n_cnt + 1]
            start = pl.multiple_of(jnp.minimum((lo // BF16_ROWS) * BF16_ROWS, cap - win), BF16_ROWS)
            overflow = jnp.logical_or(overflow, hi > start + win)
            rel = pos[:, e:e + 1] - start
            target = jnp.where(jnp.logical_and(rel >= 0, rel < win), rel + j * win, -1)
            onehot = jnp.where(target == lane, 1.0, onehot)
            parts.append(y_ref[0, e, pl.ds(start, win), :])
        acc = acc + _dot(onehot.astype(BF16), jnp.concatenate(parts, axis=0))
    o_ref[0] = x_ref[0] + gate * acc

    @pl.when(overflow)
    def _():
        lane_all = lax.broadcasted_iota(jnp.int32, (tn, cap), 1)
        full = jnp.zeros(x_ref.shape[1:], F32)
        for e in range(ne):
            onehot = jnp.where(pos[:, e:e + 1] == lane_all, 1.0, 0.0).astype(BF16)
            full = full + _dot(onehot, y_ref[0, e])
        o_ref[0] = x_ref[0] + gate * full


def _moe_combine(x, pos_t, cnt, y, mod, tile_off, n_set, mod_sel, out_rows):
    b, t, d = x.shape
    _, ne, cap, _ = y.shape
    tn = PREFIX_CHUNK
    n_cnt = cnt.shape[-1]
    alias = out_rows == t
    out_off = tile_off if alias else 0
    return pl.pallas_call(
        functools.partial(_moe_combine_kernel, cap=cap, n_cnt=n_cnt),
        grid_spec=pltpu.PrefetchScalarGridSpec(
            num_scalar_prefetch=1,
            grid=(b, n_set // tn),
            in_specs=[pl.BlockSpec((1, tn, d), lambda bb, i, c: (bb, i + tile_off, 0)),
                      pl.BlockSpec((1, tn, ne), lambda bb, i, c: (bb, i, 0)),
                      pl.BlockSpec((1, ne, cap, d), lambda bb, i, c: (bb, 0, 0, 0)),
                      pl.BlockSpec((1, 1, 6, d), lambda bb, i, c: (bb, mod_sel, 0, 0))],
            out_specs=pl.BlockSpec((1, tn, d), lambda bb, i, c: (bb, i + out_off, 0))),
        out_shape=jax.ShapeDtypeStruct((b, out_rows, d), F32),
        input_output_aliases={1: 0} if alias else {},
        compiler_params=_cparams(2),
        name="moe_combine",
    )(cnt.reshape(-1), x, pos_t, y, mod)


def _moe(x, g, mod, router, wg, wu, wd, layer, n_lat, with_ctx, final):
    b, t, d = x.shape
    n_ctx = t - n_lat
    tm = ROW_TILE
    h, aff = _router(x, g, mod, router, n_lat, (t if with_ctx else n_lat) // tm)
    aff_t = jnp.swapaxes(aff, 1, 2)
    sets = [(0, n_lat, 0, 1)]
    if with_ctx:
        sets.append((n_lat, n_ctx, 1, b))
    for start, n_set, mod_sel, bg in sets:
        cap = CAPACITY_FACTOR * n_set // N_EXPERTS
        a_set = aff_t[:, :, start:start + n_set]
        pos, cnt = _topk(a_set, cap)
        xg, gates = _moe_gather(h, pos, a_set, cnt, start // n_set, n_set, cap)
        y = _moe_ffn(xg, gates, wg, wu, wd, layer, bg)
        x = _moe_combine(x, jnp.swapaxes(pos, 1, 2), cnt, y, mod, start // PREFIX_CHUNK, n_set, mod_sel,
                         n_lat if final else t)
    return x


def _rope_tables(n_lat, n_ctx, rot_dim, starts):
    n_rows = n_lat // GRID_W
    rows = np.repeat(np.arange(n_rows, dtype=np.float32), GRID_W)
    cols = np.tile(np.arange(GRID_W, dtype=np.float32), n_rows)
    n_freq = rot_dim // 4
    inv_freq = (np.float32(ROPE_BASE) ** (-np.arange(n_freq, dtype=np.float32) / np.float32(n_freq))).astype(np.float32)
    ang = np.concatenate([rows[:, None] * inv_freq, cols[:, None] * inv_freq], axis=-1)
    half = rot_dim // 2
    cos_l, sin_l = np.cos(ang).astype(np.float32), np.sin(ang).astype(np.float32)
    cos = np.ones((n_lat + n_ctx, LANE), np.float32)
    sa = np.zeros((n_lat + n_ctx, LANE), np.float32)
    sb = np.zeros((n_lat + n_ctx, LANE), np.float32)
    for s in starts:
        cos[:n_lat, s:s + half] = cos_l
        cos[:n_lat, s + half:s + rot_dim] = cos_l
        sa[:n_lat, s:s + half] = -sin_l
        sb[:n_lat, s + half:s + rot_dim] = sin_l
    return jnp.asarray(cos), jnp.asarray(sa), jnp.asarray(sb)


def _pad_heads(w, heads, width):
    k = w.shape[0]
    w = w.reshape(k, heads, width)
    return jnp.pad(w, ((0, 0), (0, 0), (0, LANE - width))).reshape(k, heads * LANE)


def _mla_partner(a):
    half = MLA_ROPE // 2
    lane = jnp.arange(LANE)
    first = (lane >= MLA_NOPE) & (lane < MLA_NOPE + half)
    second = (lane >= MLA_NOPE + half) & (lane < MLA_QK)
    return jnp.where(first, jnp.roll(a, -half, axis=-1), jnp.where(second, jnp.roll(a, half, axis=-1), 0.0))


def _mla_weights(w_in, q_norm_g, w_uq, kv_norm_g, w_ukv):
    d = w_in.shape[0]
    lat = MLA_Q_LORA + MLA_KV_LORA
    rope_block = jnp.concatenate([jnp.zeros((d, MLA_NOPE), F32), w_in[:, lat:],
                                  jnp.zeros((d, LANE - MLA_QK), F32)], axis=1)
    ukv = w_ukv.reshape(MLA_KV_LORA, MLA_HEADS, MLA_NOPE + MLA_V)
    wuq = _pad_heads(w_uq, MLA_HEADS, MLA_QK)
    wuqp = _mla_partner(wuq.reshape(-1, MLA_HEADS, LANE)).reshape(wuq.shape)
    return {
        "win": jnp.concatenate([w_in[:, :lat], rope_block, _mla_partner(rope_block)], axis=1).astype(BF16),
        "qng": q_norm_g.reshape(1, -1),
        "wuq": wuq.astype(BF16),
        "wuqp": wuqp.astype(BF16),
        "kvng": kv_norm_g.reshape(1, -1),
        "wuk": _pad_heads(ukv[:, :, :MLA_NOPE].reshape(MLA_KV_LORA, -1), MLA_HEADS, MLA_NOPE).astype(BF16),
        "wuv": _pad_heads(ukv[:, :, MLA_NOPE:].reshape(MLA_KV_LORA, -1), MLA_HEADS, MLA_V).astype(BF16),
    }


def _diff_partner(a):
    half = DIFF_HEAD_DIM // 2
    shape = a.shape
    a = a.reshape(shape[:-1] + (shape[-1] // LANE, LANE))
    first = (jnp.arange(LANE) % DIFF_HEAD_DIM) < half
    return jnp.where(first, jnp.roll(a, -half, axis=-1), jnp.roll(a, half, axis=-1)).reshape(shape)


def _diff_weights(w_in):
    d = w_in.shape[0]
    return jnp.concatenate([w_in, _diff_partner(w_in[:, :2 * d])], axis=1).astype(BF16)


def _diff_tables(tabs, qn_g, kn_g):
    cos, sa, sb = tabs
    out = []
    for g, scale in ((qn_g, DIFF_HEAD_DIM ** -0.5 * LOG2E), (kn_g, 1.0)):
        g2 = jnp.tile(g, 2)
        out += [cos * g2 * scale, (sa + sb) * _diff_partner(g2) * scale]
    return out


def _mla_tables(tabs, qn_g, kn_g):
    cos, sa, sb = tabs
    out = []
    for g, scale in ((qn_g, MLA_QK ** -0.5 * LOG2E), (kn_g, 1.0)):
        gp = jnp.pad(g, (0, LANE - MLA_QK))
        out += [cos * gp * scale, (sa + sb) * _mla_partner(gp) * scale]
    return out


def kernel(x, c, ctx, c_ctx, ada_w, ada_b, norm_mix_g, norm_ffn_g, mla_w_in, mla_q_norm_g, mla_w_uq, mla_kv_norm_g, mla_w_ukv, mla_qn_g, mla_kn_g, mla_w_out, diff_w_in, diff_qn_g, diff_kn_g, diff_lambda_q1, diff_lambda_k1, diff_lambda_q2, diff_lambda_k2, diff_sub_g, diff_w_out, sg_w_in, sg_ln_g, sg_ln_b, sg_w_s, sg_b_s, sg_w_out, moe_router, moe_w_gate, moe_w_up, moe_w_down):
    b, n_lat, d = x.shape
    n_ctx = ctx.shape[1]
    depth = ada_w.shape[0]
    assert n_lat % ROW_TILE == 0 and n_ctx == ROW_TILE and n_lat % n_ctx == 0

    rows = -(-(b + 1) // 8) * 8
    cc = jnp.concatenate([c, c_ctx[None], jnp.zeros((rows - b - 1, d), F32)], axis=0)
    mods = _ada(cc, ada_w, ada_b).reshape(depth, rows, 6, d)
    xs = jnp.concatenate([x, ctx], axis=1)

    tabs_a = _rope_tables(n_lat, n_ctx, MLA_ROPE, (MLA_NOPE,))
    tabs_b = _rope_tables(n_lat, n_ctx, DIFF_HEAD_DIM, (0, DIFF_HEAD_DIM))

    for i in range(depth):
        kind, j = i % N_MIXERS, i // N_MIXERS
        last = i == depth - 1
        with_ctx = not last
        q_tiles = (n_lat + (n_ctx if with_ctx else 0)) // ROW_TILE
        mod = jnp.stack([mods[i, :b], jnp.broadcast_to(mods[i, b], (b, 6, d))], axis=1)
        g_mix = norm_mix_g[i].reshape(1, d)
        if kind == 0:
            w = _mla_weights(mla_w_in[j], mla_q_norm_g[j], mla_w_uq[j], mla_kv_norm_g[j], mla_w_ukv[j])
            q, k, v = _mla_in(xs, g_mix, mod, w, _mla_tables(tabs_a, mla_qn_g[j], mla_kn_g[j]), n_lat)
            a = _attn_calls(_mla_attn_kernel, "mla_attn", [q], [k, v], [], n_lat, with_ctx,
                            MLA_HEADS, MLA_HEADS_PER_STEP, MLA_HEADS_PER_STEP * MLA_V)
            xs = _out_proj(xs, a, mla_w_out[j].astype(BF16), mod, n_lat, q_tiles)
        elif kind == 1:
            lam_init = 0.8 - 0.6 * math.exp(-0.3 * i)
            q0, q1, k, v = _diff_in(xs, g_mix, mod, _diff_weights(diff_w_in[j]),
                                    _diff_tables(tabs_b, diff_qn_g[j], diff_kn_g[j]), n_lat)
            wa = [a.reshape(1, -1) for a in (diff_lambda_q1[j], diff_lambda_k1[j], diff_lambda_q2[j],
                                             diff_lambda_k2[j], diff_sub_g[j])]
            a = _attn_calls(functools.partial(_diff_attn_kernel, lam_init=lam_init), "diff_attn", [q0, q1], [k, v],
                            wa, n_lat, with_ctx, DIFF_HEADS, DIFF_HEADS_PER_STEP, DIFF_HEADS_PER_STEP * LANE)
            xs = _out_proj(xs, a, diff_w_out[j].astype(BF16), mod, n_lat, q_tiles)
        else:
            w = {"win": sg_w_in[j].astype(BF16), "lng": sg_ln_g[j].reshape(1, -1), "lnb": sg_ln_b[j].reshape(1, -1),
                 "ws": sg_w_s[j].astype(BF16), "bs": sg_b_s[j].T, "wout": sg_w_out[j].astype(BF16)}
            xs = _sg(xs, g_mix, mod, w, n_lat)
        xs = _moe(xs, norm_ffn_g[i].reshape(1, d), mod, moe_router[i], moe_w_gate, moe_w_up, moe_w_down, i,
                  n_lat, with_ctx, last)
    return xs
```

```python
import functools
import math

import jax
import jax.numpy as jnp
import numpy as np
from jax import lax
from jax.experimental import pallas as pl
from jax.experimental.pallas import tpu as pltpu

F32 = jnp.float32
BF16 = jnp.bfloat16

GRID_W = 64
ROPE_BASE = 10000.0
EPS = 1e-6
N_MIXERS = 3

MLA_HEADS = 16
MLA_Q_LORA = 512
MLA_KV_LORA = 256
MLA_NOPE = 64
MLA_ROPE = 32
MLA_V = 64
MLA_QK = MLA_NOPE + MLA_ROPE

DIFF_HEADS = 8
DIFF_HEAD_DIM = 64
DIFF_V_DIM = 2 * DIFF_HEAD_DIM

SG_CHUNK = 128
SG_GROUPS = 8

N_EXPERTS = 16
CAPACITY_FACTOR = 2

LANE = 128
BF16_ROWS = 16
ROW_TILE = 256
PREFIX_CHUNK = 256
GATHER_WINDOW = 64
COMBINE_EXPERTS_PER_MATMUL = 4
MLA_HEADS_PER_STEP = 4
DIFF_HEADS_PER_STEP = 4
ATTN_Q_TILE = 256
LOG2E = math.log2(math.e)
VMEM_LIMIT = 56 * 1024 * 1024


def _cparams(n_axes):
    return pltpu.CompilerParams(dimension_semantics=("arbitrary",) * n_axes, vmem_limit_bytes=VMEM_LIMIT)


def _const_spec(shape):
    nd = len(shape)
    return pl.BlockSpec(shape, lambda *_: (0,) * nd)


def _dot(a, b):
    return jnp.dot(a, b, preferred_element_type=F32)


def _dot_nt(a, b):
    return lax.dot_general(a, b, (((1,), (1,)), ((), ())), preferred_element_type=F32)


def _rms(x, n):
    return x * lax.rsqrt(jnp.sum(x * x, axis=-1, keepdims=True) * (1.0 / n) + EPS)


def _norm_mod(x, g, shift, scale):
    return _rms(x, x.shape[-1]) * g * (1.0 + scale) + shift


def _silu(x):
    return x / (1.0 + jnp.exp(-x))


def _gelu_tanh(x):
    c = math.sqrt(2.0 / math.pi)
    return 0.5 * x * (1.0 + jnp.tanh(c * (x + 0.044715 * (x * x * x))))


def _ada_kernel(c_ref, w_ref, b_ref, o_ref):
    s = _silu(c_ref[...])
    o_ref[0] = jnp.dot(s, w_ref[0], preferred_element_type=F32, precision=lax.Precision.HIGHEST) + b_ref[0]


def _ada(cc, ada_w, ada_b):
    depth, d, six_d = ada_w.shape
    rows = cc.shape[0]
    tn = 1536
    return pl.pallas_call(
        _ada_kernel,
        grid=(depth, six_d // tn),
        in_specs=[
            _const_spec((rows, d)),
            pl.BlockSpec((1, d, tn), lambda i, j: (i, 0, j)),
            pl.BlockSpec((1, 1, tn), lambda i, j: (i, 0, j)),
        ],
        out_specs=pl.BlockSpec((1, rows, tn), lambda i, j: (i, 0, j)),
        out_shape=jax.ShapeDtypeStruct((depth, rows, six_d), F32),
        compiler_params=_cparams(2),
        name="ada",
    )(cc, ada_w, ada_b.reshape(depth, 1, six_d))


def _mod_spec(n_lat_tiles, d):
    return pl.BlockSpec((1, 1, 6, d), lambda b, i: (b, jnp.minimum(i // n_lat_tiles, 1), 0, 0))


def _mla_in_kernel(x_ref, g_ref, mod_ref, win_ref, qng_ref, wuq_ref, wuqp_ref, kvng_ref, wuk_ref, wuv_ref,
                   aq_ref, bq_ref, ak_ref, bk_ref, q_ref, k_ref, v_ref):
    mod = mod_ref[0, 0]
    h = _norm_mod(x_ref[0], g_ref[...], mod[0:1], mod[1:2])
    down = _dot(h.astype(BF16), win_ref[...])
    cq = (_rms(down[:, :MLA_Q_LORA], MLA_Q_LORA) * qng_ref[...]).astype(BF16)
    ckv = (_rms(down[:, MLA_Q_LORA:MLA_Q_LORA + MLA_KV_LORA], MLA_KV_LORA) * kvng_ref[...]).astype(BF16)
    q = _dot(cq, wuq_ref[...])
    qp = _dot(cq, wuqp_ref[...])
    kn = _dot(ckv, wuk_ref[...])
    v = _dot(ckv, wuv_ref[...])
    is_v = (lax.broadcasted_iota(jnp.int32, v.shape, 1) & (LANE - 1)) < MLA_V
    v_ref[0] = jnp.where(is_v, v, 1.0).astype(BF16)
    lat = MLA_Q_LORA + MLA_KV_LORA
    kr, krp = down[:, lat:lat + LANE], down[:, lat + LANE:]
    aq, bq, ak, bk = aq_ref[...], bq_ref[...], ak_ref[...], bk_ref[...]
    krot = krp * bk
    inv_n = 1.0 / MLA_QK
    for hd in range(MLA_HEADS):
        sl = slice(hd * LANE, (hd + 1) * LANE)
        qh = q[:, sl]
        rq = lax.rsqrt(jnp.sum(qh * qh, axis=-1, keepdims=True) * inv_n + EPS)
        q_ref[0, :, sl] = ((qh * aq + qp[:, sl] * bq) * rq).astype(BF16)
        kh = kn[:, sl] + kr
        rk = lax.rsqrt(jnp.sum(kh * kh, axis=-1, keepdims=True) * inv_n + EPS)
        k_ref[0, :, sl] = ((kh * ak + krot) * rk).astype(BF16)


def _mla_in(x, g, mod, w, tabs, n_lat):
    b, t, d = x.shape
    tm = ROW_TILE
    hw = MLA_HEADS * LANE
    row = lambda bb, i: (bb, i, 0)
    tab = pl.BlockSpec((tm, LANE), lambda bb, i: (i, 0))
    consts = [w["win"], w["qng"], w["wuq"], w["wuqp"], w["kvng"], w["wuk"], w["wuv"]]
    return pl.pallas_call(
        _mla_in_kernel,
        grid=(b, t // tm),
        in_specs=[pl.BlockSpec((1, tm, d), row), _const_spec((1, d)), _mod_spec(n_lat // tm, d)]
        + [_const_spec(a.shape) for a in consts] + [tab] * 4,
        out_specs=[pl.BlockSpec((1, tm, hw), row)] * 3,
        out_shape=[jax.ShapeDtypeStruct((b, t, hw), BF16)] * 3,
        compiler_params=_cparams(2),
        name="mla_in",
    )(x, g, mod, *consts, *tabs)


def _diff_in_kernel(x_ref, g_ref, mod_ref, win_ref, aq_ref, bq_ref, ak_ref, bk_ref, q0_ref, q1_ref, k_ref, v_ref):
    mod = mod_ref[0, 0]
    d = x_ref.shape[-1]
    h = _norm_mod(x_ref[0], g_ref[...], mod[0:1], mod[1:2])
    qkv = _dot(h.astype(BF16), win_ref[...])
    v_ref[0] = qkv[:, 2 * d:3 * d].astype(BF16)
    lo = lax.broadcasted_iota(jnp.int32, (x_ref.shape[1], LANE), 1) < DIFF_HEAD_DIM
    inv_n = 1.0 / DIFF_HEAD_DIM

    def normed_rope(off, hd, a, b):
        xh = qkv[:, off + hd * LANE:off + (hd + 1) * LANE]
        xp = qkv[:, off + 3 * d + hd * LANE:off + 3 * d + (hd + 1) * LANE]
        sq = xh * xh
        s_lo = jnp.sum(jnp.where(lo, sq, 0.0), axis=-1, keepdims=True)
        s_hi = jnp.sum(jnp.where(lo, 0.0, sq), axis=-1, keepdims=True)
        return (xh * a + xp * b) * lax.rsqrt(jnp.where(lo, s_lo, s_hi) * inv_n + EPS)

    aq, bq, ak, bk = aq_ref[...], bq_ref[...], ak_ref[...], bk_ref[...]
    for hd in range(DIFF_HEADS):
        sl = slice(hd * LANE, (hd + 1) * LANE)
        qh = normed_rope(0, hd, aq, bq)
        q0_ref[0, :, sl] = jnp.where(lo, qh, 0.0).astype(BF16)
        q1_ref[0, :, sl] = jnp.where(lo, 0.0, qh).astype(BF16)
        k_ref[0, :, sl] = normed_rope(d, hd, ak, bk).astype(BF16)


def _diff_in(x, g, mod, win, tabs, n_lat):
    b, t, d = x.shape
    tm = ROW_TILE
    row = lambda bb, i: (bb, i, 0)
    tab = pl.BlockSpec((tm, LANE), lambda bb, i: (i, 0))
    out = jax.ShapeDtypeStruct((b, t, d), BF16)
    return pl.pallas_call(
        _diff_in_kernel,
        grid=(b, t // tm),
        in_specs=[pl.BlockSpec((1, tm, d), row), _const_spec((1, d)), _mod_spec(n_lat // tm, d),
                  _const_spec(win.shape)] + [tab] * 4,
        out_specs=[pl.BlockSpec((1, tm, d), row)] * 4,
        out_shape=[out] * 4,
        compiler_params=_cparams(2),
        name="diff_in",
    )(x, g, mod, win, *tabs)


def _exp2_scores(q, k):
    s = _dot_nt(q, k)
    return jnp.exp2(s - jnp.max(s, axis=-1, keepdims=True)).astype(BF16)


def _mla_attn_kernel(q_ref, k_ref, v_ref, *rest):
    o_ref = rest[-1]
    tq = q_ref.shape[1]
    low = lax.broadcasted_iota(jnp.int32, (tq, LANE), 1) < MLA_V
    blocks = [slice(hd * LANE, (hd + 1) * LANE) for hd in range(MLA_HEADS_PER_STEP)]
    scores = [_dot_nt(q_ref[0, :, sl], k_ref[0, :, sl]) for sl in blocks]
    weights = [jnp.exp2(s - jnp.max(s, axis=-1, keepdims=True)).astype(BF16) for s in scores]
    for pr in range(MLA_HEADS_PER_STEP // 2):
        halves = []
        for hd in (2 * pr, 2 * pr + 1):
            r = _dot(weights[hd], v_ref[0, :, blocks[hd]])
            halves.append(r / pltpu.roll(r, LANE - MLA_V, 1))
        pair = jnp.where(low, halves[0], pltpu.roll(halves[1], MLA_V, 1))
        o_ref[0, :, pr * LANE:(pr + 1) * LANE] = pair.astype(o_ref.dtype)


def _attn_calls(kernel, name, qs, kvs, consts, n_lat, with_ctx, heads, hs, out_width):
    b, t, _ = kvs[0].shape
    n_ctx = t - n_lat
    ctx_blk = n_lat // n_ctx
    out_shape = jax.ShapeDtypeStruct((b, t, heads // hs * out_width), BF16)
    cspecs = [_const_spec(a.shape) for a in consts]

    def call(tq, q_map, kv_rows, kv_map, n_tiles, prev):
        q_specs = [pl.BlockSpec((1, tq, a.shape[-1] // (heads // hs)), q_map) for a in qs]
        kv_specs = [pl.BlockSpec((1, kv_rows, a.shape[-1] // (heads // hs)), kv_map) for a in kvs]
        extra = [] if prev is None else [pl.BlockSpec(memory_space=pl.ANY)]
        n_in = len(qs) + len(kvs) + len(consts)
        return pl.pallas_call(
            kernel,
            grid=(b, heads // hs, n_tiles),
            in_specs=q_specs + kv_specs + cspecs + extra,
            out_specs=pl.BlockSpec((1, tq, out_width), q_map),
            out_shape=out_shape,
            input_output_aliases={} if prev is None else {n_in: 0},
            compiler_params=_cparams(3),
            name=name,
        )(*qs, *kvs, *consts, *([] if prev is None else [prev]))

    tq = min(ATTN_Q_TILE, n_lat)
    out = call(tq, lambda bb, hg, i: (bb, i, hg), t, lambda bb, hg, i: (bb, 0, hg), n_lat // tq, None)
    if with_ctx:
        ctx_map = lambda bb, hg, i: (bb, ctx_blk, hg)
        out = call(n_ctx, ctx_map, n_ctx, ctx_map, 1, out)
    return out


def _diff_attn_kernel(q0_ref, q1_ref, k_ref, v_ref, lq1_ref, lk1_ref, lq2_ref, lk2_ref, sub_ref, *rest, lam_init):
    o_ref = rest[-1]
    lam = (jnp.exp(jnp.sum(lq1_ref[...] * lk1_ref[...], axis=-1, keepdims=True))
           - jnp.exp(jnp.sum(lq2_ref[...] * lk2_ref[...], axis=-1, keepdims=True)) + lam_init)
    for a in range(DIFF_HEADS_PER_STEP):
        sl = slice(a * LANE, (a + 1) * LANE)
        k, v = k_ref[0, :, sl], v_ref[0, :, sl]
        scores = [_dot_nt(q_ref[0, :, sl], k) for q_ref in (q0_ref, q1_ref)]
        outs = []
        for s in scores:
            p = jnp.exp2(s - jnp.max(s, axis=-1, keepdims=True))
            outs.append(_dot(p.astype(BF16), v) / jnp.sum(p, axis=-1, keepdims=True))
        o = outs[0] - lam * outs[1]
        o_ref[0, :, sl] = (_rms(o, DIFF_V_DIM) * sub_ref[...] * (1.0 - lam_init)).astype(o_ref.dtype)


def _out_proj_kernel(x_ref, a_ref, w_ref, mod_ref, o_ref):
    gate = mod_ref[0, 0][2:3]
    o_ref[0] = x_ref[0] + gate * _dot(a_ref[0], w_ref[...])


def _out_proj(x, a, w, mod, n_lat, n_tiles):
    b, t, d = x.shape
    tm = ROW_TILE
    row = lambda bb, i: (bb, i, 0)
    return pl.pallas_call(
        _out_proj_kernel,
        grid=(b, n_tiles),
        in_specs=[pl.BlockSpec((1, tm, d), row), pl.BlockSpec((1, tm, a.shape[-1]), row), _const_spec(w.shape),
                  _mod_spec(n_lat // tm, d)],
        out_specs=pl.BlockSpec((1, tm, d), row),
        out_shape=jax.ShapeDtypeStruct(x.shape, F32),
        input_output_aliases={0: 0},
        compiler_params=_cparams(2),
        name="out_proj",
    )(x, a, w, mod)


def _sg_kernel(x_ref, g_ref, mod_ref, win_ref, lng_ref, lnb_ref, ws_ref, bs_ref, wout_ref, o_ref, gated_ref):
    mod = mod_ref[0, 0]
    x = x_ref[0]
    tm = x.shape[0]
    width = lng_ref.shape[-1]
    gdim = width // SG_GROUPS
    h = _norm_mod(x, g_ref[...], mod[0:1], mod[1:2])
    z = _gelu_tanh(_dot(h.astype(BF16), win_ref[...]))
    u, v = z[:, :width], z[:, width:]
    mu = jnp.mean(v, axis=-1, keepdims=True)
    vc = v - mu
    var = jnp.mean(vc * vc, axis=-1, keepdims=True)
    vn = (vc * lax.rsqrt(var + EPS) * lng_ref[...] + lnb_ref[...]).astype(BF16)
    for c in range(tm // SG_CHUNK):
        rows = slice(c * SG_CHUNK, (c + 1) * SG_CHUNK)
        for gi in range(SG_GROUPS):
            cols = slice(gi * gdim, (gi + 1) * gdim)
            mixed = _dot(ws_ref[gi], vn[rows, cols]) + bs_ref[:, gi:gi + 1]
            gated_ref[rows, cols] = (u[rows, cols] * mixed).astype(BF16)
    o_ref[0] = x + mod[2:3] * _dot(gated_ref[...], wout_ref[...])


def _sg(x, g, mod, w, n_lat):
    b, t, d = x.shape
    tm = ROW_TILE
    row = lambda bb, i: (bb, i, 0)
    consts = [w["win"], w["lng"], w["lnb"], w["ws"], w["bs"], w["wout"]]
    return pl.pallas_call(
        _sg_kernel,
        grid=(b, t // tm),
        in_specs=[pl.BlockSpec((1, tm, d), row), _const_spec((1, d)), _mod_spec(n_lat // tm, d)]
        + [_const_spec(a.shape) for a in consts],
        out_specs=pl.BlockSpec((1, tm, d), row),
        out_shape=jax.ShapeDtypeStruct(x.shape, F32),
        scratch_shapes=[pltpu.VMEM((tm, w["lng"].shape[-1]), BF16)],
        input_output_aliases={0: 0},
        compiler_params=_cparams(2),
        name="chunk_mlp",
    )(x, g, mod, *consts)


def _router_kernel(x_ref, g_ref, mod_ref, rhi_ref, rlo_ref, h_ref, aff_ref):
    mod = mod_ref[0, 0]
    h = _norm_mod(x_ref[0], g_ref[...], mod[3:4], mod[4:5])
    h_hi = h.astype(BF16)
    h_lo = (h - h_hi.astype(F32)).astype(BF16)
    h_ref[0] = h_hi
    logits = _dot(h_hi, rhi_ref[...]) + (_dot(h_lo, rhi_ref[...]) + _dot(h_hi, rlo_ref[...]))
    e = jnp.exp(logits - jnp.max(logits, axis=-1, keepdims=True))
    aff_ref[0] = e / jnp.sum(e, axis=-1, keepdims=True)


def _router(x, g, mod, router, n_lat, n_tiles):
    b, t, d = x.shape
    tm = ROW_TILE
    ne = router.shape[-1]
    row = lambda bb, i: (bb, i, 0)
    r_hi = router.astype(BF16)
    r_lo = (router - r_hi.astype(F32)).astype(BF16)
    return pl.pallas_call(
        _router_kernel,
        grid=(b, n_tiles),
        in_specs=[pl.BlockSpec((1, tm, d), row), _const_spec((1, d)), _mod_spec(n_lat // tm, d),
                  _const_spec(router.shape), _const_spec(router.shape)],
        out_specs=[pl.BlockSpec((1, tm, d), row), pl.BlockSpec((1, tm, ne), row)],
        out_shape=[jax.ShapeDtypeStruct((b, n_tiles * tm, d), BF16), jax.ShapeDtypeStruct((b, n_tiles * tm, ne), F32)],
        compiler_params=_cparams(2),
        name="router",
    )(x, g, mod, r_hi, r_lo)


def _prefix_counts(mask, tri):
    out = []
    carry = jnp.zeros((mask.shape[0], 1), F32)
    before = [carry]
    for c in range(mask.shape[1] // PREFIX_CHUNK):
        m = mask[:, c * PREFIX_CHUNK:(c + 1) * PREFIX_CHUNK]
        out.append(_dot(m.astype(BF16), tri) + carry)
        carry = carry + jnp.sum(m, axis=-1, keepdims=True)
        before.append(carry)
    return out, before


def _topk_kernel(aff_ref, pos_ref, cnt_ref, *, cap):
    a = aff_ref[0]
    ne = a.shape[0]
    bits = pltpu.bitcast(a, jnp.int32)

    def step(i, lo):
        cand = lo | jnp.left_shift(jnp.int32(1), 30 - i)
        cnt = jnp.sum(jnp.where(bits >= cand, 1.0, 0.0), axis=-1, keepdims=True)
        return jnp.where(cnt >= cap, cand, lo)

    thr = lax.fori_loop(0, 31, step, jnp.zeros((ne, 1), jnp.int32))
    gt = jnp.where(bits > thr, 1.0, 0.0)
    eq = jnp.where(bits == thr, 1.0, 0.0)
    room = cap - jnp.sum(gt, axis=-1, keepdims=True)
    ri = lax.broadcasted_iota(jnp.int32, (PREFIX_CHUNK, PREFIX_CHUNK), 0)
    ci = lax.broadcasted_iota(jnp.int32, (PREFIX_CHUNK, PREFIX_CHUNK), 1)
    tri = jnp.where(ri <= ci, 1.0, 0.0).astype(BF16)
    eq_rank, _ = _prefix_counts(eq, tri)
    sel = jnp.concatenate([
        jnp.maximum(gt[:, c * PREFIX_CHUNK:(c + 1) * PREFIX_CHUNK],
                    jnp.where(r <= room, eq[:, c * PREFIX_CHUNK:(c + 1) * PREFIX_CHUNK], 0.0))
        for c, r in enumerate(eq_rank)], axis=-1)
    sel_rank, before = _prefix_counts(sel, tri)
    for c, r in enumerate(sel_rank):
        cols = slice(c * PREFIX_CHUNK, (c + 1) * PREFIX_CHUNK)
        pos_ref[0, :, cols] = jnp.where(sel[:, cols] > 0.0, r - 1.0, -1.0).astype(jnp.int32)
    for c, cnt in enumerate(before):
        cnt_ref[0, :, c:c + 1] = cnt.astype(jnp.int32)


def _topk(aff_t, cap):
    b, ne, n = aff_t.shape
    spec = pl.BlockSpec((1, ne, n), lambda bb: (bb, 0, 0))
    n_cnt = n // PREFIX_CHUNK + 1
    return pl.pallas_call(
        functools.partial(_topk_kernel, cap=cap),
        grid=(b,),
        in_specs=[spec],
        out_specs=[spec, pl.BlockSpec((1, ne, n_cnt), lambda bb: (bb, 0, 0))],
        out_shape=[jax.ShapeDtypeStruct((b, ne, n), jnp.int32), jax.ShapeDtypeStruct((b, ne, n_cnt), jnp.int32)],
        compiler_params=_cparams(1),
        name="topk",
    )(aff_t)


def _moe_gather_kernel(cnt_ref, h_ref, pos_ref, aff_ref, xg_ref, gate_ref, *, cap, n_cnt):
    i = pl.program_id(1)
    ne, tn = pos_ref.shape[1:]
    win = min(GATHER_WINDOW, cap)
    row = (pl.program_id(0) * ne) * n_cnt + i

    @pl.when(i == 0)
    def _():
        xg_ref[...] = jnp.zeros(xg_ref.shape, xg_ref.dtype)
        gate_ref[...] = jnp.zeros(gate_ref.shape, gate_ref.dtype)

    h = h_ref[0]
    slot = lax.broadcasted_iota(jnp.int32, (win, tn), 0)

    def add_window(e, start, hit, res):
        rows = pl.ds(start, win)
        xg_ref[0, e, rows, :] = (xg_ref[0, e, rows, :].astype(F32) + res).astype(BF16)
        gate_ref[0, e, rows, :] += jnp.sum(jnp.where(hit, aff_ref[0, e:e + 1, :], 0.0), axis=-1, keepdims=True)

    starts, his, hits = [], [], []
    for e in range(ne):
        lo, hi = cnt_ref[row + e * n_cnt], cnt_ref[row + e * n_cnt + 1]
        start = pl.multiple_of(jnp.minimum((lo // BF16_ROWS) * BF16_ROWS, cap - win), BF16_ROWS)
        starts.append(start)
        his.append(hi)
        hits.append(pos_ref[0, e:e + 1, :] - start == slot)
    onehot = jnp.concatenate([jnp.where(hit, 1.0, 0.0).astype(BF16) for hit in hits], axis=0)
    res = _dot(onehot, h)
    for e in range(ne):
        add_window(e, starts[e], hits[e], res[e * win:(e + 1) * win])

    for e in range(ne):
        first = starts[e] + win
        n_more = jnp.maximum(his[e] - first + (win - 1), 0) // win

        def more(k, carry, e=e, first=first):
            lower = first + k * win
            start = pl.multiple_of(jnp.minimum(lower, cap - win), BF16_ROWS)
            p = pos_ref[0, e:e + 1, :]
            hit = jnp.logical_and(p - start == slot, p >= lower)
            add_window(e, start, hit, _dot(jnp.where(hit, 1.0, 0.0).astype(BF16), h))
            return carry

        lax.fori_loop(0, n_more, more, 0)


def _moe_gather(h, pos, aff_t, cnt, set_block, n_set, cap):
    b, _, d = h.shape
    ne = pos.shape[1]
    tn = PREFIX_CHUNK
    tiles = n_set // tn
    tile = pl.BlockSpec((1, ne, tn), lambda bb, i, c: (bb, 0, i))
    return pl.pallas_call(
        functools.partial(_moe_gather_kernel, cap=cap, n_cnt=cnt.shape[-1]),
        grid_spec=pltpu.PrefetchScalarGridSpec(
            num_scalar_prefetch=1,
            grid=(b, tiles),
            in_specs=[pl.BlockSpec((1, tn, d), lambda bb, i, c: (bb, set_block * tiles + i, 0)), tile, tile],
            out_specs=[pl.BlockSpec((1, ne, cap, d), lambda bb, i, c: (bb, 0, 0, 0)),
                       pl.BlockSpec((1, ne, cap, 1), lambda bb, i, c: (bb, 0, 0, 0))]),
        out_shape=[jax.ShapeDtypeStruct((b, ne, cap, d), BF16), jax.ShapeDtypeStruct((b, ne, cap, 1), F32)],
        compiler_params=_cparams(2),
        name="moe_gather",
    )(cnt.reshape(-1), h, pos, aff_t)


def _moe_ffn_kernel(xg_ref, gate_ref, wg32_ref, wu32_ref, wd32_ref, y_ref, wg_ref, wu_ref, wd_ref):
    @pl.when(pl.program_id(1) == 0)
    def _():
        wg_ref[...] = wg32_ref[0, 0].astype(BF16)
        wu_ref[...] = wu32_ref[0, 0].astype(BF16)
        wd_ref[...] = wd32_ref[0, 0].astype(BF16)

    bg, _, cap, _ = xg_ref.shape
    rows = lambda ref: ref[0, 0] if bg == 1 else jnp.concatenate([ref[bb, 0] for bb in range(bg)], axis=0)
    xg = rows(xg_ref)
    hid = _silu(_dot(xg, wg_ref[...])) * _dot(xg, wu_ref[...])
    y = _dot(hid.astype(BF16), wd_ref[...]) * rows(gate_ref)
    for bb in range(bg):
        y_ref[bb, 0] = y[bb * cap:(bb + 1) * cap].astype(BF16)


def _moe_ffn(xg, gates, wg, wu, wd, layer, bg):
    b, ne, cap, d = xg.shape
    ff = wg.shape[-1]
    tok = pl.BlockSpec((bg, 1, cap, d), lambda e, i: (i, e, 0, 0))
    return pl.pallas_call(
        _moe_ffn_kernel,
        grid=(ne, b // bg),
        in_specs=[tok, pl.BlockSpec((bg, 1, cap, 1), lambda e, i: (i, e, 0, 0)),
                  pl.BlockSpec((1, 1, d, ff), lambda e, i: (layer, e, 0, 0)),
                  pl.BlockSpec((1, 1, d, ff), lambda e, i: (layer, e, 0, 0)),
                  pl.BlockSpec((1, 1, ff, d), lambda e, i: (layer, e, 0, 0))],
        out_specs=tok,
        out_shape=jax.ShapeDtypeStruct((b, ne, cap, d), BF16),
        scratch_shapes=[pltpu.VMEM((d, ff), BF16), pltpu.VMEM((d, ff), BF16), pltpu.VMEM((ff, d), BF16)],
        compiler_params=_cparams(2),
        name="moe_ffn",
    )(xg, gates, wg, wu, wd)


def _moe_combine_kernel(cnt_ref, x_ref, pos_ref, y_ref, mod_ref, o_ref, *, cap, n_cnt):
    pos = pos_ref[0]
    tn, ne = pos.shape
    win = min(GATHER_WINDOW, cap)
    per = COMBINE_EXPERTS_PER_MATMUL
    gate = mod_ref[0, 0][5:6]
    row = (pl.program_id(0) * ne) * n_cnt + pl.program_id(1)
    lane = lax.broadcasted_iota(jnp.int32, (tn, per * win), 1)
    acc = jnp.zeros(x_ref.shape[1:], F32)
    overflow = False
    for g in range(ne // per):
        onehot = jnp.zeros((tn, per * win), F32)
        parts = []
        for j in range(per):
            e = g * per + j
            lo, hi = cnt_ref[row + e * n_cnt], cnt_ref[row + e * n_cnt + 1]
            start = pl.multiple_of(jnp.minimum((lo // BF16_ROWS) * BF16_ROWS, cap - win), BF16_ROWS)
            overflow = jnp.logical_or(overflow, hi > start + win)
            rel = pos[:, e:e + 1] - start
            target = jnp.where(jnp.logical_and(rel >= 0, rel < win), rel + j * win, -1)
            onehot = jnp.where(target == lane, 1.0, onehot)
            parts.append(y_ref[0, e, pl.ds(start, win), :])
        acc = acc + _dot(onehot.astype(BF16), jnp.concatenate(parts, axis=0))
    o_ref[0] = x_ref[0] + gate * acc

    @pl.when(overflow)
    def _():
        lane_all = lax.broadcasted_iota(jnp.int32, (tn, cap), 1)
        full = jnp.zeros(x_ref.shape[1:], F32)
        for e in range(ne):
            onehot = jnp.where(pos[:, e:e + 1] == lane_all, 1.0, 0.0).astype(BF16)
            full = full + _dot(onehot, y_ref[0, e])
        o_ref[0] = x_ref[0] + gate * full


def _moe_combine(x, pos_t, cnt, y, mod, tile_off, n_set, mod_sel, out_rows):
    b, t, d = x.shape
    _, ne, cap, _ = y.shape
    tn = PREFIX_CHUNK
    n_cnt = cnt.shape[-1]
    alias = out_rows == t
    out_off = tile_off if alias else 0
    return pl.pallas_call(
        functools.partial(_moe_combine_kernel, cap=cap, n_cnt=n_cnt),
        grid_spec=pltpu.PrefetchScalarGridSpec(
            num_scalar_prefetch=1,
            grid=(b, n_set // tn),
            in_specs=[pl.BlockSpec((1, tn, d), lambda bb, i, c: (bb, i + tile_off, 0)),
                      pl.BlockSpec((1, tn, ne), lambda bb, i, c: (bb, i, 0)),
                      pl.BlockSpec((1, ne, cap, d), lambda bb, i, c: (bb, 0, 0, 0)),
                      pl.BlockSpec((1, 1, 6, d), lambda bb, i, c: (bb, mod_sel, 0, 0))],
            out_specs=pl.BlockSpec((1, tn, d), lambda bb, i, c: (bb, i + out_off, 0))),
        out_shape=jax.ShapeDtypeStruct((b, out_rows, d), F32),
        input_output_aliases={1: 0} if alias else {},
        compiler_params=_cparams(2),
        name="moe_combine",
    )(cnt.reshape(-1), x, pos_t, y, mod)


def _moe(x, g, mod, router, wg, wu, wd, layer, n_lat, with_ctx, final):
    b, t, d = x.shape
    n_ctx = t - n_lat
    tm = ROW_TILE
    h, aff = _router(x, g, mod, router, n_lat, (t if with_ctx else n_lat) // tm)
    aff_t = jnp.swapaxes(aff, 1, 2)
    sets = [(0, n_lat, 0, 1)]
    if with_ctx:
        sets.append((n_lat, n_ctx, 1, b))
    for start, n_set, mod_sel, bg in sets:
        cap = CAPACITY_FACTOR * n_set // N_EXPERTS
        a_set = aff_t[:, :, start:start + n_set]
        pos, cnt = _topk(a_set, cap)
        xg, gates = _moe_gather(h, pos, a_set, cnt, start // n_set, n_set, cap)
        y = _moe_ffn(xg, gates, wg, wu, wd, layer, bg)
        x = _moe_combine(x, jnp.swapaxes(pos, 1, 2), cnt, y, mod, start // PREFIX_CHUNK, n_set, mod_sel,
                         n_lat if final else t)
    return x


def _rope_tables(n_lat, n_ctx, rot_dim, starts):
    n_rows = n_lat // GRID_W
    rows = np.repeat(np.arange(n_rows, dtype=np.float32), GRID_W)
    cols = np.tile(np.arange(GRID_W, dtype=np.float32), n_rows)
    n_freq = rot_dim // 4
    inv_freq = (np.float32(ROPE_BASE) ** (-np.arange(n_freq, dtype=np.float32) / np.float32(n_freq))).astype(np.float32)
    ang = np.concatenate([rows[:, None] * inv_freq, cols[:, None] * inv_freq], axis=-1)
    half = rot_dim // 2
    cos_l, sin_l = np.cos(ang).astype(np.float32), np.sin(ang).astype(np.float32)
    cos = np.ones((n_lat + n_ctx, LANE), np.float32)
    sa = np.zeros((n_lat + n_ctx, LANE), np.float32)
    sb = np.zeros((n_lat + n_ctx, LANE), np.float32)
    for s in starts:
        cos[:n_lat, s:s + half] = cos_l
        cos[:n_lat, s + half:s + rot_dim] = cos_l
        sa[:n_lat, s:s + half] = -sin_l
        sb[:n_lat, s + half:s + rot_dim] = sin_l
    return jnp.asarray(cos), jnp.asarray(sa), jnp.asarray(sb)


def _pad_heads(w, heads, width):
    k = w.shape[0]
    w = w.reshape(k, heads, width)
    return jnp.pad(w, ((0, 0), (0, 0), (0, LANE - width))).reshape(k, heads * LANE)


def _mla_partner(a):
    half = MLA_ROPE // 2
    lane = jnp.arange(LANE)
    first = (lane >= MLA_NOPE) & (lane < MLA_NOPE + half)
    second = (lane >= MLA_NOPE + half) & (lane < MLA_QK)
    return jnp.where(first, jnp.roll(a, -half, axis=-1), jnp.where(second, jnp.roll(a, half, axis=-1), 0.0))


def _mla_weights(w_in, q_norm_g, w_uq, kv_norm_g, w_ukv):
    d = w_in.shape[0]
    lat = MLA_Q_LORA + MLA_KV_LORA
    rope_block = jnp.concatenate([jnp.zeros((d, MLA_NOPE), F32), w_in[:, lat:],
                                  jnp.zeros((d, LANE - MLA_QK), F32)], axis=1)
    ukv = w_ukv.reshape(MLA_KV_LORA, MLA_HEADS, MLA_NOPE + MLA_V)
    wuq = _pad_heads(w_uq, MLA_HEADS, MLA_QK)
    wuqp = _mla_partner(wuq.reshape(-1, MLA_HEADS, LANE)).reshape(wuq.shape)
    return {
        "win": jnp.concatenate([w_in[:, :lat], rope_block, _mla_partner(rope_block)], axis=1).astype(BF16),
        "qng": q_norm_g.reshape(1, -1),
        "wuq": wuq.astype(BF16),
        "wuqp": wuqp.astype(BF16),
        "kvng": kv_norm_g.reshape(1, -1),
        "wuk": _pad_heads(ukv[:, :, :MLA_NOPE].reshape(MLA_KV_LORA, -1), MLA_HEADS, MLA_NOPE).astype(BF16),
        "wuv": _pad_heads(ukv[:, :, MLA_NOPE:].reshape(MLA_KV_LORA, -1), MLA_HEADS, MLA_V).astype(BF16),
    }


def _diff_partner(a):
    half = DIFF_HEAD_DIM // 2
    shape = a.shape
    a = a.reshape(shape[:-1] + (shape[-1] // LANE, LANE))
    first = (jnp.arange(LANE) % DIFF_HEAD_DIM) < half
    return jnp.where(first, jnp.roll(a, -half, axis=-1), jnp.roll(a, half, axis=-1)).reshape(shape)


def _diff_weights(w_in):
    d = w_in.shape[0]
    return jnp.concatenate([w_in, _diff_partner(w_in[:, :2 * d])], axis=1).astype(BF16)


def _diff_tables(tabs, qn_g, kn_g):
    cos, sa, sb = tabs
    out = []
    for g, scale in ((qn_g, DIFF_HEAD_DIM ** -0.5 * LOG2E), (kn_g, 1.0)):
        g2 = jnp.tile(g, 2)
        out += [cos * g2 * scale, (sa + sb) * _diff_partner(g2) * scale]
    return out


def _mla_tables(tabs, qn_g, kn_g):
    cos, sa, sb = tabs
    out = []
    for g, scale in ((qn_g, MLA_QK ** -0.5 * LOG2E), (kn_g, 1.0)):
        gp = jnp.pad(g, (0, LANE - MLA_QK))
        out += [cos * gp * scale, (sa + sb) * _mla_partner(gp) * scale]
    return out


def kernel(x, c, ctx, c_ctx, ada_w, ada_b, norm_mix_g, norm_ffn_g, mla_w_in, mla_q_norm_g, mla_w_uq, mla_kv_norm_g, mla_w_ukv, mla_qn_g, mla_kn_g, mla_w_out, diff_w_in, diff_qn_g, diff_kn_g, diff_lambda_q1, diff_lambda_k1, diff_lambda_q2, diff_lambda_k2, diff_sub_g, diff_w_out, sg_w_in, sg_ln_g, sg_ln_b, sg_w_s, sg_b_s, sg_w_out, moe_router, moe_w_gate, moe_w_up, moe_w_down):
    b, n_lat, d = x.shape
    n_ctx = ctx.shape[1]
    depth = ada_w.shape[0]
    assert n_lat % ROW_TILE == 0 and n_ctx == ROW_TILE and n_lat % n_ctx == 0

    rows = -(-(b + 1) // 8) * 8
    cc = jnp.concatenate([c, c_ctx[None], jnp.zeros((rows - b - 1, d), F32)], axis=0)
    mods = _ada(cc, ada_w, ada_b).reshape(depth, rows, 6, d)
    xs = jnp.concatenate([x, ctx], axis=1)

    tabs_a = _rope_tables(n_lat, n_ctx, MLA_ROPE, (MLA_NOPE,))
    tabs_b = _rope_tables(n_lat, n_ctx, DIFF_HEAD_DIM, (0, DIFF_HEAD_DIM))

    for i in range(depth):
        kind, j = i % N_MIXERS, i // N_MIXERS
        last = i == depth - 1
        with_ctx = not last
        q_tiles = (n_lat + (n_ctx if with_ctx else 0)) // ROW_TILE
        mod = jnp.stack([mods[i, :b], jnp.broadcast_to(mods[i, b], (b, 6, d))], axis=1)
        g_mix = norm_mix_g[i].reshape(1, d)
        if kind == 0:
            w = _mla_weights(mla_w_in[j], mla_q_norm_g[j], mla_w_uq[j], mla_kv_norm_g[j], mla_w_ukv[j])
            q, k, v = _mla_in(xs, g_mix, mod, w, _mla_tables(tabs_a, mla_qn_g[j], mla_kn_g[j]), n_lat)
            a = _attn_calls(_mla_attn_kernel, "mla_attn", [q], [k, v], [], n_lat, with_ctx,
                            MLA_HEADS, MLA_HEADS_PER_STEP, MLA_HEADS_PER_STEP * MLA_V)
            xs = _out_proj(xs, a, mla_w_out[j].astype(BF16), mod, n_lat, q_tiles)
        elif kind == 1:
            lam_init = 0.8 - 0.6 * math.exp(-0.3 * i)
            q0, q1, k, v = _diff_in(xs, g_mix, mod, _diff_weights(diff_w_in[j]),
                                    _diff_tables(tabs_b, diff_qn_g[j], diff_kn_g[j]), n_lat)
            wa = [a.reshape(1, -1) for a in (diff_lambda_q1[j], diff_lambda_k1[j], diff_lambda_q2[j],
                                             diff_lambda_k2[j], diff_sub_g[j])]
            a = _attn_calls(functools.partial(_diff_attn_kernel, lam_init=lam_init), "diff_attn", [q0, q1], [k, v],
                            wa, n_lat, with_ctx, DIFF_HEADS, DIFF_HEADS_PER_STEP, DIFF_HEADS_PER_STEP * LANE)
            xs = _out_proj(xs, a, diff_w_out[j].astype(BF16), mod, n_lat, q_tiles)
        else:
            w = {"win": sg_w_in[j].astype(BF16), "lng": sg_ln_g[j].reshape(1, -1), "lnb": sg_ln_b[j].reshape(1, -1),
                 "ws": sg_w_s[j].astype(BF16), "bs": sg_b_s[j].T, "wout": sg_w_out[j].astype(BF16)}
            xs = _sg(xs, g_mix, mod, w, n_lat)
        xs = _moe(xs, norm_ffn_g[i].reshape(1, d), mod, moe_router[i], moe_w_gate, moe_w_up, moe_w_down, i,
                  n_lat, with_ctx, last)
    return xs
```

```python
import functools
import math

import jax
import jax.numpy as jnp
import numpy as np
from jax import lax
from jax.experimental import pallas as pl
from jax.experimental.pallas import tpu as pltpu

F32 = jnp.float32
BF16 = jnp.bfloat16

GRID_W = 64
ROPE_BASE = 10000.0
EPS = 1e-6
N_MIXERS = 3

MLA_HEADS = 16
MLA_Q_LORA = 512
MLA_KV_LORA = 256
MLA_NOPE = 64
MLA_ROPE = 32
MLA_V = 64
MLA_QK = MLA_NOPE + MLA_ROPE

DIFF_HEADS = 8
DIFF_HEAD_DIM = 64
DIFF_V_DIM = 2 * DIFF_HEAD_DIM

SG_CHUNK = 128
SG_GROUPS = 8

N_EXPERTS = 16
CAPACITY_FACTOR = 2

LANE = 128
BF16_ROWS = 16
ROW_TILE = 256
PREFIX_CHUNK = 256
GATHER_WINDOW = 64
COMBINE_EXPERTS_PER_MATMUL = 4
MLA_HEADS_PER_STEP = 4
DIFF_HEADS_PER_STEP = 4
ATTN_Q_TILE = 256
LOG2E = math.log2(math.e)
VMEM_LIMIT = 56 * 1024 * 1024


def _cparams(n_axes):
    return pltpu.CompilerParams(dimension_semantics=("arbitrary",) * n_axes, vmem_limit_bytes=VMEM_LIMIT)


def _const_spec(shape):
    nd = len(shape)
    return pl.BlockSpec(shape, lambda *_: (0,) * nd)


def _dot(a, b):
    return jnp.dot(a, b, preferred_element_type=F32)


def _dot_nt(a, b):
    return lax.dot_general(a, b, (((1,), (1,)), ((), ())), preferred_element_type=F32)


def _rms(x, n):
    return x * lax.rsqrt(jnp.sum(x * x, axis=-1, keepdims=True) * (1.0 / n) + EPS)


def _norm_mod(x, g, shift, scale):
    return _rms(x, x.shape[-1]) * g * (1.0 + scale) + shift


def _silu(x):
    return x / (1.0 + jnp.exp(-x))


def _gelu_tanh(x):
    c = math.sqrt(2.0 / math.pi)
    return 0.5 * x * (1.0 + jnp.tanh(c * (x + 0.044715 * (x * x * x))))


def _ada_kernel(c_ref, w_ref, b_ref, o_ref):
    s = _silu(c_ref[...])
    o_ref[0] = jnp.dot(s, w_ref[0], preferred_element_type=F32, precision=lax.Precision.HIGHEST) + b_ref[0]


def _ada(cc, ada_w, ada_b):
    depth, d, six_d = ada_w.shape
    rows = cc.shape[0]
    tn = 1536
    return pl.pallas_call(
        _ada_kernel,
        grid=(depth, six_d // tn),
        in_specs=[
            _const_spec((rows, d)),
            pl.BlockSpec((1, d, tn), lambda i, j: (i, 0, j)),
            pl.BlockSpec((1, 1, tn), lambda i, j: (i, 0, j)),
        ],
        out_specs=pl.BlockSpec((1, rows, tn), lambda i, j: (i, 0, j)),
        out_shape=jax.ShapeDtypeStruct((depth, rows, six_d), F32),
        compiler_params=_cparams(2),
        name="ada",
    )(cc, ada_w, ada_b.reshape(depth, 1, six_d))


def _mod_spec(n_lat_tiles, d):
    return pl.BlockSpec((1, 1, 6, d), lambda b, i: (b, jnp.minimum(i // n_lat_tiles, 1), 0, 0))


def _mla_in_kernel(x_ref, g_ref, mod_ref, win_ref, qng_ref, wuq_ref, wuqp_ref, kvng_ref, wuk_ref, wuv_ref,
                   aq_ref, bq_ref, ak_ref, bk_ref, q_ref, k_ref, v_ref):
    mod = mod_ref[0, 0]
    h = _norm_mod(x_ref[0], g_ref[...], mod[0:1], mod[1:2])
    down = _dot(h.astype(BF16), win_ref[...])
    cq = (_rms(down[:, :MLA_Q_LORA], MLA_Q_LORA) * qng_ref[...]).astype(BF16)
    ckv = (_rms(down[:, MLA_Q_LORA:MLA_Q_LORA + MLA_KV_LORA], MLA_KV_LORA) * kvng_ref[...]).astype(BF16)
    q = _dot(cq, wuq_ref[...])
    qp = _dot(cq, wuqp_ref[...])
    kn = _dot(ckv, wuk_ref[...])
    v = _dot(ckv, wuv_ref[...])
    is_v = (lax.broadcasted_iota(jnp.int32, v.shape, 1) & (LANE - 1)) < MLA_V
    v_ref[0] = jnp.where(is_v, v, 1.0).astype(BF16)
    lat = MLA_Q_LORA + MLA_KV_LORA
    kr, krp = down[:, lat:lat + LANE], down[:, lat + LANE:]
    aq, bq, ak, bk = aq_ref[...], bq_ref[...], ak_ref[...], bk_ref[...]
    krot = krp * bk
    inv_n = 1.0 / MLA_QK
    for hd in range(MLA_HEADS):
        sl = slice(hd * LANE, (hd + 1) * LANE)
        qh = q[:, sl]
        rq = lax.rsqrt(jnp.sum(qh * qh, axis=-1, keepdims=True) * inv_n + EPS)
        q_ref[0, :, sl] = ((qh * aq + qp[:, sl] * bq) * rq).astype(BF16)
        kh = kn[:, sl] + kr
        rk = lax.rsqrt(jnp.sum(kh * kh, axis=-1, keepdims=True) * inv_n + EPS)
        k_ref[0, :, sl] = ((kh * ak + krot) * rk).astype(BF16)


def _mla_in(x, g, mod, w, tabs, n_lat):
    b, t, d = x.shape
    tm = ROW_TILE
    hw = MLA_HEADS * LANE
    row = lambda bb, i: (bb, i, 0)
    tab = pl.BlockSpec((tm, LANE), lambda bb, i: (i, 0))
    consts = [w["win"], w["qng"], w["wuq"], w["wuqp"], w["kvng"], w["wuk"], w["wuv"]]
    return pl.pallas_call(
        _mla_in_kernel,
        grid=(b, t // tm),
        in_specs=[pl.BlockSpec((1, tm, d), row), _const_spec((1, d)), _mod_spec(n_lat // tm, d)]
        + [_const_spec(a.shape) for a in consts] + [tab] * 4,
        out_specs=[pl.BlockSpec((1, tm, hw), row)] * 3,
        out_shape=[jax.ShapeDtypeStruct((b, t, hw), BF16)] * 3,
        compiler_params=_cparams(2),
        name="mla_in",
    )(x, g, mod, *consts, *tabs)


def _diff_in_kernel(x_ref, g_ref, mod_ref, win_ref, aq_ref, bq_ref, ak_ref, bk_ref, q0_ref, q1_ref, k_ref, v_ref):
    mod = mod_ref[0, 0]
    d = x_ref.shape[-1]
    h = _norm_mod(x_ref[0], g_ref[...], mod[0:1], mod[1:2])
    qkv = _dot(h.astype(BF16), win_ref[...])
    ones = jnp.ones((x_ref.shape[1], LANE), BF16)
    for hd in range(DIFF_HEADS):
        v_ref[0, :, 2 * hd * LANE:(2 * hd + 1) * LANE] = qkv[:, 2 * d + hd * LANE:2 * d + (hd + 1) * LANE].astype(BF16)
        v_ref[0, :, (2 * hd + 1) * LANE:(2 * hd + 2) * LANE] = ones
    lo = lax.broadcasted_iota(jnp.int32, (x_ref.shape[1], LANE), 1) < DIFF_HEAD_DIM
    inv_n = 1.0 / DIFF_HEAD_DIM

    def normed_rope(off, hd, a, b):
        xh = qkv[:, off + hd * LANE:off + (hd + 1) * LANE]
        xp = qkv[:, off + 3 * d + hd * LANE:off + 3 * d + (hd + 1) * LANE]
        sq = xh * xh
        s_lo = jnp.sum(jnp.where(lo, sq, 0.0), axis=-1, keepdims=True)
        s_hi = jnp.sum(jnp.where(lo, 0.0, sq), axis=-1, keepdims=True)
        return (xh * a + xp * b) * lax.rsqrt(jnp.where(lo, s_lo, s_hi) * inv_n + EPS)

    aq, bq, ak, bk = aq_ref[...], bq_ref[...], ak_ref[...], bk_ref[...]
    for hd in range(DIFF_HEADS):
        sl = slice(hd * LANE, (hd + 1) * LANE)
        qh = normed_rope(0, hd, aq, bq)
        q0_ref[0, :, sl] = jnp.where(lo, qh, 0.0).astype(BF16)
        q1_ref[0, :, sl] = jnp.where(lo, 0.0, qh).astype(BF16)
        k_ref[0, :, sl] = normed_rope(d, hd, ak, bk).astype(BF16)


def _diff_in(x, g, mod, win, tabs, n_lat):
    b, t, d = x.shape
    tm = ROW_TILE
    row = lambda bb, i: (bb, i, 0)
    tab = pl.BlockSpec((tm, LANE), lambda bb, i: (i, 0))
    out = jax.ShapeDtypeStruct((b, t, d), BF16)
    return pl.pallas_call(
        _diff_in_kernel,
        grid=(b, t // tm),
        in_specs=[pl.BlockSpec((1, tm, d), row), _const_spec((1, d)), _mod_spec(n_lat // tm, d),
                  _const_spec(win.shape)] + [tab] * 4,
        out_specs=[pl.BlockSpec((1, tm, d), row)] * 3 + [pl.BlockSpec((1, tm, 2 * d), row)],
        out_shape=[out] * 3 + [jax.ShapeDtypeStruct((b, t, 2 * d), BF16)],
        compiler_params=_cparams(2),
        name="diff_in",
    )(x, g, mod, win, *tabs)


def _exp2_scores(q, k):
    s = _dot_nt(q, k)
    return jnp.exp2(s - jnp.max(s, axis=-1, keepdims=True)).astype(BF16)


def _mla_attn_kernel(q_ref, k_ref, v_ref, *rest):
    o_ref = rest[-1]
    tq = q_ref.shape[1]
    low = lax.broadcasted_iota(jnp.int32, (tq, LANE), 1) < MLA_V
    blocks = [slice(hd * LANE, (hd + 1) * LANE) for hd in range(MLA_HEADS_PER_STEP)]
    scores = [_dot_nt(q_ref[0, :, sl], k_ref[0, :, sl]) for sl in blocks]
    weights = [jnp.exp2(s - jnp.max(s, axis=-1, keepdims=True)).astype(BF16) for s in scores]
    for pr in range(MLA_HEADS_PER_STEP // 2):
        halves = []
        for hd in (2 * pr, 2 * pr + 1):
            r = _dot(weights[hd], v_ref[0, :, blocks[hd]])
            halves.append(r / pltpu.roll(r, LANE - MLA_V, 1))
        pair = jnp.where(low, halves[0], pltpu.roll(halves[1], MLA_V, 1))
        o_ref[0, :, pr * LANE:(pr + 1) * LANE] = pair.astype(o_ref.dtype)


def _attn_calls(kernel, name, qs, kvs, consts, n_lat, with_ctx, heads, hs, out_width):
    b, t, _ = kvs[0].shape
    n_ctx = t - n_lat
    ctx_blk = n_lat // n_ctx
    out_shape = jax.ShapeDtypeStruct((b, t, heads // hs * out_width), BF16)
    cspecs = [_const_spec(a.shape) for a in consts]

    def call(tq, q_map, kv_rows, kv_map, n_tiles, prev):
        q_specs = [pl.BlockSpec((1, tq, a.shape[-1] // (heads // hs)), q_map) for a in qs]
        kv_specs = [pl.BlockSpec((1, kv_rows, a.shape[-1] // (heads // hs)), kv_map) for a in kvs]
        extra = [] if prev is None else [pl.BlockSpec(memory_space=pl.ANY)]
        n_in = len(qs) + len(kvs) + len(consts)
        return pl.pallas_call(
            kernel,
            grid=(b, heads // hs, n_tiles),
            in_specs=q_specs + kv_specs + cspecs + extra,
            out_specs=pl.BlockSpec((1, tq, out_width), q_map),
            out_shape=out_shape,
            input_output_aliases={} if prev is None else {n_in: 0},
            compiler_params=_cparams(3),
            name=name,
        )(*qs, *kvs, *consts, *([] if prev is None else [prev]))

    tq = min(ATTN_Q_TILE, n_lat)
    out = call(tq, lambda bb, hg, i: (bb, i, hg), t, lambda bb, hg, i: (bb, 0, hg), n_lat // tq, None)
    if with_ctx:
        ctx_map = lambda bb, hg, i: (bb, ctx_blk, hg)
        out = call(n_ctx, ctx_map, n_ctx, ctx_map, 1, out)
    return out


def _diff_attn_kernel(q0_ref, q1_ref, k_ref, v_ref, lq1_ref, lk1_ref, lq2_ref, lk2_ref, sub_ref, *rest, lam_init):
    o_ref = rest[-1]
    lam = (jnp.exp(jnp.sum(lq1_ref[...] * lk1_ref[...], axis=-1, keepdims=True))
           - jnp.exp(jnp.sum(lq2_ref[...] * lk2_ref[...], axis=-1, keepdims=True)) + lam_init)
    for a in range(DIFF_HEADS_PER_STEP):
        sl = slice(a * LANE, (a + 1) * LANE)
        k, v = k_ref[0, :, sl], v_ref[0, :, 2 * a * LANE:(2 * a + 2) * LANE]
        scores = [_dot_nt(q_ref[0, :, sl], k) for q_ref in (q0_ref, q1_ref)]
        outs = []
        for s in scores:
            r = _dot(jnp.exp2(s - jnp.max(s, axis=-1, keepdims=True)).astype(BF16), v)
            outs.append(r[:, :LANE] / r[:, LANE:])
        o = outs[0] - lam * outs[1]
        o_ref[0, :, sl] = (_rms(o, DIFF_V_DIM) * sub_ref[...] * (1.0 - lam_init)).astype(o_ref.dtype)


def _out_proj_kernel(x_ref, a_ref, w_ref, mod_ref, o_ref):
    gate = mod_ref[0, 0][2:3]
    o_ref[0] = x_ref[0] + gate * _dot(a_ref[0], w_ref[...])


def _out_proj(x, a, w, mod, n_lat, n_tiles):
    b, t, d = x.shape
    tm = ROW_TILE
    row = lambda bb, i: (bb, i, 0)
    return pl.pallas_call(
        _out_proj_kernel,
        grid=(b, n_tiles),
        in_specs=[pl.BlockSpec((1, tm, d), row), pl.BlockSpec((1, tm, a.shape[-1]), row), _const_spec(w.shape),
                  _mod_spec(n_lat // tm, d)],
        out_specs=pl.BlockSpec((1, tm, d), row),
        out_shape=jax.ShapeDtypeStruct(x.shape, F32),
        input_output_aliases={0: 0},
        compiler_params=_cparams(2),
        name="out_proj",
    )(x, a, w, mod)


def _sg_kernel(x_ref, g_ref, mod_ref, win_ref, lng_ref, lnb_ref, ws_ref, bs_ref, wout_ref, o_ref, gated_ref):
    mod = mod_ref[0, 0]
    x = x_ref[0]
    tm = x.shape[0]
    width = lng_ref.shape[-1]
    gdim = width // SG_GROUPS
    h = _norm_mod(x, g_ref[...], mod[0:1], mod[1:2])
    z = _gelu_tanh(_dot(h.astype(BF16), win_ref[...]))
    u, v = z[:, :width], z[:, width:]
    mu = jnp.mean(v, axis=-1, keepdims=True)
    vc = v - mu
    var = jnp.mean(vc * vc, axis=-1, keepdims=True)
    vn = (vc * lax.rsqrt(var + EPS) * lng_ref[...] + lnb_ref[...]).astype(BF16)
    for c in range(tm // SG_CHUNK):
        rows = slice(c * SG_CHUNK, (c + 1) * SG_CHUNK)
        for gi in range(SG_GROUPS):
            cols = slice(gi * gdim, (gi + 1) * gdim)
            mixed = _dot(ws_ref[gi], vn[rows, cols]) + bs_ref[:, gi:gi + 1]
            gated_ref[rows, cols] = (u[rows, cols] * mixed).astype(BF16)
    o_ref[0] = x + mod[2:3] * _dot(gated_ref[...], wout_ref[...])


def _sg(x, g, mod, w, n_lat):
    b, t, d = x.shape
    tm = ROW_TILE
    row = lambda bb, i: (bb, i, 0)
    consts = [w["win"], w["lng"], w["lnb"], w["ws"], w["bs"], w["wout"]]
    return pl.pallas_call(
        _sg_kernel,
        grid=(b, t // tm),
        in_specs=[pl.BlockSpec((1, tm, d), row), _const_spec((1, d)), _mod_spec(n_lat // tm, d)]
        + [_const_spec(a.shape) for a in consts],
        out_specs=pl.BlockSpec((1, tm, d), row),
        out_shape=jax.ShapeDtypeStruct(x.shape, F32),
        scratch_shapes=[pltpu.VMEM((tm, w["lng"].shape[-1]), BF16)],
        input_output_aliases={0: 0},
        compiler_params=_cparams(2),
        name="chunk_mlp",
    )(x, g, mod, *consts)


def _router_kernel(x_ref, g_ref, mod_ref, rhi_ref, rlo_ref, h_ref, aff_ref):
    mod = mod_ref[0, 0]
    h = _norm_mod(x_ref[0], g_ref[...], mod[3:4], mod[4:5])
    h_hi = h.astype(BF16)
    h_lo = (h - h_hi.astype(F32)).astype(BF16)
    h_ref[0] = h_hi
    logits = _dot(h_hi, rhi_ref[...]) + (_dot(h_lo, rhi_ref[...]) + _dot(h_hi, rlo_ref[...]))
    e = jnp.exp(logits - jnp.max(logits, axis=-1, keepdims=True))
    aff_ref[0] = e / jnp.sum(e, axis=-1, keepdims=True)


def _router(x, g, mod, router, n_lat, n_tiles):
    b, t, d = x.shape
    tm = PREFIX_CHUNK
    ne = router.shape[-1]
    row = lambda bb, i: (bb, i, 0)
    r_hi = router.astype(BF16)
    r_lo = (router - r_hi.astype(F32)).astype(BF16)
    return pl.pallas_call(
        _router_kernel,
        grid=(b, n_tiles),
        in_specs=[pl.BlockSpec((1, tm, d), row), _const_spec((1, d)), _mod_spec(n_lat // tm, d),
                  _const_spec(router.shape), _const_spec(router.shape)],
        out_specs=[pl.BlockSpec((1, tm, d), row), pl.BlockSpec((1, tm, ne), row)],
        out_shape=[jax.ShapeDtypeStruct((b, n_tiles * tm, d), BF16), jax.ShapeDtypeStruct((b, n_tiles * tm, ne), F32)],
        compiler_params=_cparams(2),
        name="router",
    )(x, g, mod, r_hi, r_lo)


def _prefix_counts(mask, tri):
    out = []
    carry = jnp.zeros((mask.shape[0], 1), F32)
    before = [carry]
    for c in range(mask.shape[1] // PREFIX_CHUNK):
        m = mask[:, c * PREFIX_CHUNK:(c + 1) * PREFIX_CHUNK]
        out.append(_dot(m.astype(BF16), tri) + carry)
        carry = carry + jnp.sum(m, axis=-1, keepdims=True)
        before.append(carry)
    return out, before


def _topk_kernel(aff_ref, pos_ref, cnt_ref, *, cap):
    a = aff_ref[0]
    ne = a.shape[0]
    bits = pltpu.bitcast(a, jnp.int32)

    def step(i, lo):
        cand = lo | jnp.left_shift(jnp.int32(1), 30 - i)
        cnt = jnp.sum(jnp.where(bits >= cand, 1.0, 0.0), axis=-1, keepdims=True)
        return jnp.where(cnt >= cap, cand, lo)

    thr = lax.fori_loop(0, 31, step, jnp.zeros((ne, 1), jnp.int32))
    gt = jnp.where(bits > thr, 1.0, 0.0)
    eq = jnp.where(bits == thr, 1.0, 0.0)
    room = cap - jnp.sum(gt, axis=-1, keepdims=True)
    ri = lax.broadcasted_iota(jnp.int32, (PREFIX_CHUNK, PREFIX_CHUNK), 0)
    ci = lax.broadcasted_iota(jnp.int32, (PREFIX_CHUNK, PREFIX_CHUNK), 1)
    tri = jnp.where(ri <= ci, 1.0, 0.0).astype(BF16)
    eq_rank, _ = _prefix_counts(eq, tri)
    sel = jnp.concatenate([
        jnp.maximum(gt[:, c * PREFIX_CHUNK:(c + 1) * PREFIX_CHUNK],
                    jnp.where(r <= room, eq[:, c * PREFIX_CHUNK:(c + 1) * PREFIX_CHUNK], 0.0))
        for c, r in enumerate(eq_rank)], axis=-1)
    sel_rank, before = _prefix_counts(sel, tri)
    for c, r in enumerate(sel_rank):
        cols = slice(c * PREFIX_CHUNK, (c + 1) * PREFIX_CHUNK)
        pos_ref[0, :, cols] = jnp.where(sel[:, cols] > 0.0, r - 1.0, -1.0).astype(jnp.int32)
    for c, cnt in enumerate(before):
        cnt_ref[0, :, c:c + 1] = cnt.astype(jnp.int32)


def _topk(aff_t, cap):
    b, ne, n = aff_t.shape
    spec = pl.BlockSpec((1, ne, n), lambda bb: (bb, 0, 0))
    n_cnt = n // PREFIX_CHUNK + 1
    return pl.pallas_call(
        functools.partial(_topk_kernel, cap=cap),
        grid=(b,),
        in_specs=[spec],
        out_specs=[spec, pl.BlockSpec((1, ne, n_cnt), lambda bb: (bb, 0, 0))],
        out_shape=[jax.ShapeDtypeStruct((b, ne, n), jnp.int32), jax.ShapeDtypeStruct((b, ne, n_cnt), jnp.int32)],
        compiler_params=_cparams(1),
        name="topk",
    )(aff_t)


def _moe_gather_kernel(cnt_ref, h_ref, pos_ref, aff_ref, xg_ref, gate_ref, *, cap, n_cnt):
    i = pl.program_id(1)
    ne, tn = pos_ref.shape[1:]
    win = min(GATHER_WINDOW, cap)
    row = (pl.program_id(0) * ne) * n_cnt + i

    @pl.when(i == 0)
    def _():
        xg_ref[...] = jnp.zeros(xg_ref.shape, xg_ref.dtype)
        gate_ref[...] = jnp.zeros(gate_ref.shape, gate_ref.dtype)

    h = h_ref[0]
    slot = lax.broadcasted_iota(jnp.int32, (win, tn), 0)

    def add_window(e, start, hit, res):
        rows = pl.ds(start, win)
        xg_ref[0, e, rows, :] = (xg_ref[0, e, rows, :].astype(F32) + res).astype(BF16)
        gate_ref[0, e, rows, :] += jnp.sum(jnp.where(hit, aff_ref[0, e:e + 1, :], 0.0), axis=-1, keepdims=True)

    starts, his, hits = [], [], []
    for e in range(ne):
        lo, hi = cnt_ref[row + e * n_cnt], cnt_ref[row + e * n_cnt + 1]
        start = pl.multiple_of(jnp.minimum((lo // BF16_ROWS) * BF16_ROWS, cap - win), BF16_ROWS)
        starts.append(start)
        his.append(hi)
        hits.append(pos_ref[0, e:e + 1, :] - start == slot)
    onehot = jnp.concatenate([jnp.where(hit, 1.0, 0.0).astype(BF16) for hit in hits], axis=0)
    res = _dot(onehot, h)
    for e in range(ne):
        add_window(e, starts[e], hits[e], res[e * win:(e + 1) * win])

    for e in range(ne):
        first = starts[e] + win
        n_more = jnp.maximum(his[e] - first + (win - 1), 0) // win

        def more(k, carry, e=e, first=first):
            lower = first + k * win
            start = pl.multiple_of(jnp.minimum(lower, cap - win), BF16_ROWS)
            p = pos_ref[0, e:e + 1, :]
            hit = jnp.logical_and(p - start == slot, p >= lower)
            add_window(e, start, hit, _dot(jnp.where(hit, 1.0, 0.0).astype(BF16), h))
            return carry

        lax.fori_loop(0, n_more, more, 0)


def _moe_gather(h, pos, aff_t, cnt, set_block, n_set, cap):
    b, _, d = h.shape
    ne = pos.shape[1]
    tn = PREFIX_CHUNK
    tiles = n_set // tn
    tile = pl.BlockSpec((1, ne, tn), lambda bb, i, c: (bb, 0, i))
    return pl.pallas_call(
        functools.partial(_moe_gather_kernel, cap=cap, n_cnt=cnt.shape[-1]),
        grid_spec=pltpu.PrefetchScalarGridSpec(
            num_scalar_prefetch=1,
            grid=(b, tiles),
            in_specs=[pl.BlockSpec((1, tn, d), lambda bb, i, c: (bb, set_block * tiles + i, 0)), tile, tile],
            out_specs=[pl.BlockSpec((1, ne, cap, d), lambda bb, i, c: (bb, 0, 0, 0)),
                       pl.BlockSpec((1, ne, cap, 1), lambda bb, i, c: (bb, 0, 0, 0))]),
        out_shape=[jax.ShapeDtypeStruct((b, ne, cap, d), BF16), jax.ShapeDtypeStruct((b, ne, cap, 1), F32)],
        compiler_params=_cparams(2),
        name="moe_gather",
    )(cnt.reshape(-1), h, pos, aff_t)


def _moe_ffn_kernel(xg_ref, gate_ref, wg32_ref, wu32_ref, wd32_ref, y_ref, wg_ref, wu_ref, wd_ref):
    @pl.when(pl.program_id(1) == 0)
    def _():
        wg_ref[...] = wg32_ref[0, 0].astype(BF16)
        wu_ref[...] = wu32_ref[0, 0].astype(BF16)
        wd_ref[...] = wd32_ref[0, 0].astype(BF16)

    bg, _, cap, _ = xg_ref.shape
    rows = lambda ref: ref[0, 0] if bg == 1 else jnp.concatenate([ref[bb, 0] for bb in range(bg)], axis=0)
    xg = rows(xg_ref)
    hid = _silu(_dot(xg, wg_ref[...])) * _dot(xg, wu_ref[...])
    y = _dot(hid.astype(BF16), wd_ref[...]) * rows(gate_ref)
    for bb in range(bg):
        y_ref[bb, 0] = y[bb * cap:(bb + 1) * cap].astype(BF16)


def _moe_ffn(xg, gates, wg, wu, wd, layer, bg):
    b, ne, cap, d = xg.shape
    ff = wg.shape[-1]
    tok = pl.BlockSpec((bg, 1, cap, d), lambda e, i: (i, e, 0, 0))
    return pl.pallas_call(
        _moe_ffn_kernel,
        grid=(ne, b // bg),
        in_specs=[tok, pl.BlockSpec((bg, 1, cap, 1), lambda e, i: (i, e, 0, 0)),
                  pl.BlockSpec((1, 1, d, ff), lambda e, i: (layer, e, 0, 0)),
                  pl.BlockSpec((1, 1, d, ff), lambda e, i: (layer, e, 0, 0)),
                  pl.BlockSpec((1, 1, ff, d), lambda e, i: (layer, e, 0, 0))],
        out_specs=tok,
        out_shape=jax.ShapeDtypeStruct((b, ne, cap, d), BF16),
        scratch_shapes=[pltpu.VMEM((d, ff), BF16), pltpu.VMEM((d, ff), BF16), pltpu.VMEM((ff, d), BF16)],
        compiler_params=_cparams(2),
        name="moe_ffn",
    )(xg, gates, wg, wu, wd)


def _moe_combine_kernel(cnt_ref, x_ref, pos_ref, y_ref, mod_ref, o_ref, *, cap, n_cnt):
    pos = pos_ref[0]
    tn, ne = pos.shape
    win = min(GATHER_WINDOW, cap)
    per = COMBINE_EXPERTS_PER_MATMUL
    gate = mod_ref[0, 0][5:6]
    row = (pl.program_id(0) * ne) * n_cnt + pl.program_id(1)
    lane = lax.broadcasted_iota(jnp.int32, (tn, per * win), 1)
    acc = jnp.zeros(x_ref.shape[1:], F32)
    overflow = False
    for g in range(ne // per):
        onehot = jnp.zeros((tn, per * win), F32)
        parts = []
        for j in range(per):
            e = g * per + j
            lo, hi = cnt_ref[row + e * n_cnt], cnt_ref[row + e * n_cnt + 1]
            start = pl.multiple_of(jnp.minimum((lo // BF16_ROWS) * BF16_ROWS, cap - win), BF16_ROWS)
            overflow = jnp.logical_or(overflow, hi > start + win)
            rel = pos[:, e:e + 1] - start
            target = jnp.where(jnp.logical_and(rel >= 0, rel < win), rel + j * win, -1)
            onehot = jnp.where(target == lane, 1.0, onehot)
            parts.append(y_ref[0, e, pl.ds(start, win), :])
        acc = acc + _dot(onehot.astype(BF16), jnp.concatenate(parts, axis=0))
    o_ref[0] = x_ref[0] + gate * acc

    @pl.when(overflow)
    def _():
        lane_all = lax.broadcasted_iota(jnp.int32, (tn, cap), 1)
        full = jnp.zeros(x_ref.shape[1:], F32)
        for e in range(ne):
            onehot = jnp.where(pos[:, e:e + 1] == lane_all, 1.0, 0.0).astype(BF16)
            full = full + _dot(onehot, y_ref[0, e])
        o_ref[0] = x_ref[0] + gate * full


def _moe_combine(x, pos_t, cnt, y, mod, tile_off, n_set, mod_sel, out_rows):
    b, t, d = x.shape
    _, ne, cap, _ = y.shape
    tn = PREFIX_CHUNK
    n_cnt = cnt.shape[-1]
    alias = out_rows == t
    out_off = tile_off if alias else 0
    return pl.pallas_call(
        functools.partial(_moe_combine_kernel, cap=cap, n_cnt=n_cnt),
        grid_spec=pltpu.PrefetchScalarGridSpec(
            num_scalar_prefetch=1,
            grid=(b, n_set // tn),
            in_specs=[pl.BlockSpec((1, tn, d), lambda bb, i, c: (bb, i + tile_off, 0)),
                      pl.BlockSpec((1, tn, ne), lambda bb, i, c: (bb, i, 0)),
                      pl.BlockSpec((1, ne, cap, d), lambda bb, i, c: (bb, 0, 0, 0)),
                      pl.BlockSpec((1, 1, 6, d), lambda bb, i, c: (bb, mod_sel, 0, 0))],
            out_specs=pl.BlockSpec((1, tn, d), lambda bb, i, c: (bb, i + out_off, 0))),
        out_shape=jax.ShapeDtypeStruct((b, out_rows, d), F32),
        input_output_aliases={1: 0} if alias else {},
        compiler_params=_cparams(2),
        name="moe_combine",
    )(cnt.reshape(-1), x, pos_t, y, mod)


def _moe(x, g, mod, router, wg, wu, wd, layer, n_lat, with_ctx, final):
    b, t, d = x.shape
    n_ctx = t - n_lat
    tm = PREFIX_CHUNK
    h, aff = _router(x, g, mod, router, n_lat, (t if with_ctx else n_lat) // tm)
    aff_t = jnp.swapaxes(aff, 1, 2)
    sets = [(0, n_lat, 0, 1)]
    if with_ctx:
        sets.append((n_lat, n_ctx, 1, b))
    for start, n_set, mod_sel, bg in sets:
        cap = CAPACITY_FACTOR * n_set // N_EXPERTS
        a_set = aff_t[:, :, start:start + n_set]
        pos, cnt = _topk(a_set, cap)
        xg, gates = _moe_gather(h, pos, a_set, cnt, start // n_set, n_set, cap)
        y = _moe_ffn(xg, gates, wg, wu, wd, layer, bg)
        x = _moe_combine(x, jnp.swapaxes(pos, 1, 2), cnt, y, mod, start // PREFIX_CHUNK, n_set, mod_sel,
                         n_lat if final else t)
    return x


def _rope_tables(n_lat, n_ctx, rot_dim, starts):
    n_rows = n_lat // GRID_W
    rows = np.repeat(np.arange(n_rows, dtype=np.float32), GRID_W)
    cols = np.tile(np.arange(GRID_W, dtype=np.float32), n_rows)
    n_freq = rot_dim // 4
    inv_freq = (np.float32(ROPE_BASE) ** (-np.arange(n_freq, dtype=np.float32) / np.float32(n_freq))).astype(np.float32)
    ang = np.concatenate([rows[:, None] * inv_freq, cols[:, None] * inv_freq], axis=-1)
    half = rot_dim // 2
    cos_l, sin_l = np.cos(ang).astype(np.float32), np.sin(ang).astype(np.float32)
    cos = np.ones((n_lat + n_ctx, LANE), np.float32)
    sa = np.zeros((n_lat + n_ctx, LANE), np.float32)
    sb = np.zeros((n_lat + n_ctx, LANE), np.float32)
    for s in starts:
        cos[:n_lat, s:s + half] = cos_l
        cos[:n_lat, s + half:s + rot_dim] = cos_l
        sa[:n_lat, s:s + half] = -sin_l
        sb[:n_lat, s + half:s + rot_dim] = sin_l
    return jnp.asarray(cos), jnp.asarray(sa), jnp.asarray(sb)


def _pad_heads(w, heads, width):
    k = w.shape[0]
    w = w.reshape(k, heads, width)
    return jnp.pad(w, ((0, 0), (0, 0), (0, LANE - width))).reshape(k, heads * LANE)


def _mla_partner(a):
    half = MLA_ROPE // 2
    lane = jnp.arange(LANE)
    first = (lane >= MLA_NOPE) & (lane < MLA_NOPE + half)
    second = (lane >= MLA_NOPE + half) & (lane < MLA_QK)
    return jnp.where(first, jnp.roll(a, -half, axis=-1), jnp.where(second, jnp.roll(a, half, axis=-1), 0.0))


def _mla_weights(w_in, q_norm_g, w_uq, kv_norm_g, w_ukv):
    d = w_in.shape[0]
    lat = MLA_Q_LORA + MLA_KV_LORA
    rope_block = jnp.concatenate([jnp.zeros((d, MLA_NOPE), F32), w_in[:, lat:],
                                  jnp.zeros((d, LANE - MLA_QK), F32)], axis=1)
    ukv = w_ukv.reshape(MLA_KV_LORA, MLA_HEADS, MLA_NOPE + MLA_V)
    wuq = _pad_heads(w_uq, MLA_HEADS, MLA_QK)
    wuqp = _mla_partner(wuq.reshape(-1, MLA_HEADS, LANE)).reshape(wuq.shape)
    return {
        "win": jnp.concatenate([w_in[:, :lat], rope_block, _mla_partner(rope_block)], axis=1).astype(BF16),
        "qng": q_norm_g.reshape(1, -1),
        "wuq": wuq.astype(BF16),
        "wuqp": wuqp.astype(BF16),
        "kvng": kv_norm_g.reshape(1, -1),
        "wuk": _pad_heads(ukv[:, :, :MLA_NOPE].reshape(MLA_KV_LORA, -1), MLA_HEADS, MLA_NOPE).astype(BF16),
        "wuv": _pad_heads(ukv[:, :, MLA_NOPE:].reshape(MLA_KV_LORA, -1), MLA_HEADS, MLA_V).astype(BF16),
    }


def _diff_partner(a):
    half = DIFF_HEAD_DIM // 2
    shape = a.shape
    a = a.reshape(shape[:-1] + (shape[-1] // LANE, LANE))
    first = (jnp.arange(LANE) % DIFF_HEAD_DIM) < half
    return jnp.where(first, jnp.roll(a, -half, axis=-1), jnp.roll(a, half, axis=-1)).reshape(shape)


def _diff_weights(w_in):
    d = w_in.shape[0]
    return jnp.concatenate([w_in, _diff_partner(w_in[:, :2 * d])], axis=1).astype(BF16)


def _diff_tables(tabs, qn_g, kn_g):
    cos, sa, sb = tabs
    out = []
    for g, scale in ((qn_g, DIFF_HEAD_DIM ** -0.5 * LOG2E), (kn_g, 1.0)):
        g2 = jnp.tile(g, 2)
        out += [cos * g2 * scale, (sa + sb) * _diff_partner(g2) * scale]
    return out


def _mla_tables(tabs, qn_g, kn_g):
    cos, sa, sb = tabs
    out = []
    for g, scale in ((qn_g, MLA_QK ** -0.5 * LOG2E), (kn_g, 1.0)):
        gp = jnp.pad(g, (0, LANE - MLA_QK))
        out += [cos * gp * scale, (sa + sb) * _mla_partner(gp) * scale]
    return out


def kernel(x, c, ctx, c_ctx, ada_w, ada_b, norm_mix_g, norm_ffn_g, mla_w_in, mla_q_norm_g, mla_w_uq, mla_kv_norm_g, mla_w_ukv, mla_qn_g, mla_kn_g, mla_w_out, diff_w_in, diff_qn_g, diff_kn_g, diff_lambda_q1, diff_lambda_k1, diff_lambda_q2, diff_lambda_k2, diff_sub_g, diff_w_out, sg_w_in, sg_ln_g, sg_ln_b, sg_w_s, sg_b_s, sg_w_out, moe_router, moe_w_gate, moe_w_up, moe_w_down):
    b, n_lat, d = x.shape
    n_ctx = ctx.shape[1]
    depth = ada_w.shape[0]
    assert n_lat % ROW_TILE == 0 and n_ctx == ROW_TILE and n_lat % n_ctx == 0

    rows = -(-(b + 1) // 8) * 8
    cc = jnp.concatenate([c, c_ctx[None], jnp.zeros((rows - b - 1, d), F32)], axis=0)
    mods = _ada(cc, ada_w, ada_b).reshape(depth, rows, 6, d)
    xs = jnp.concatenate([x, ctx], axis=1)

    tabs_a = _rope_tables(n_lat, n_ctx, MLA_ROPE, (MLA_NOPE,))
    tabs_b = _rope_tables(n_lat, n_ctx, DIFF_HEAD_DIM, (0, DIFF_HEAD_DIM))

    for i in range(depth):
        kind, j = i % N_MIXERS, i // N_MIXERS
        last = i == depth - 1
        with_ctx = not last
        q_tiles = (n_lat + (n_ctx if with_ctx else 0)) // ROW_TILE
        mod = jnp.stack([mods[i, :b], jnp.broadcast_to(mods[i, b], (b, 6, d))], axis=1)
        g_mix = norm_mix_g[i].reshape(1, d)
        if kind == 0:
            w = _mla_weights(mla_w_in[j], mla_q_norm_g[j], mla_w_uq[j], mla_kv_norm_g[j], mla_w_ukv[j])
            q, k, v = _mla_in(xs, g_mix, mod, w, _mla_tables(tabs_a, mla_qn_g[j], mla_kn_g[j]), n_lat)
            a = _attn_calls(_mla_attn_kernel, "mla_attn", [q], [k, v], [], n_lat, with_ctx,
                            MLA_HEADS, MLA_HEADS_PER_STEP, MLA_HEADS_PER_STEP * MLA_V)
            xs = _out_proj(xs, a, mla_w_out[j].astype(BF16), mod, n_lat, q_tiles)
        elif kind == 1:
            lam_init = 0.8 - 0.6 * math.exp(-0.3 * i)
            q0, q1, k, v = _diff_in(xs, g_mix, mod, _diff_weights(diff_w_in[j]),
                                    _diff_tables(tabs_b, diff_qn_g[j], diff_kn_g[j]), n_lat)
            wa = [a.reshape(1, -1) for a in (diff_lambda_q1[j], diff_lambda_k1[j], diff_lambda_q2[j],
                                             diff_lambda_k2[j], diff_sub_g[j])]
            a = _attn_calls(functools.partial(_diff_attn_kernel, lam_init=lam_init), "diff_attn", [q0, q1], [k, v],
                            wa, n_lat, with_ctx, DIFF_HEADS, DIFF_HEADS_PER_STEP, DIFF_HEADS_PER_STEP * LANE)
            xs = _out_proj(xs, a, diff_w_out[j].astype(BF16), mod, n_lat, q_tiles)
        else:
            w = {"win": sg_w_in[j].astype(BF16), "lng": sg_ln_g[j].reshape(1, -1), "lnb": sg_ln_b[j].reshape(1, -1),
                 "ws": sg_w_s[j].astype(BF16), "bs": sg_b_s[j].T, "wout": sg_w_out[j].astype(BF16)}
            xs = _sg(xs, g_mix, mod, w, n_lat)
        xs = _moe(xs, norm_ffn_g[i].reshape(1, d), mod, moe_router[i], moe_w_gate, moe_w_up, moe_w_down, i,
                  n_lat, with_ctx, last)
    return xs
```

```python
import functools
import math

import jax
import jax.numpy as jnp
import numpy as np
from jax import lax
from jax.experimental import pallas as pl
from jax.experimental.pallas import tpu as pltpu

F32 = jnp.float32
BF16 = jnp.bfloat16

GRID_W = 64
ROPE_BASE = 10000.0
EPS = 1e-6
N_MIXERS = 3

MLA_HEADS = 16
MLA_Q_LORA = 512
MLA_KV_LORA = 256
MLA_NOPE = 64
MLA_ROPE = 32
MLA_V = 64
MLA_QK = MLA_NOPE + MLA_ROPE

DIFF_HEADS = 8
DIFF_HEAD_DIM = 64
DIFF_V_DIM = 2 * DIFF_HEAD_DIM

SG_CHUNK = 128
SG_GROUPS = 8

N_EXPERTS = 16
CAPACITY_FACTOR = 2

LANE = 128
BF16_ROWS = 16
ROW_TILE = 256
PREFIX_CHUNK = 256
GATHER_WINDOW = 64
COMBINE_EXPERTS_PER_MATMUL = 4
MLA_HEADS_PER_STEP = 4
DIFF_HEADS_PER_STEP = 4
ATTN_Q_TILE = 256
LOG2E = math.log2(math.e)
VMEM_LIMIT = 56 * 1024 * 1024


def _cparams(n_axes):
    return pltpu.CompilerParams(dimension_semantics=("arbitrary",) * n_axes, vmem_limit_bytes=VMEM_LIMIT)


def _const_spec(shape):
    nd = len(shape)
    return pl.BlockSpec(shape, lambda *_: (0,) * nd)


def _dot(a, b):
    return jnp.dot(a, b, preferred_element_type=F32)


def _dot_nt(a, b):
    return lax.dot_general(a, b, (((1,), (1,)), ((), ())), preferred_element_type=F32)


def _rms(x, n):
    return x * lax.rsqrt(jnp.sum(x * x, axis=-1, keepdims=True) * (1.0 / n) + EPS)


def _norm_mod(x, g, shift, scale):
    return _rms(x, x.shape[-1]) * g * (1.0 + scale) + shift


def _silu(x):
    return x / (1.0 + jnp.exp(-x))


def _gelu_tanh(x):
    c = math.sqrt(2.0 / math.pi)
    return 0.5 * x * (1.0 + jnp.tanh(c * (x + 0.044715 * (x * x * x))))


def _ada_kernel(c_ref, w_ref, b_ref, o_ref):
    s = _silu(c_ref[...])
    o_ref[0] = jnp.dot(s, w_ref[0], preferred_element_type=F32, precision=lax.Precision.HIGHEST) + b_ref[0]


def _ada(cc, ada_w, ada_b):
    depth, d, six_d = ada_w.shape
    rows = cc.shape[0]
    tn = 1536
    return pl.pallas_call(
        _ada_kernel,
        grid=(depth, six_d // tn),
        in_specs=[
            _const_spec((rows, d)),
            pl.BlockSpec((1, d, tn), lambda i, j: (i, 0, j)),
            pl.BlockSpec((1, 1, tn), lambda i, j: (i, 0, j)),
        ],
        out_specs=pl.BlockSpec((1, rows, tn), lambda i, j: (i, 0, j)),
        out_shape=jax.ShapeDtypeStruct((depth, rows, six_d), F32),
        compiler_params=_cparams(2),
        name="ada",
    )(cc, ada_w, ada_b.reshape(depth, 1, six_d))


def _mod_spec(n_lat_tiles, d):
    return pl.BlockSpec((1, 1, 6, d), lambda b, i: (b, jnp.minimum(i // n_lat_tiles, 1), 0, 0))


def _mla_in_kernel(x_ref, g_ref, mod_ref, win_ref, qng_ref, wuq_ref, wuqp_ref, kvng_ref, wuk_ref, wuv_ref,
                   aq_ref, bq_ref, ak_ref, bk_ref, q_ref, k_ref, v_ref):
    mod = mod_ref[0, 0]
    h = _norm_mod(x_ref[0], g_ref[...], mod[0:1], mod[1:2])
    down = _dot(h.astype(BF16), win_ref[...])
    cq = (_rms(down[:, :MLA_Q_LORA], MLA_Q_LORA) * qng_ref[...]).astype(BF16)
    ckv = (_rms(down[:, MLA_Q_LORA:MLA_Q_LORA + MLA_KV_LORA], MLA_KV_LORA) * kvng_ref[...]).astype(BF16)
    q = _dot(cq, wuq_ref[...])
    qp = _dot(cq, wuqp_ref[...])
    kn = _dot(ckv, wuk_ref[...])
    v = _dot(ckv, wuv_ref[...])
    is_v = (lax.broadcasted_iota(jnp.int32, v.shape, 1) & (LANE - 1)) < MLA_V
    v_ref[0] = jnp.where(is_v, v, 1.0).astype(BF16)
    lat = MLA_Q_LORA + MLA_KV_LORA
    kr, krp = down[:, lat:lat + LANE], down[:, lat + LANE:]
    aq, bq, ak, bk = aq_ref[...], bq_ref[...], ak_ref[...], bk_ref[...]
    krot = krp * bk
    inv_n = 1.0 / MLA_QK
    for hd in range(MLA_HEADS):
        sl = slice(hd * LANE, (hd + 1) * LANE)
        qh = q[:, sl]
        rq = lax.rsqrt(jnp.sum(qh * qh, axis=-1, keepdims=True) * inv_n + EPS)
        q_ref[0, :, sl] = ((qh * aq + qp[:, sl] * bq) * rq).astype(BF16)
        kh = kn[:, sl] + kr
        rk = lax.rsqrt(jnp.sum(kh * kh, axis=-1, keepdims=True) * inv_n + EPS)
        k_ref[0, :, sl] = ((kh * ak + krot) * rk).astype(BF16)


def _mla_in(x, g, mod, w, tabs, n_lat):
    b, t, d = x.shape
    tm = ROW_TILE
    hw = MLA_HEADS * LANE
    row = lambda bb, i: (bb, i, 0)
    tab = pl.BlockSpec((tm, LANE), lambda bb, i: (i, 0))
    consts = [w["win"], w["qng"], w["wuq"], w["wuqp"], w["kvng"], w["wuk"], w["wuv"]]
    return pl.pallas_call(
        _mla_in_kernel,
        grid=(b, t // tm),
        in_specs=[pl.BlockSpec((1, tm, d), row), _const_spec((1, d)), _mod_spec(n_lat // tm, d)]
        + [_const_spec(a.shape) for a in consts] + [tab] * 4,
        out_specs=[pl.BlockSpec((1, tm, hw), row)] * 3,
        out_shape=[jax.ShapeDtypeStruct((b, t, hw), BF16)] * 3,
        compiler_params=_cparams(2),
        name="mla_in",
    )(x, g, mod, *consts, *tabs)


def _diff_in_kernel(x_ref, g_ref, mod_ref, win_ref, aq_ref, bq_ref, ak_ref, bk_ref, q0_ref, q1_ref, k_ref, v_ref):
    mod = mod_ref[0, 0]
    d = x_ref.shape[-1]
    h = _norm_mod(x_ref[0], g_ref[...], mod[0:1], mod[1:2])
    qkv = _dot(h.astype(BF16), win_ref[...])
    ones = jnp.ones((x_ref.shape[1], LANE), BF16)
    for hd in range(DIFF_HEADS):
        v_ref[0, :, 2 * hd * LANE:(2 * hd + 1) * LANE] = qkv[:, 2 * d + hd * LANE:2 * d + (hd + 1) * LANE].astype(BF16)
        v_ref[0, :, (2 * hd + 1) * LANE:(2 * hd + 2) * LANE] = ones
    lo = lax.broadcasted_iota(jnp.int32, (x_ref.shape[1], LANE), 1) < DIFF_HEAD_DIM
    inv_n = 1.0 / DIFF_HEAD_DIM

    def normed_rope(off, hd, a, b):
        xh = qkv[:, off + hd * LANE:off + (hd + 1) * LANE]
        xp = qkv[:, off + 3 * d + hd * LANE:off + 3 * d + (hd + 1) * LANE]
        sq = xh * xh
        s_lo = jnp.sum(jnp.where(lo, sq, 0.0), axis=-1, keepdims=True)
        s_hi = jnp.sum(jnp.where(lo, 0.0, sq), axis=-1, keepdims=True)
        return (xh * a + xp * b) * lax.rsqrt(jnp.where(lo, s_lo, s_hi) * inv_n + EPS)

    aq, bq, ak, bk = aq_ref[...], bq_ref[...], ak_ref[...], bk_ref[...]
    for hd in range(DIFF_HEADS):
        sl = slice(hd * LANE, (hd + 1) * LANE)
        qh = normed_rope(0, hd, aq, bq)
        q0_ref[0, :, sl] = jnp.where(lo, qh, 0.0).astype(BF16)
        q1_ref[0, :, sl] = jnp.where(lo, 0.0, qh).astype(BF16)
        k_ref[0, :, sl] = normed_rope(d, hd, ak, bk).astype(BF16)


def _diff_in(x, g, mod, win, tabs, n_lat):
    b, t, d = x.shape
    tm = ROW_TILE
    row = lambda bb, i: (bb, i, 0)
    tab = pl.BlockSpec((tm, LANE), lambda bb, i: (i, 0))
    out = jax.ShapeDtypeStruct((b, t, d), BF16)
    return pl.pallas_call(
        _diff_in_kernel,
        grid=(b, t // tm),
        in_specs=[pl.BlockSpec((1, tm, d), row), _const_spec((1, d)), _mod_spec(n_lat // tm, d),
                  _const_spec(win.shape)] + [tab] * 4,
        out_specs=[pl.BlockSpec((1, tm, d), row)] * 3 + [pl.BlockSpec((1, tm, 2 * d), row)],
        out_shape=[out] * 3 + [jax.ShapeDtypeStruct((b, t, 2 * d), BF16)],
        compiler_params=_cparams(2),
        name="diff_in",
    )(x, g, mod, win, *tabs)


def _exp2_scores(q, k):
    s = _dot_nt(q, k)
    return jnp.exp2(s - jnp.max(s, axis=-1, keepdims=True)).astype(BF16)


def _mla_attn_kernel(q_ref, k_ref, v_ref, *rest):
    o_ref = rest[-1]
    tq = q_ref.shape[1]
    low = lax.broadcasted_iota(jnp.int32, (tq, LANE), 1) < MLA_V
    blocks = [slice(hd * LANE, (hd + 1) * LANE) for hd in range(MLA_HEADS_PER_STEP)]
    scores = [_dot_nt(q_ref[0, :, sl], k_ref[0, :, sl]) for sl in blocks]
    weights = [jnp.exp2(s - jnp.max(s, axis=-1, keepdims=True)).astype(BF16) for s in scores]
    for pr in range(MLA_HEADS_PER_STEP // 2):
        halves = []
        for hd in (2 * pr, 2 * pr + 1):
            r = _dot(weights[hd], v_ref[0, :, blocks[hd]])
            halves.append(r / pltpu.roll(r, LANE - MLA_V, 1))
        pair = jnp.where(low, halves[0], pltpu.roll(halves[1], MLA_V, 1))
        o_ref[0, :, pr * LANE:(pr + 1) * LANE] = pair.astype(o_ref.dtype)


def _attn_calls(kernel, name, qs, kvs, consts, n_lat, with_ctx, heads, hs, out_width):
    b, t, _ = kvs[0].shape
    n_ctx = t - n_lat
    ctx_blk = n_lat // n_ctx
    out_shape = jax.ShapeDtypeStruct((b, t, heads // hs * out_width), BF16)
    cspecs = [_const_spec(a.shape) for a in consts]

    def call(tq, q_map, kv_rows, kv_map, n_tiles, prev):
        q_specs = [pl.BlockSpec((1, tq, a.shape[-1] // (heads // hs)), q_map) for a in qs]
        kv_specs = [pl.BlockSpec((1, kv_rows, a.shape[-1] // (heads // hs)), kv_map) for a in kvs]
        extra = [] if prev is None else [pl.BlockSpec(memory_space=pl.ANY)]
        n_in = len(qs) + len(kvs) + len(consts)
        return pl.pallas_call(
            kernel,
            grid=(b, heads // hs, n_tiles),
            in_specs=q_specs + kv_specs + cspecs + extra,
            out_specs=pl.BlockSpec((1, tq, out_width), q_map),
            out_shape=out_shape,
            input_output_aliases={} if prev is None else {n_in: 0},
            compiler_params=_cparams(3),
            name=name,
        )(*qs, *kvs, *consts, *([] if prev is None else [prev]))

    tq = min(ATTN_Q_TILE, n_lat)
    out = call(tq, lambda bb, hg, i: (bb, i, hg), t, lambda bb, hg, i: (bb, 0, hg), n_lat // tq, None)
    if with_ctx:
        ctx_map = lambda bb, hg, i: (bb, ctx_blk, hg)
        out = call(n_ctx, ctx_map, n_ctx, ctx_map, 1, out)
    return out


def _diff_attn_kernel(q0_ref, q1_ref, k_ref, v_ref, lq1_ref, lk1_ref, lq2_ref, lk2_ref, sub_ref, *rest, lam_init):
    o_ref = rest[-1]
    lam = (jnp.exp(jnp.sum(lq1_ref[...] * lk1_ref[...], axis=-1, keepdims=True))
           - jnp.exp(jnp.sum(lq2_ref[...] * lk2_ref[...], axis=-1, keepdims=True)) + lam_init)
    for a in range(DIFF_HEADS_PER_STEP):
        sl = slice(a * LANE, (a + 1) * LANE)
        k, v = k_ref[0, :, sl], v_ref[0, :, 2 * a * LANE:(2 * a + 2) * LANE]
        scores = [_dot_nt(q_ref[0, :, sl], k) for q_ref in (q0_ref, q1_ref)]
        outs = []
        for s in scores:
            r = _dot(jnp.exp2(s - jnp.max(s, axis=-1, keepdims=True)).astype(BF16), v)
            outs.append(r[:, :LANE] / r[:, LANE:])
        o = outs[0] - lam * outs[1]
        o_ref[0, :, sl] = (_rms(o, DIFF_V_DIM) * sub_ref[...] * (1.0 - lam_init)).astype(o_ref.dtype)


def _route(x, mod, gf_ref, rhi_ref, rlo_ref, h_ref, aff_ref):
    h = _norm_mod(x, gf_ref[...], mod[3:4], mod[4:5])
    h_hi = h.astype(BF16)
    h_lo = (h - h_hi.astype(F32)).astype(BF16)
    h_ref[0] = h_hi
    logits = _dot(h_hi, rhi_ref[...]) + (_dot(h_lo, rhi_ref[...]) + _dot(h_hi, rlo_ref[...]))
    e = jnp.exp(logits - jnp.max(logits, axis=-1, keepdims=True))
    aff_ref[0] = e / jnp.sum(e, axis=-1, keepdims=True)


def _route_operands(x, norm_ffn_g, router, n_tiles, tm):
    b, _, d = x.shape
    ne = router.shape[-1]
    row = lambda bb, i: (bb, i, 0)
    r_hi = router.astype(BF16)
    r_lo = (router - r_hi.astype(F32)).astype(BF16)
    ins = [norm_ffn_g, r_hi, r_lo]
    in_specs = [_const_spec(a.shape) for a in ins]
    out_specs = [pl.BlockSpec((1, tm, d), row), pl.BlockSpec((1, tm, ne), row)]
    out_shape = [jax.ShapeDtypeStruct((b, n_tiles * tm, d), BF16), jax.ShapeDtypeStruct((b, n_tiles * tm, ne), F32)]
    return ins, in_specs, out_specs, out_shape


def _out_proj_kernel(x_ref, a_ref, w_ref, mod_ref, gf_ref, rhi_ref, rlo_ref, o_ref, h_ref, aff_ref):
    mod = mod_ref[0, 0]
    x = x_ref[0] + mod[2:3] * _dot(a_ref[0], w_ref[...])
    o_ref[0] = x
    _route(x, mod, gf_ref, rhi_ref, rlo_ref, h_ref, aff_ref)


def _out_proj(x, a, w, mod, norm_ffn_g, router, n_lat, n_tiles):
    b, t, d = x.shape
    tm = ROW_TILE
    row = lambda bb, i: (bb, i, 0)
    r_ins, r_in_specs, r_out_specs, r_out_shape = _route_operands(x, norm_ffn_g, router, n_tiles, tm)
    return pl.pallas_call(
        _out_proj_kernel,
        grid=(b, n_tiles),
        in_specs=[pl.BlockSpec((1, tm, d), row), pl.BlockSpec((1, tm, a.shape[-1]), row), _const_spec(w.shape),
                  _mod_spec(n_lat // tm, d)] + r_in_specs,
        out_specs=[pl.BlockSpec((1, tm, d), row)] + r_out_specs,
        out_shape=[jax.ShapeDtypeStruct(x.shape, F32)] + r_out_shape,
        input_output_aliases={0: 0},
        compiler_params=_cparams(2),
        name="out_proj",
    )(x, a, w, mod, *r_ins)


def _sg_kernel(x_ref, g_ref, mod_ref, win_ref, lng_ref, lnb_ref, ws_ref, bs_ref, wout_ref, gf_ref, rhi_ref, rlo_ref,
               o_ref, h_ref, aff_ref, gated_ref):
    mod = mod_ref[0, 0]
    x = x_ref[0]
    tm = x.shape[0]
    width = lng_ref.shape[-1]
    gdim = width // SG_GROUPS
    h = _norm_mod(x, g_ref[...], mod[0:1], mod[1:2])
    z = _gelu_tanh(_dot(h.astype(BF16), win_ref[...]))
    u, v = z[:, :width], z[:, width:]
    mu = jnp.mean(v, axis=-1, keepdims=True)
    vc = v - mu
    var = jnp.mean(vc * vc, axis=-1, keepdims=True)
    vn = (vc * lax.rsqrt(var + EPS) * lng_ref[...] + lnb_ref[...]).astype(BF16)
    for c in range(tm // SG_CHUNK):
        rows = slice(c * SG_CHUNK, (c + 1) * SG_CHUNK)
        for gi in range(SG_GROUPS):
            cols = slice(gi * gdim, (gi + 1) * gdim)
            mixed = _dot(ws_ref[gi], vn[rows, cols]) + bs_ref[:, gi:gi + 1]
            gated_ref[rows, cols] = (u[rows, cols] * mixed).astype(BF16)
    x = x + mod[2:3] * _dot(gated_ref[...], wout_ref[...])
    o_ref[0] = x
    _route(x, mod, gf_ref, rhi_ref, rlo_ref, h_ref, aff_ref)


def _sg(x, g, mod, w, norm_ffn_g, router, n_lat):
    b, t, d = x.shape
    tm = ROW_TILE
    row = lambda bb, i: (bb, i, 0)
    consts = [w["win"], w["lng"], w["lnb"], w["ws"], w["bs"], w["wout"]]
    r_ins, r_in_specs, r_out_specs, r_out_shape = _route_operands(x, norm_ffn_g, router, t // tm, tm)
    return pl.pallas_call(
        _sg_kernel,
        grid=(b, t // tm),
        in_specs=[pl.BlockSpec((1, tm, d), row), _const_spec((1, d)), _mod_spec(n_lat // tm, d)]
        + [_const_spec(a.shape) for a in consts] + r_in_specs,
        out_specs=[pl.BlockSpec((1, tm, d), row)] + r_out_specs,
        out_shape=[jax.ShapeDtypeStruct(x.shape, F32)] + r_out_shape,
        scratch_shapes=[pltpu.VMEM((tm, w["lng"].shape[-1]), BF16)],
        input_output_aliases={0: 0},
        compiler_params=_cparams(2),
        name="chunk_mlp",
    )(x, g, mod, *consts, *r_ins)


def _prefix_counts(mask, tri):
    out = []
    carry = jnp.zeros((mask.shape[0], 1), F32)
    before = [carry]
    for c in range(mask.shape[1] // PREFIX_CHUNK):
        m = mask[:, c * PREFIX_CHUNK:(c + 1) * PREFIX_CHUNK]
        out.append(_dot(m.astype(BF16), tri) + carry)
        carry = carry + jnp.sum(m, axis=-1, keepdims=True)
        before.append(carry)
    return out, before


def _topk_kernel(aff_ref, pos_ref, cnt_ref, *, cap):
    a = aff_ref[0]
    ne = a.shape[0]
    bits = pltpu.bitcast(a, jnp.int32)

    def step(i, lo):
        cand = lo | jnp.left_shift(jnp.int32(1), 30 - i)
        cnt = jnp.sum(jnp.where(bits >= cand, 1.0, 0.0), axis=-1, keepdims=True)
        return jnp.where(cnt >= cap, cand, lo)

    thr = lax.fori_loop(0, 31, step, jnp.zeros((ne, 1), jnp.int32))
    gt = jnp.where(bits > thr, 1.0, 0.0)
    eq = jnp.where(bits == thr, 1.0, 0.0)
    room = cap - jnp.sum(gt, axis=-1, keepdims=True)
    ri = lax.broadcasted_iota(jnp.int32, (PREFIX_CHUNK, PREFIX_CHUNK), 0)
    ci = lax.broadcasted_iota(jnp.int32, (PREFIX_CHUNK, PREFIX_CHUNK), 1)
    tri = jnp.where(ri <= ci, 1.0, 0.0).astype(BF16)
    eq_rank, _ = _prefix_counts(eq, tri)
    sel = jnp.concatenate([
        jnp.maximum(gt[:, c * PREFIX_CHUNK:(c + 1) * PREFIX_CHUNK],
                    jnp.where(r <= room, eq[:, c * PREFIX_CHUNK:(c + 1) * PREFIX_CHUNK], 0.0))
        for c, r in enumerate(eq_rank)], axis=-1)
    sel_rank, before = _prefix_counts(sel, tri)
    for c, r in enumerate(sel_rank):
        cols = slice(c * PREFIX_CHUNK, (c + 1) * PREFIX_CHUNK)
        pos_ref[0, :, cols] = jnp.where(sel[:, cols] > 0.0, r - 1.0, -1.0).astype(jnp.int32)
    for c, cnt in enumerate(before):
        cnt_ref[0, :, c:c + 1] = cnt.astype(jnp.int32)


def _topk(aff_t, cap):
    b, ne, n = aff_t.shape
    spec = pl.BlockSpec((1, ne, n), lambda bb: (bb, 0, 0))
    n_cnt = n // PREFIX_CHUNK + 1
    return pl.pallas_call(
        functools.partial(_topk_kernel, cap=cap),
        grid=(b,),
        in_specs=[spec],
        out_specs=[spec, pl.BlockSpec((1, ne, n_cnt), lambda bb: (bb, 0, 0))],
        out_shape=[jax.ShapeDtypeStruct((b, ne, n), jnp.int32), jax.ShapeDtypeStruct((b, ne, n_cnt), jnp.int32)],
        compiler_params=_cparams(1),
        name="topk",
    )(aff_t)


def _moe_gather_kernel(cnt_ref, h_ref, pos_ref, aff_ref, xg_ref, gate_ref, *, cap, n_cnt):
    i = pl.program_id(1)
    ne, tn = pos_ref.shape[1:]
    win = min(GATHER_WINDOW, cap)
    row = (pl.program_id(0) * ne) * n_cnt + i

    @pl.when(i == 0)
    def _():
        xg_ref[...] = jnp.zeros(xg_ref.shape, xg_ref.dtype)
        gate_ref[...] = jnp.zeros(gate_ref.shape, gate_ref.dtype)

    h = h_ref[0]
    slot = lax.broadcasted_iota(jnp.int32, (win, tn), 0)

    def add_window(e, start, hit, res):
        rows = pl.ds(start, win)
        xg_ref[0, e, rows, :] = (xg_ref[0, e, rows, :].astype(F32) + res).astype(BF16)
        gate_ref[0, e, rows, :] += jnp.sum(jnp.where(hit, aff_ref[0, e:e + 1, :], 0.0), axis=-1, keepdims=True)

    starts, his, hits = [], [], []
    for e in range(ne):
        lo, hi = cnt_ref[row + e * n_cnt], cnt_ref[row + e * n_cnt + 1]
        start = pl.multiple_of(jnp.minimum((lo // BF16_ROWS) * BF16_ROWS, cap - win), BF16_ROWS)
        starts.append(start)
        his.append(hi)
        hits.append(pos_ref[0, e:e + 1, :] - start == slot)
    onehot = jnp.concatenate([jnp.where(hit, 1.0, 0.0).astype(BF16) for hit in hits], axis=0)
    res = _dot(onehot, h)
    for e in range(ne):
        add_window(e, starts[e], hits[e], res[e * win:(e + 1) * win])

    for e in range(ne):
        first = starts[e] + win
        n_more = jnp.maximum(his[e] - first + (win - 1), 0) // win

        def more(k, carry, e=e, first=first):
            lower = first + k * win
            start = pl.multiple_of(jnp.minimum(lower, cap - win), BF16_ROWS)
            p = pos_ref[0, e:e + 1, :]
            hit = jnp.logical_and(p - start == slot, p >= lower)
            add_window(e, start, hit, _dot(jnp.where(hit, 1.0, 0.0).astype(BF16), h))
            return carry

        lax.fori_loop(0, n_more, more, 0)


def _moe_gather(h, pos, aff_t, cnt, set_block, n_set, cap):
    b, _, d = h.shape
    ne = pos.shape[1]
    tn = PREFIX_CHUNK
    tiles = n_set // tn
    tile = pl.BlockSpec((1, ne, tn), lambda bb, i, c: (bb, 0, i))
    return pl.pallas_call(
        functools.partial(_moe_gather_kernel, cap=cap, n_cnt=cnt.shape[-1]),
        grid_spec=pltpu.PrefetchScalarGridSpec(
            num_scalar_prefetch=1,
            grid=(b, tiles),
            in_specs=[pl.BlockSpec((1, tn, d), lambda bb, i, c: (bb, set_block * tiles + i, 0)), tile, tile],
            out_specs=[pl.BlockSpec((1, ne, cap, d), lambda bb, i, c: (bb, 0, 0, 0)),
                       pl.BlockSpec((1, ne, cap, 1), lambda bb, i, c: (bb, 0, 0, 0))]),
        out_shape=[jax.ShapeDtypeStruct((b, ne, cap, d), BF16), jax.ShapeDtypeStruct((b, ne, cap, 1), F32)],
        compiler_params=_cparams(2),
        name="moe_gather",
    )(cnt.reshape(-1), h, pos, aff_t)


def _moe_ffn_kernel(xg_ref, gate_ref, wg32_ref, wu32_ref, wd32_ref, y_ref, wg_ref, wu_ref, wd_ref):
    @pl.when(pl.program_id(1) == 0)
    def _():
        wg_ref[...] = wg32_ref[0, 0].astype(BF16)
        wu_ref[...] = wu32_ref[0, 0].astype(BF16)
        wd_ref[...] = wd32_ref[0, 0].astype(BF16)

    bg, _, cap, _ = xg_ref.shape
    rows = lambda ref: ref[0, 0] if bg == 1 else jnp.concatenate([ref[bb, 0] for bb in range(bg)], axis=0)
    xg = rows(xg_ref)
    hid = _silu(_dot(xg, wg_ref[...])) * _dot(xg, wu_ref[...])
    y = _dot(hid.astype(BF16), wd_ref[...]) * rows(gate_ref)
    for bb in range(bg):
        y_ref[bb, 0] = y[bb * cap:(bb + 1) * cap].astype(BF16)


def _moe_ffn(xg, gates, wg, wu, wd, layer, bg):
    b, ne, cap, d = xg.shape
    ff = wg.shape[-1]
    tok = pl.BlockSpec((bg, 1, cap, d), lambda e, i: (i, e, 0, 0))
    return pl.pallas_call(
        _moe_ffn_kernel,
        grid=(ne, b // bg),
        in_specs=[tok, pl.BlockSpec((bg, 1, cap, 1), lambda e, i: (i, e, 0, 0)),
                  pl.BlockSpec((1, 1, d, ff), lambda e, i: (layer, e, 0, 0)),
                  pl.BlockSpec((1, 1, d, ff), lambda e, i: (layer, e, 0, 0)),
                  pl.BlockSpec((1, 1, ff, d), lambda e, i: (layer, e, 0, 0))],
        out_specs=tok,
        out_shape=jax.ShapeDtypeStruct((b, ne, cap, d), BF16),
        scratch_shapes=[pltpu.VMEM((d, ff), BF16), pltpu.VMEM((d, ff), BF16), pltpu.VMEM((ff, d), BF16)],
        compiler_params=_cparams(2),
        name="moe_ffn",
    )(xg, gates, wg, wu, wd)


def _moe_combine_kernel(cnt_ref, x_ref, pos_ref, y_ref, mod_ref, o_ref, *, cap, n_cnt):
    pos = pos_ref[0]
    tn, ne = pos.shape
    win = min(GATHER_WINDOW, cap)
    per = COMBINE_EXPERTS_PER_MATMUL
    gate = mod_ref[0, 0][5:6]
    row = (pl.program_id(0) * ne) * n_cnt + pl.program_id(1)
    lane = lax.broadcasted_iota(jnp.int32, (tn, per * win), 1)
    acc = jnp.zeros(x_ref.shape[1:], F32)
    overflow = False
    for g in range(ne // per):
        onehot = jnp.zeros((tn, per * win), F32)
        parts = []
        for j in range(per):
            e = g * per + j
            lo, hi = cnt_ref[row + e * n_cnt], cnt_ref[row + e * n_cnt + 1]
            start = pl.multiple_of(jnp.minimum((lo // BF16_ROWS) * BF16_ROWS, cap - win), BF16_ROWS)
            overflow = jnp.logical_or(overflow, hi > start + win)
            rel = pos[:, e:e + 1] - start
            target = jnp.where(jnp.logical_and(rel >= 0, rel < win), rel + j * win, -1)
            onehot = jnp.where(target == lane, 1.0, onehot)
            parts.append(y_ref[0, e, pl.ds(start, win), :])
        acc = acc + _dot(onehot.astype(BF16), jnp.concatenate(parts, axis=0))
    o_ref[0] = x_ref[0] + gate * acc

    @pl.when(overflow)
    def _():
        lane_all = lax.broadcasted_iota(jnp.int32, (tn, cap), 1)
        full = jnp.zeros(x_ref.shape[1:], F32)
        for e in range(ne):
            onehot = jnp.where(pos[:, e:e + 1] == lane_all, 1.0, 0.0).astype(BF16)
            full = full + _dot(onehot, y_ref[0, e])
        o_ref[0] = x_ref[0] + gate * full


def _moe_combine(x, pos_t, cnt, y, mod, tile_off, n_set, mod_sel, out_rows):
    b, t, d = x.shape
    _, ne, cap, _ = y.shape
    tn = PREFIX_CHUNK
    n_cnt = cnt.shape[-1]
    alias = out_rows == t
    out_off = tile_off if alias else 0
    return pl.pallas_call(
        functools.partial(_moe_combine_kernel, cap=cap, n_cnt=n_cnt),
        grid_spec=pltpu.PrefetchScalarGridSpec(
            num_scalar_prefetch=1,
            grid=(b, n_set // tn),
            in_specs=[pl.BlockSpec((1, tn, d), lambda bb, i, c: (bb, i + tile_off, 0)),
                      pl.BlockSpec((1, tn, ne), lambda bb, i, c: (bb, i, 0)),
                      pl.BlockSpec((1, ne, cap, d), lambda bb, i, c: (bb, 0, 0, 0)),
                      pl.BlockSpec((1, 1, 6, d), lambda bb, i, c: (bb, mod_sel, 0, 0))],
            out_specs=pl.BlockSpec((1, tn, d), lambda bb, i, c: (bb, i + out_off, 0))),
        out_shape=jax.ShapeDtypeStruct((b, out_rows, d), F32),
        input_output_aliases={1: 0} if alias else {},
        compiler_params=_cparams(2),
        name="moe_combine",
    )(cnt.reshape(-1), x, pos_t, y, mod)


def _moe(x, h, aff, mod, wg, wu, wd, layer, n_lat, with_ctx, final):
    b, t, d = x.shape
    n_ctx = t - n_lat
    aff_t = jnp.swapaxes(aff, 1, 2)
    sets = [(0, n_lat, 0, 1)]
    if with_ctx:
        sets.append((n_lat, n_ctx, 1, b))
    for start, n_set, mod_sel, bg in sets:
        cap = CAPACITY_FACTOR * n_set // N_EXPERTS
        a_set = aff_t[:, :, start:start + n_set]
        pos, cnt = _topk(a_set, cap)
        xg, gates = _moe_gather(h, pos, a_set, cnt, start // n_set, n_set, cap)
        y = _moe_ffn(xg, gates, wg, wu, wd, layer, bg)
        x = _moe_combine(x, jnp.swapaxes(pos, 1, 2), cnt, y, mod, start // PREFIX_CHUNK, n_set, mod_sel,
                         n_lat if final else t)
    return x


def _rope_tables(n_lat, n_ctx, rot_dim, starts):
    n_rows = n_lat // GRID_W
    rows = np.repeat(np.arange(n_rows, dtype=np.float32), GRID_W)
    cols = np.tile(np.arange(GRID_W, dtype=np.float32), n_rows)
    n_freq = rot_dim // 4
    inv_freq = (np.float32(ROPE_BASE) ** (-np.arange(n_freq, dtype=np.float32) / np.float32(n_freq))).astype(np.float32)
    ang = np.concatenate([rows[:, None] * inv_freq, cols[:, None] * inv_freq], axis=-1)
    half = rot_dim // 2
    cos_l, sin_l = np.cos(ang).astype(np.float32), np.sin(ang).astype(np.float32)
    cos = np.ones((n_lat + n_ctx, LANE), np.float32)
    sa = np.zeros((n_lat + n_ctx, LANE), np.float32)
    sb = np.zeros((n_lat + n_ctx, LANE), np.float32)
    for s in starts:
        cos[:n_lat, s:s + half] = cos_l
        cos[:n_lat, s + half:s + rot_dim] = cos_l
        sa[:n_lat, s:s + half] = -sin_l
        sb[:n_lat, s + half:s + rot_dim] = sin_l
    return jnp.asarray(cos), jnp.asarray(sa), jnp.asarray(sb)


def _pad_heads(w, heads, width):
    k = w.shape[0]
    w = w.reshape(k, heads, width)
    return jnp.pad(w, ((0, 0), (0, 0), (0, LANE - width))).reshape(k, heads * LANE)


def _mla_partner(a):
    half = MLA_ROPE // 2
    lane = jnp.arange(LANE)
    first = (lane >= MLA_NOPE) & (lane < MLA_NOPE + half)
    second = (lane >= MLA_NOPE + half) & (lane < MLA_QK)
    return jnp.where(first, jnp.roll(a, -half, axis=-1), jnp.where(second, jnp.roll(a, half, axis=-1), 0.0))


def _mla_weights(w_in, q_norm_g, w_uq, kv_norm_g, w_ukv):
    d = w_in.shape[0]
    lat = MLA_Q_LORA + MLA_KV_LORA
    rope_block = jnp.concatenate([jnp.zeros((d, MLA_NOPE), F32), w_in[:, lat:],
                                  jnp.zeros((d, LANE - MLA_QK), F32)], axis=1)
    ukv = w_ukv.reshape(MLA_KV_LORA, MLA_HEADS, MLA_NOPE + MLA_V)
    wuq = _pad_heads(w_uq, MLA_HEADS, MLA_QK)
    wuqp = _mla_partner(wuq.reshape(-1, MLA_HEADS, LANE)).reshape(wuq.shape)
    return {
        "win": jnp.concatenate([w_in[:, :lat], rope_block, _mla_partner(rope_block)], axis=1).astype(BF16),
        "qng": q_norm_g.reshape(1, -1),
        "wuq": wuq.astype(BF16),
        "wuqp": wuqp.astype(BF16),
        "kvng": kv_norm_g.reshape(1, -1),
        "wuk": _pad_heads(ukv[:, :, :MLA_NOPE].reshape(MLA_KV_LORA, -1), MLA_HEADS, MLA_NOPE).astype(BF16),
        "wuv": _pad_heads(ukv[:, :, MLA_NOPE:].reshape(MLA_KV_LORA, -1), MLA_HEADS, MLA_V).astype(BF16),
    }


def _diff_partner(a):
    half = DIFF_HEAD_DIM // 2
    shape = a.shape
    a = a.reshape(shape[:-1] + (shape[-1] // LANE, LANE))
    first = (jnp.arange(LANE) % DIFF_HEAD_DIM) < half
    return jnp.where(first, jnp.roll(a, -half, axis=-1), jnp.roll(a, half, axis=-1)).reshape(shape)


def _diff_weights(w_in):
    d = w_in.shape[0]
    return jnp.concatenate([w_in, _diff_partner(w_in[:, :2 * d])], axis=1).astype(BF16)


def _diff_tables(tabs, qn_g, kn_g):
    cos, sa, sb = tabs
    out = []
    for g, scale in ((qn_g, DIFF_HEAD_DIM ** -0.5 * LOG2E), (kn_g, 1.0)):
        g2 = jnp.tile(g, 2)
        out += [cos * g2 * scale, (sa + sb) * _diff_partner(g2) * scale]
    return out


def _mla_tables(tabs, qn_g, kn_g):
    cos, sa, sb = tabs
    out = []
    for g, scale in ((qn_g, MLA_QK ** -0.5 * LOG2E), (kn_g, 1.0)):
        gp = jnp.pad(g, (0, LANE - MLA_QK))
        out += [cos * gp * scale, (sa + sb) * _mla_partner(gp) * scale]
    return out


def kernel(x, c, ctx, c_ctx, ada_w, ada_b, norm_mix_g, norm_ffn_g, mla_w_in, mla_q_norm_g, mla_w_uq, mla_kv_norm_g, mla_w_ukv, mla_qn_g, mla_kn_g, mla_w_out, diff_w_in, diff_qn_g, diff_kn_g, diff_lambda_q1, diff_lambda_k1, diff_lambda_q2, diff_lambda_k2, diff_sub_g, diff_w_out, sg_w_in, sg_ln_g, sg_ln_b, sg_w_s, sg_b_s, sg_w_out, moe_router, moe_w_gate, moe_w_up, moe_w_down):
    b, n_lat, d = x.shape
    n_ctx = ctx.shape[1]
    depth = ada_w.shape[0]
    assert n_lat % ROW_TILE == 0 and n_ctx == ROW_TILE and n_lat % n_ctx == 0

    rows = -(-(b + 1) // 8) * 8
    cc = jnp.concatenate([c, c_ctx[None], jnp.zeros((rows - b - 1, d), F32)], axis=0)
    mods = _ada(cc, ada_w, ada_b).reshape(depth, rows, 6, d)
    xs = jnp.concatenate([x, ctx], axis=1)

    tabs_a = _rope_tables(n_lat, n_ctx, MLA_ROPE, (MLA_NOPE,))
    tabs_b = _rope_tables(n_lat, n_ctx, DIFF_HEAD_DIM, (0, DIFF_HEAD_DIM))

    for i in range(depth):
        kind, j = i % N_MIXERS, i // N_MIXERS
        last = i == depth - 1
        with_ctx = not last
        q_tiles = (n_lat + (n_ctx if with_ctx else 0)) // ROW_TILE
        mod = jnp.stack([mods[i, :b], jnp.broadcast_to(mods[i, b], (b, 6, d))], axis=1)
        g_mix, g_ffn = norm_mix_g[i].reshape(1, d), norm_ffn_g[i].reshape(1, d)
        if kind == 0:
            w = _mla_weights(mla_w_in[j], mla_q_norm_g[j], mla_w_uq[j], mla_kv_norm_g[j], mla_w_ukv[j])
            q, k, v = _mla_in(xs, g_mix, mod, w, _mla_tables(tabs_a, mla_qn_g[j], mla_kn_g[j]), n_lat)
            a = _attn_calls(_mla_attn_kernel, "mla_attn", [q], [k, v], [], n_lat, with_ctx,
                            MLA_HEADS, MLA_HEADS_PER_STEP, MLA_HEADS_PER_STEP * MLA_V)
            xs, h, aff = _out_proj(xs, a, mla_w_out[j].astype(BF16), mod, g_ffn, moe_router[i], n_lat, q_tiles)
        elif kind == 1:
            lam_init = 0.8 - 0.6 * math.exp(-0.3 * i)
            q0, q1, k, v = _diff_in(xs, g_mix, mod, _diff_weights(diff_w_in[j]),
                                    _diff_tables(tabs_b, diff_qn_g[j], diff_kn_g[j]), n_lat)
            wa = [a.reshape(1, -1) for a in (diff_lambda_q1[j], diff_lambda_k1[j], diff_lambda_q2[j],
                                             diff_lambda_k2[j], diff_sub_g[j])]
            a = _attn_calls(functools.partial(_diff_attn_kernel, lam_init=lam_init), "diff_attn", [q0, q1], [k, v],
                            wa, n_lat, with_ctx, DIFF_HEADS, DIFF_HEADS_PER_STEP, DIFF_HEADS_PER_STEP * LANE)
            xs, h, aff = _out_proj(xs, a, diff_w_out[j].astype(BF16), mod, g_ffn, moe_router[i], n_lat, q_tiles)
        else:
            w = {"win": sg_w_in[j].astype(BF16), "lng": sg_ln_g[j].reshape(1, -1), "lnb": sg_ln_b[j].reshape(1, -1),
                 "ws": sg_w_s[j].astype(BF16), "bs": sg_b_s[j].T, "wout": sg_w_out[j].astype(BF16)}
            xs, h, aff = _sg(xs, g_mix, mod, w, g_ffn, moe_router[i], n_lat)
        xs = _moe(xs, h, aff, mod, moe_w_gate, moe_w_up, moe_w_down, i, n_lat, with_ctx, last)
    return xs
```

```python
import functools
import math

import jax
import jax.numpy as jnp
import numpy as np
from jax import lax
from jax.experimental import pallas as pl
from jax.experimental.pallas import tpu as pltpu

F32 = jnp.float32
BF16 = jnp.bfloat16

GRID_W = 64
ROPE_BASE = 10000.0
EPS = 1e-6
N_MIXERS = 3

MLA_HEADS = 16
MLA_Q_LORA = 512
MLA_KV_LORA = 256
MLA_NOPE = 64
MLA_ROPE = 32
MLA_V = 64
MLA_QK = MLA_NOPE + MLA_ROPE

DIFF_HEADS = 8
DIFF_HEAD_DIM = 64
DIFF_V_DIM = 2 * DIFF_HEAD_DIM

SG_CHUNK = 128
SG_GROUPS = 8

N_EXPERTS = 16
CAPACITY_FACTOR = 2

LANE = 128
BF16_ROWS = 16
ROW_TILE = 256
ADA_COLUMN_TILE = 1536
PREFIX_CHUNK = 256
GATHER_WINDOW = 64
COMBINE_EXPERTS_PER_MATMUL = 4
MLA_HEADS_PER_STEP = 4
DIFF_HEADS_PER_STEP = 4
ATTN_Q_TILE = 256
LOG2E = math.log2(math.e)
VMEM_LIMIT = 56 * 1024 * 1024


def _cparams(n_axes):
    return pltpu.CompilerParams(dimension_semantics=("arbitrary",) * n_axes, vmem_limit_bytes=VMEM_LIMIT)


def _const_spec(shape):
    nd = len(shape)
    return pl.BlockSpec(shape, lambda *_: (0,) * nd)


def _dot(a, b):
    return jnp.dot(a, b, preferred_element_type=F32)


def _dot_nt(a, b):
    return lax.dot_general(a, b, (((1,), (1,)), ((), ())), preferred_element_type=F32)


def _rms(x, n):
    return x * lax.rsqrt(jnp.sum(x * x, axis=-1, keepdims=True) * (1.0 / n) + EPS)


def _norm_mod(x, g, shift, scale):
    return _rms(x, x.shape[-1]) * g * (1.0 + scale) + shift


def _silu(x):
    return x / (1.0 + jnp.exp(-x))


def _gelu_tanh(x):
    c = math.sqrt(2.0 / math.pi)
    return 0.5 * x * (1.0 + jnp.tanh(c * (x + 0.044715 * (x * x * x))))


def _ada_kernel(c_ref, w_ref, b_ref, o_ref):
    s = _silu(c_ref[...])
    o_ref[0] = jnp.dot(s, w_ref[0], preferred_element_type=F32, precision=lax.Precision.HIGHEST) + b_ref[0]


def _ada(cc, ada_w, ada_b):
    depth, d, six_d = ada_w.shape
    rows = cc.shape[0]
    tn = ADA_COLUMN_TILE
    return pl.pallas_call(
        _ada_kernel,
        grid=(depth, six_d // tn),
        in_specs=[
            _const_spec((rows, d)),
            pl.BlockSpec((1, d, tn), lambda i, j: (i, 0, j)),
            pl.BlockSpec((1, 1, tn), lambda i, j: (i, 0, j)),
        ],
        out_specs=pl.BlockSpec((1, rows, tn), lambda i, j: (i, 0, j)),
        out_shape=jax.ShapeDtypeStruct((depth, rows, six_d), F32),
        compiler_params=_cparams(2),
        name="ada",
    )(cc, ada_w, ada_b.reshape(depth, 1, six_d))


def _mod_spec(n_lat_tiles, d):
    return pl.BlockSpec((1, 1, 6, d), lambda b, i: (b, jnp.minimum(i // n_lat_tiles, 1), 0, 0))


def _stream_operands(xs, tm, n_lat):
    if not isinstance(xs, tuple):
        return [xs], [pl.BlockSpec((1, tm, xs.shape[-1]), lambda bb, i: (bb, i, 0))]
    lat, ctx = xs
    last = n_lat // tm - 1
    return [lat, ctx], [pl.BlockSpec((1, tm, lat.shape[-1]), lambda bb, i: (bb, jnp.minimum(i, last), 0)),
                        pl.BlockSpec((1, tm, ctx.shape[-1]), lambda bb, i: (bb, 0, 0))]


def _stream_tile(x_refs, n_lat):
    if len(x_refs) == 1:
        return x_refs[0][0]
    is_lat = pl.program_id(1) * x_refs[0].shape[1] < n_lat
    return jnp.where(is_lat, x_refs[0][0], x_refs[1][0])


def _mla_in_kernel(*refs, n_x, n_lat):
    (g_ref, mod_ref, win_ref, qng_ref, wuq_ref, wuqp_ref, kvng_ref, wuk_ref, wuv_ref,
     aq_ref, bq_ref, ak_ref, bk_ref, q_ref, k_ref, v_ref) = refs[n_x:]
    mod = mod_ref[0, 0]
    h = _norm_mod(_stream_tile(refs[:n_x], n_lat), g_ref[...], mod[0:1], mod[1:2])
    down = _dot(h.astype(BF16), win_ref[...])
    cq = (_rms(down[:, :MLA_Q_LORA], MLA_Q_LORA) * qng_ref[...]).astype(BF16)
    ckv = (_rms(down[:, MLA_Q_LORA:MLA_Q_LORA + MLA_KV_LORA], MLA_KV_LORA) * kvng_ref[...]).astype(BF16)
    q = _dot(cq, wuq_ref[...])
    qp = _dot(cq, wuqp_ref[...])
    kn = _dot(ckv, wuk_ref[...])
    v = _dot(ckv, wuv_ref[...])
    is_v = (lax.broadcasted_iota(jnp.int32, v.shape, 1) & (LANE - 1)) < MLA_V
    v_ref[0] = jnp.where(is_v, v, 1.0).astype(BF16)
    lat = MLA_Q_LORA + MLA_KV_LORA
    kr, krp = down[:, lat:lat + LANE], down[:, lat + LANE:]
    aq, bq, ak, bk = aq_ref[...], bq_ref[...], ak_ref[...], bk_ref[...]
    krot = krp * bk
    inv_n = 1.0 / MLA_QK
    for hd in range(MLA_HEADS):
        sl = slice(hd * LANE, (hd + 1) * LANE)
        qh = q[:, sl]
        rq = lax.rsqrt(jnp.sum(qh * qh, axis=-1, keepdims=True) * inv_n + EPS)
        group = qp[:, hd // 4 * LANE:(hd // 4 + 1) * LANE]
        shift = (MLA_NOPE - MLA_ROPE * (hd % 4)) % LANE
        qph = pltpu.roll(group, shift, 1) if shift else group
        q_ref[0, :, sl] = ((qh * aq + qph * bq) * rq).astype(BF16)
        kh = kn[:, sl] + kr
        rk = lax.rsqrt(jnp.sum(kh * kh, axis=-1, keepdims=True) * inv_n + EPS)
        k_ref[0, :, sl] = ((kh * ak + krot) * rk).astype(BF16)


def _mla_in(xs, g, mod, w, tabs, n_lat, t):
    tm = ROW_TILE
    x_ins, x_specs = _stream_operands(xs, tm, n_lat)
    b, _, d = x_ins[0].shape
    hw = MLA_HEADS * LANE
    row = lambda bb, i: (bb, i, 0)
    tab = pl.BlockSpec((tm, LANE), lambda bb, i: (i, 0))
    consts = [w["win"], w["qng"], w["wuq"], w["wuqp"], w["kvng"], w["wuk"], w["wuv"]]
    return pl.pallas_call(
        functools.partial(_mla_in_kernel, n_x=len(x_ins), n_lat=n_lat),
        grid=(b, t // tm),
        in_specs=x_specs + [_const_spec((1, d)), _mod_spec(n_lat // tm, d)]
        + [_const_spec(a.shape) for a in consts] + [tab] * 4,
        out_specs=[pl.BlockSpec((1, tm, hw), row)] * 3,
        out_shape=[jax.ShapeDtypeStruct((b, t, hw), BF16)] * 3,
        compiler_params=_cparams(2),
        name="mla_in",
    )(*x_ins, g, mod, *consts, *tabs)


def _diff_in_kernel(x_ref, g_ref, mod_ref, win_ref, aq_ref, bq_ref, ak_ref, bk_ref, q0_ref, q1_ref, k_ref, v_ref):
    mod = mod_ref[0, 0]
    d = x_ref.shape[-1]
    h = _norm_mod(x_ref[0], g_ref[...], mod[0:1], mod[1:2])
    qkv = _dot(h.astype(BF16), win_ref[...])
    ones = jnp.ones((x_ref.shape[1], LANE), BF16)
    for hd in range(DIFF_HEADS):
        v_ref[0, :, 2 * hd * LANE:(2 * hd + 1) * LANE] = qkv[:, 2 * d + hd * LANE:2 * d + (hd + 1) * LANE].astype(BF16)
        v_ref[0, :, (2 * hd + 1) * LANE:(2 * hd + 2) * LANE] = ones
    lo = lax.broadcasted_iota(jnp.int32, (x_ref.shape[1], LANE), 1) < DIFF_HEAD_DIM
    inv_n = 1.0 / DIFF_HEAD_DIM

    def normed_rope(off, hd, a, b):
        xh = qkv[:, off + hd * LANE:off + (hd + 1) * LANE]
        xp = qkv[:, off + 3 * d + hd * LANE:off + 3 * d + (hd + 1) * LANE]
        sq = xh * xh
        s_lo = jnp.sum(jnp.where(lo, sq, 0.0), axis=-1, keepdims=True)
        s_hi = jnp.sum(jnp.where(lo, 0.0, sq), axis=-1, keepdims=True)
        return (xh * a + xp * b) * lax.rsqrt(jnp.where(lo, s_lo, s_hi) * inv_n + EPS)

    aq, bq, ak, bk = aq_ref[...], bq_ref[...], ak_ref[...], bk_ref[...]
    for hd in range(DIFF_HEADS):
        sl = slice(hd * LANE, (hd + 1) * LANE)
        qh = normed_rope(0, hd, aq, bq)
        q0_ref[0, :, sl] = jnp.where(lo, qh, 0.0).astype(BF16)
        q1_ref[0, :, sl] = jnp.where(lo, 0.0, qh).astype(BF16)
        k_ref[0, :, sl] = normed_rope(d, hd, ak, bk).astype(BF16)


def _diff_in(x, g, mod, win, tabs, n_lat):
    b, t, d = x.shape
    tm = ROW_TILE
    row = lambda bb, i: (bb, i, 0)
    tab = pl.BlockSpec((tm, LANE), lambda bb, i: (i, 0))
    out = jax.ShapeDtypeStruct((b, t, d), BF16)
    return pl.pallas_call(
        _diff_in_kernel,
        grid=(b, t // tm),
        in_specs=[pl.BlockSpec((1, tm, d), row), _const_spec((1, d)), _mod_spec(n_lat // tm, d),
                  _const_spec(win.shape)] + [tab] * 4,
        out_specs=[pl.BlockSpec((1, tm, d), row)] * 3 + [pl.BlockSpec((1, tm, 2 * d), row)],
        out_shape=[out] * 3 + [jax.ShapeDtypeStruct((b, t, 2 * d), BF16)],
        compiler_params=_cparams(2),
        name="diff_in",
    )(x, g, mod, win, *tabs)


def _exp2_scores(q, k):
    s = _dot_nt(q, k)
    return jnp.exp2(s - jnp.max(s, axis=-1, keepdims=True)).astype(BF16)


def _mla_attn_kernel(q_ref, k_ref, v_ref, *rest):
    o_ref = rest[-1]
    tq = q_ref.shape[1]
    low = lax.broadcasted_iota(jnp.int32, (tq, LANE), 1) < MLA_V
    blocks = [slice(hd * LANE, (hd + 1) * LANE) for hd in range(MLA_HEADS_PER_STEP)]
    scores = [_dot_nt(q_ref[0, :, sl], k_ref[0, :, sl]) for sl in blocks]
    weights = [jnp.exp2(s - jnp.max(s, axis=-1, keepdims=True)).astype(BF16) for s in scores]
    for pr in range(MLA_HEADS_PER_STEP // 2):
        halves = []
        for hd in (2 * pr, 2 * pr + 1):
            r = _dot(weights[hd], v_ref[0, :, blocks[hd]])
            halves.append(r / pltpu.roll(r, LANE - MLA_V, 1))
        pair = jnp.where(low, halves[0], pltpu.roll(halves[1], MLA_V, 1))
        o_ref[0, :, pr * LANE:(pr + 1) * LANE] = pair.astype(o_ref.dtype)


def _attn_calls(kernel, name, qs, kvs, consts, n_lat, with_ctx, heads, hs, out_width):
    b, t, _ = kvs[0].shape
    n_ctx = t - n_lat
    ctx_blk = n_lat // n_ctx
    out_shape = jax.ShapeDtypeStruct((b, t, heads // hs * out_width), BF16)
    cspecs = [_const_spec(a.shape) for a in consts]

    def call(tq, q_map, kv_rows, kv_map, n_tiles, prev):
        q_specs = [pl.BlockSpec((1, tq, a.shape[-1] // (heads // hs)), q_map) for a in qs]
        kv_specs = [pl.BlockSpec((1, kv_rows, a.shape[-1] // (heads // hs)), kv_map) for a in kvs]
        extra = [] if prev is None else [pl.BlockSpec(memory_space=pl.ANY)]
        n_in = len(qs) + len(kvs) + len(consts)
        return pl.pallas_call(
            kernel,
            grid=(b, heads // hs, n_tiles),
            in_specs=q_specs + kv_specs + cspecs + extra,
            out_specs=pl.BlockSpec((1, tq, out_width), q_map),
            out_shape=out_shape,
            input_output_aliases={} if prev is None else {n_in: 0},
            compiler_params=_cparams(3),
            name=name,
        )(*qs, *kvs, *consts, *([] if prev is None else [prev]))

    tq = min(ATTN_Q_TILE, n_lat)
    out = call(tq, lambda bb, hg, i: (bb, i, hg), t, lambda bb, hg, i: (bb, 0, hg), n_lat // tq, None)
    if with_ctx:
        ctx_map = lambda bb, hg, i: (bb, ctx_blk, hg)
        out = call(n_ctx, ctx_map, n_ctx, ctx_map, 1, out)
    return out


def _diff_attn_kernel(q0_ref, q1_ref, k_ref, v_ref, lq1_ref, lk1_ref, lq2_ref, lk2_ref, sub_ref, *rest, lam_init):
    o_ref = rest[-1]
    lam = (jnp.exp(jnp.sum(lq1_ref[...] * lk1_ref[...], axis=-1, keepdims=True))
           - jnp.exp(jnp.sum(lq2_ref[...] * lk2_ref[...], axis=-1, keepdims=True)) + lam_init)
    for a in range(DIFF_HEADS_PER_STEP):
        sl = slice(a * LANE, (a + 1) * LANE)
        k, v = k_ref[0, :, sl], v_ref[0, :, 2 * a * LANE:(2 * a + 2) * LANE]
        scores = [_dot_nt(q_ref[0, :, sl], k) for q_ref in (q0_ref, q1_ref)]
        outs = []
        for s in scores:
            r = _dot(jnp.exp2(s - jnp.max(s, axis=-1, keepdims=True)).astype(BF16), v)
            outs.append(r[:, :LANE] / r[:, LANE:])
        o = outs[0] - lam * outs[1]
        o_ref[0, :, sl] = (_rms(o, DIFF_V_DIM) * sub_ref[...] * (1.0 - lam_init)).astype(o_ref.dtype)


def _route(x, mod, gf_ref, rhi_ref, rlo_ref, h_ref, aff_ref):
    h = _norm_mod(x, gf_ref[...], mod[3:4], mod[4:5])
    h_hi = h.astype(BF16)
    h_lo = (h - h_hi.astype(F32)).astype(BF16)
    h_ref[0] = h_hi
    logits = _dot(h_hi, rhi_ref[...]) + (_dot(h_lo, rhi_ref[...]) + _dot(h_hi, rlo_ref[...]))
    e = jnp.exp(logits - jnp.max(logits, axis=-1, keepdims=True))
    aff_ref[0] = e / jnp.sum(e, axis=-1, keepdims=True)


def _route_operands(x, norm_ffn_g, router, n_tiles, tm):
    b, _, d = x.shape
    ne = router.shape[-1]
    row = lambda bb, i: (bb, i, 0)
    r_hi = router.astype(BF16)
    r_lo = (router - r_hi.astype(F32)).astype(BF16)
    ins = [norm_ffn_g, r_hi, r_lo]
    in_specs = [_const_spec(a.shape) for a in ins]
    out_specs = [pl.BlockSpec((1, tm, d), row), pl.BlockSpec((1, tm, ne), row)]
    out_shape = [jax.ShapeDtypeStruct((b, n_tiles * tm, d), BF16), jax.ShapeDtypeStruct((b, n_tiles * tm, ne), F32)]
    return ins, in_specs, out_specs, out_shape


def _out_proj_kernel(*refs, n_x, n_lat):
    a_ref, w_ref, mod_ref, gf_ref, rhi_ref, rlo_ref, o_ref, h_ref, aff_ref = refs[n_x:]
    mod = mod_ref[0, 0]
    x = _stream_tile(refs[:n_x], n_lat) + mod[2:3] * _dot(a_ref[0], w_ref[...])
    o_ref[0] = x
    _route(x, mod, gf_ref, rhi_ref, rlo_ref, h_ref, aff_ref)


def _out_proj(xs, a, w, mod, norm_ffn_g, router, n_lat, n_tiles):
    tm = ROW_TILE
    x_ins, x_specs = _stream_operands(xs, tm, n_lat)
    b, _, d = x_ins[0].shape
    t = a.shape[1]
    row = lambda bb, i: (bb, i, 0)
    r_ins, r_in_specs, r_out_specs, r_out_shape = _route_operands(x_ins[0], norm_ffn_g, router, n_tiles, tm)
    return pl.pallas_call(
        functools.partial(_out_proj_kernel, n_x=len(x_ins), n_lat=n_lat),
        grid=(b, n_tiles),
        in_specs=x_specs + [pl.BlockSpec((1, tm, a.shape[-1]), row), _const_spec(w.shape),
                            _mod_spec(n_lat // tm, d)] + r_in_specs,
        out_specs=[pl.BlockSpec((1, tm, d), row)] + r_out_specs,
        out_shape=[jax.ShapeDtypeStruct((b, t, d), F32)] + r_out_shape,
        input_output_aliases={0: 0} if len(x_ins) == 1 else {},
        compiler_params=_cparams(2),
        name="out_proj",
    )(*x_ins, a, w, mod, *r_ins)


def _sg_kernel(x_ref, g_ref, mod_ref, win_ref, lng_ref, lnb_ref, ws_ref, bs_ref, wout_ref, gf_ref, rhi_ref, rlo_ref,
               o_ref, h_ref, aff_ref, gated_ref):
    mod = mod_ref[0, 0]
    x = x_ref[0]
    tm = x.shape[0]
    width = lng_ref.shape[-1]
    gdim = width // SG_GROUPS
    h = _norm_mod(x, g_ref[...], mod[0:1], mod[1:2])
    z = _gelu_tanh(_dot(h.astype(BF16), win_ref[...]))
    u, v = z[:, :width], z[:, width:]
    mu = jnp.mean(v, axis=-1, keepdims=True)
    vc = v - mu
    var = jnp.mean(vc * vc, axis=-1, keepdims=True)
    vn = (vc * lax.rsqrt(var + EPS) * lng_ref[...] + lnb_ref[...]).astype(BF16)
    for c in range(tm // SG_CHUNK):
        rows = slice(c * SG_CHUNK, (c + 1) * SG_CHUNK)
        for gi in range(SG_GROUPS):
            cols = slice(gi * gdim, (gi + 1) * gdim)
            mixed = _dot(ws_ref[gi], vn[rows, cols]) + bs_ref[:, gi:gi + 1]
            gated_ref[rows, cols] = (u[rows, cols] * mixed).astype(BF16)
    x = x + mod[2:3] * _dot(gated_ref[...], wout_ref[...])
    o_ref[0] = x
    _route(x, mod, gf_ref, rhi_ref, rlo_ref, h_ref, aff_ref)


def _sg(x, g, mod, w, norm_ffn_g, router, n_lat):
    b, t, d = x.shape
    tm = ROW_TILE
    row = lambda bb, i: (bb, i, 0)
    consts = [w["win"], w["lng"], w["lnb"], w["ws"], w["bs"], w["wout"]]
    r_ins, r_in_specs, r_out_specs, r_out_shape = _route_operands(x, norm_ffn_g, router, t // tm, tm)
    return pl.pallas_call(
        _sg_kernel,
        grid=(b, t // tm),
        in_specs=[pl.BlockSpec((1, tm, d), row), _const_spec((1, d)), _mod_spec(n_lat // tm, d)]
        + [_const_spec(a.shape) for a in consts] + r_in_specs,
        out_specs=[pl.BlockSpec((1, tm, d), row)] + r_out_specs,
        out_shape=[jax.ShapeDtypeStruct(x.shape, F32)] + r_out_shape,
        scratch_shapes=[pltpu.VMEM((tm, w["lng"].shape[-1]), BF16)],
        input_output_aliases={0: 0},
        compiler_params=_cparams(2),
        name="chunk_mlp",
    )(x, g, mod, *consts, *r_ins)


def _prefix_counts(mask, tri):
    out = []
    carry = jnp.zeros((mask.shape[0], 1), F32)
    before = [carry]
    for c in range(mask.shape[1] // PREFIX_CHUNK):
        m = mask[:, c * PREFIX_CHUNK:(c + 1) * PREFIX_CHUNK]
        out.append(_dot(m.astype(BF16), tri) + carry)
        carry = carry + jnp.sum(m, axis=-1, keepdims=True)
        before.append(carry)
    return out, before


def _topk_kernel(aff_ref, pos_ref, cnt_ref, *, cap):
    a = aff_ref[0]
    ne = a.shape[0]
    bits = pltpu.bitcast(a, jnp.int32)

    def step(i, lo):
        cand = lo | jnp.left_shift(jnp.int32(1), 30 - i)
        cnt = jnp.sum(jnp.where(bits >= cand, 1.0, 0.0), axis=-1, keepdims=True)
        return jnp.where(cnt >= cap, cand, lo)

    thr = lax.fori_loop(0, 31, step, jnp.zeros((ne, 1), jnp.int32))
    gt = jnp.where(bits > thr, 1.0, 0.0)
    eq = jnp.where(bits == thr, 1.0, 0.0)
    room = cap - jnp.sum(gt, axis=-1, keepdims=True)
    ri = lax.broadcasted_iota(jnp.int32, (PREFIX_CHUNK, PREFIX_CHUNK), 0)
    ci = lax.broadcasted_iota(jnp.int32, (PREFIX_CHUNK, PREFIX_CHUNK), 1)
    tri = jnp.where(ri <= ci, 1.0, 0.0).astype(BF16)
    eq_rank, _ = _prefix_counts(eq, tri)
    sel = jnp.concatenate([
        jnp.maximum(gt[:, c * PREFIX_CHUNK:(c + 1) * PREFIX_CHUNK],
                    jnp.where(r <= room, eq[:, c * PREFIX_CHUNK:(c + 1) * PREFIX_CHUNK], 0.0))
        for c, r in enumerate(eq_rank)], axis=-1)
    sel_rank, before = _prefix_counts(sel, tri)
    for c, r in enumerate(sel_rank):
        cols = slice(c * PREFIX_CHUNK, (c + 1) * PREFIX_CHUNK)
        pos_ref[0, :, cols] = jnp.where(sel[:, cols] > 0.0, r - 1.0, -1.0).astype(jnp.int32)
    for c, cnt in enumerate(before):
        cnt_ref[0, :, c:c + 1] = cnt.astype(jnp.int32)


def _topk(aff_t, cap):
    b, ne, n = aff_t.shape
    spec = pl.BlockSpec((1, ne, n), lambda bb: (bb, 0, 0))
    n_cnt = n // PREFIX_CHUNK + 1
    return pl.pallas_call(
        functools.partial(_topk_kernel, cap=cap),
        grid=(b,),
        in_specs=[spec],
        out_specs=[spec, pl.BlockSpec((1, ne, n_cnt), lambda bb: (bb, 0, 0))],
        out_shape=[jax.ShapeDtypeStruct((b, ne, n), jnp.int32), jax.ShapeDtypeStruct((b, ne, n_cnt), jnp.int32)],
        compiler_params=_cparams(1),
        name="topk",
    )(aff_t)


def _moe_gather_kernel(cnt_ref, h_ref, pos_ref, aff_ref, xg_ref, gate_ref, *, cap, n_cnt):
    i = pl.program_id(1)
    ne, tn = pos_ref.shape[1:]
    win = min(GATHER_WINDOW, cap)
    row = (pl.program_id(0) * ne) * n_cnt + i

    @pl.when(i == 0)
    def _():
        xg_ref[...] = jnp.zeros(xg_ref.shape, xg_ref.dtype)
        gate_ref[...] = jnp.zeros(gate_ref.shape, gate_ref.dtype)

    h = h_ref[0]
    slot = lax.broadcasted_iota(jnp.int32, (win, tn), 0)

    def add_window(e, start, hit, res):
        rows = pl.ds(start, win)
        xg_ref[0, e, rows, :] = (xg_ref[0, e, rows, :].astype(F32) + res).astype(BF16)
        gate_ref[0, e, rows, :] += jnp.sum(jnp.where(hit, aff_ref[0, e:e + 1, :], 0.0), axis=-1, keepdims=True)

    starts, his, hits = [], [], []
    for e in range(ne):
        lo, hi = cnt_ref[row + e * n_cnt], cnt_ref[row + e * n_cnt + 1]
        start = pl.multiple_of(jnp.minimum((lo // BF16_ROWS) * BF16_ROWS, cap - win), BF16_ROWS)
        starts.append(start)
        his.append(hi)
        hits.append(pos_ref[0, e:e + 1, :] - start == slot)
    onehot = jnp.concatenate([jnp.where(hit, 1.0, 0.0).astype(BF16) for hit in hits], axis=0)
    res = _dot(onehot, h)
    for e in range(ne):
        add_window(e, starts[e], hits[e], res[e * win:(e + 1) * win])

    for e in range(ne):
        first = starts[e] + win
        n_more = jnp.maximum(his[e] - first + (win - 1), 0) // win

        def more(k, carry, e=e, first=first):
            lower = first + k * win
            start = pl.multiple_of(jnp.minimum(lower, cap - win), BF16_ROWS)
            p = pos_ref[0, e:e + 1, :]
            hit = jnp.logical_and(p - start == slot, p >= lower)
            add_window(e, start, hit, _dot(jnp.where(hit, 1.0, 0.0).astype(BF16), h))
            return carry

        lax.fori_loop(0, n_more, more, 0)


def _moe_gather(h, pos, aff_t, cnt, set_block, n_set, cap):
    b, _, d = h.shape
    ne = pos.shape[1]
    tn = PREFIX_CHUNK
    tiles = n_set // tn
    tile = pl.BlockSpec((1, ne, tn), lambda bb, i, c: (bb, 0, i))
    return pl.pallas_call(
        functools.partial(_moe_gather_kernel, cap=cap, n_cnt=cnt.shape[-1]),
        grid_spec=pltpu.PrefetchScalarGridSpec(
            num_scalar_prefetch=1,
            grid=(b, tiles),
            in_specs=[pl.BlockSpec((1, tn, d), lambda bb, i, c: (bb, set_block * tiles + i, 0)), tile, tile],
            out_specs=[pl.BlockSpec((1, ne, cap, d), lambda bb, i, c: (bb, 0, 0, 0)),
                       pl.BlockSpec((1, ne, cap, 1), lambda bb, i, c: (bb, 0, 0, 0))]),
        out_shape=[jax.ShapeDtypeStruct((b, ne, cap, d), BF16), jax.ShapeDtypeStruct((b, ne, cap, 1), F32)],
        compiler_params=_cparams(2),
        name="moe_gather",
    )(cnt.reshape(-1), h, pos, aff_t)


def _moe_ffn_kernel(xg_ref, gate_ref, wg32_ref, wu32_ref, wd32_ref, y_ref, wg_ref, wu_ref, wd_ref):
    @pl.when(pl.program_id(1) == 0)
    def _():
        wg_ref[...] = wg32_ref[0, 0].astype(BF16)
        wu_ref[...] = wu32_ref[0, 0].astype(BF16)
        wd_ref[...] = wd32_ref[0, 0].astype(BF16)

    bg, _, cap, _ = xg_ref.shape
    rows = lambda ref: ref[0, 0] if bg == 1 else jnp.concatenate([ref[bb, 0] for bb in range(bg)], axis=0)
    xg = rows(xg_ref)
    hid = _silu(_dot(xg, wg_ref[...])) * _dot(xg, wu_ref[...])
    y = _dot(hid.astype(BF16), wd_ref[...]) * rows(gate_ref)
    for bb in range(bg):
        y_ref[bb, 0] = y[bb * cap:(bb + 1) * cap].astype(BF16)


def _moe_ffn(xg, gates, wg, wu, wd, layer, bg):
    b, ne, cap, d = xg.shape
    ff = wg.shape[-1]
    tok = pl.BlockSpec((bg, 1, cap, d), lambda e, i: (i, e, 0, 0))
    return pl.pallas_call(
        _moe_ffn_kernel,
        grid=(ne, b // bg),
        in_specs=[tok, pl.BlockSpec((bg, 1, cap, 1), lambda e, i: (i, e, 0, 0)),
                  pl.BlockSpec((1, 1, d, ff), lambda e, i: (layer, e, 0, 0)),
                  pl.BlockSpec((1, 1, d, ff), lambda e, i: (layer, e, 0, 0)),
                  pl.BlockSpec((1, 1, ff, d), lambda e, i: (layer, e, 0, 0))],
        out_specs=tok,
        out_shape=jax.ShapeDtypeStruct((b, ne, cap, d), BF16),
        scratch_shapes=[pltpu.VMEM((d, ff), BF16), pltpu.VMEM((d, ff), BF16), pltpu.VMEM((ff, d), BF16)],
        compiler_params=_cparams(2),
        name="moe_ffn",
    )(xg, gates, wg, wu, wd)


def _moe_combine_kernel(cnt_ref, x_ref, pos_ref, y_ref, mod_ref, o_ref, *, cap, n_cnt):
    pos = pos_ref[0]
    tn, ne = pos.shape
    win = min(GATHER_WINDOW, cap)
    per = COMBINE_EXPERTS_PER_MATMUL
    gate = mod_ref[0, 0][5:6]
    row = (pl.program_id(0) * ne) * n_cnt + pl.program_id(1)
    lane = lax.broadcasted_iota(jnp.int32, (tn, per * win), 1)
    acc = jnp.zeros(x_ref.shape[1:], F32)
    overflow = False
    for g in range(ne // per):
        onehot = jnp.zeros((tn, per * win), F32)
        parts = []
        for j in range(per):
            e = g * per + j
            lo, hi = cnt_ref[row + e * n_cnt], cnt_ref[row + e * n_cnt + 1]
            start = pl.multiple_of(jnp.minimum((lo // BF16_ROWS) * BF16_ROWS, cap - win), BF16_ROWS)
            overflow = jnp.logical_or(overflow, hi > start + win)
            rel = pos[:, e:e + 1] - start
            target = jnp.where(jnp.logical_and(rel >= 0, rel < win), rel + j * win, -1)
            onehot = jnp.where(target == lane, 1.0, onehot)
            parts.append(y_ref[0, e, pl.ds(start, win), :])
        acc = acc + _dot(onehot.astype(BF16), jnp.concatenate(parts, axis=0))
    o_ref[0] = x_ref[0] + gate * acc

    @pl.when(overflow)
    def _():
        lane_all = lax.broadcasted_iota(jnp.int32, (tn, cap), 1)
        full = jnp.zeros(x_ref.shape[1:], F32)
        for e in range(ne):
            onehot = jnp.where(pos[:, e:e + 1] == lane_all, 1.0, 0.0).astype(BF16)
            full = full + _dot(onehot, y_ref[0, e])
        o_ref[0] = x_ref[0] + gate * full


def _moe_combine(x, pos_t, cnt, y, mod, tile_off, n_set, mod_sel, out_rows):
    b, t, d = x.shape
    _, ne, cap, _ = y.shape
    tn = PREFIX_CHUNK
    n_cnt = cnt.shape[-1]
    alias = out_rows == t
    out_off = tile_off if alias else 0
    return pl.pallas_call(
        functools.partial(_moe_combine_kernel, cap=cap, n_cnt=n_cnt),
        grid_spec=pltpu.PrefetchScalarGridSpec(
            num_scalar_prefetch=1,
            grid=(b, n_set // tn),
            in_specs=[pl.BlockSpec((1, tn, d), lambda bb, i, c: (bb, i + tile_off, 0)),
                      pl.BlockSpec((1, tn, ne), lambda bb, i, c: (bb, i, 0)),
                      pl.BlockSpec((1, ne, cap, d), lambda bb, i, c: (bb, 0, 0, 0)),
                      pl.BlockSpec((1, 1, 6, d), lambda bb, i, c: (bb, mod_sel, 0, 0))],
            out_specs=pl.BlockSpec((1, tn, d), lambda bb, i, c: (bb, i + out_off, 0))),
        out_shape=jax.ShapeDtypeStruct((b, out_rows, d), F32),
        input_output_aliases={1: 0} if alias else {},
        compiler_params=_cparams(2),
        name="moe_combine",
    )(cnt.reshape(-1), x, pos_t, y, mod)


def _moe(x, h, aff, mod, wg, wu, wd, layer, n_lat, with_ctx, final):
    b, t, d = x.shape
    n_ctx = t - n_lat
    aff_t = jnp.swapaxes(aff, 1, 2)
    sets = [(0, n_lat, 0, 1)]
    if with_ctx:
        sets.append((n_lat, n_ctx, 1, b))
    for start, n_set, mod_sel, bg in sets:
        cap = CAPACITY_FACTOR * n_set // N_EXPERTS
        a_set = aff_t[:, :, start:start + n_set]
        pos, cnt = _topk(a_set, cap)
        xg, gates = _moe_gather(h, pos, a_set, cnt, start // n_set, n_set, cap)
        y = _moe_ffn(xg, gates, wg, wu, wd, layer, bg)
        x = _moe_combine(x, jnp.swapaxes(pos, 1, 2), cnt, y, mod, start // PREFIX_CHUNK, n_set, mod_sel,
                         n_lat if final else t)
    return x


def _rope_tables(n_lat, n_ctx, rot_dim, starts):
    n_rows = n_lat // GRID_W
    rows = np.repeat(np.arange(n_rows, dtype=np.float32), GRID_W)
    cols = np.tile(np.arange(GRID_W, dtype=np.float32), n_rows)
    n_freq = rot_dim // 4
    inv_freq = (np.float32(ROPE_BASE) ** (-np.arange(n_freq, dtype=np.float32) / np.float32(n_freq))).astype(np.float32)
    ang = np.concatenate([rows[:, None] * inv_freq, cols[:, None] * inv_freq], axis=-1)
    half = rot_dim // 2
    cos_l, sin_l = np.cos(ang).astype(np.float32), np.sin(ang).astype(np.float32)
    cos = np.ones((n_lat + n_ctx, LANE), np.float32)
    sa = np.zeros((n_lat + n_ctx, LANE), np.float32)
    sb = np.zeros((n_lat + n_ctx, LANE), np.float32)
    for s in starts:
        cos[:n_lat, s:s + half] = cos_l
        cos[:n_lat, s + half:s + rot_dim] = cos_l
        sa[:n_lat, s:s + half] = -sin_l
        sb[:n_lat, s + half:s + rot_dim] = sin_l
    return jnp.asarray(cos), jnp.asarray(sa), jnp.asarray(sb)


def _pad_heads(w, heads, width):
    k = w.shape[0]
    w = w.reshape(k, heads, width)
    return jnp.pad(w, ((0, 0), (0, 0), (0, LANE - width))).reshape(k, heads * LANE)


def _mla_partner(a):
    half = MLA_ROPE // 2
    lane = jnp.arange(LANE)
    first = (lane >= MLA_NOPE) & (lane < MLA_NOPE + half)
    second = (lane >= MLA_NOPE + half) & (lane < MLA_QK)
    return jnp.where(first, jnp.roll(a, -half, axis=-1), jnp.where(second, jnp.roll(a, half, axis=-1), 0.0))


def _mla_weights(w_in, q_norm_g, w_uq, kv_norm_g, w_ukv):
    d = w_in.shape[0]
    lat = MLA_Q_LORA + MLA_KV_LORA
    rope_block = jnp.concatenate([jnp.zeros((d, MLA_NOPE), F32), w_in[:, lat:],
                                  jnp.zeros((d, LANE - MLA_QK), F32)], axis=1)
    ukv = w_ukv.reshape(MLA_KV_LORA, MLA_HEADS, MLA_NOPE + MLA_V)
    wuq = _pad_heads(w_uq, MLA_HEADS, MLA_QK)
    wuqp = _mla_partner(wuq.reshape(-1, MLA_HEADS, LANE))[:, :, MLA_NOPE:MLA_QK].reshape(wuq.shape[0], -1)
    return {
        "win": jnp.concatenate([w_in[:, :lat], rope_block, _mla_partner(rope_block)], axis=1).astype(BF16),
        "qng": q_norm_g.reshape(1, -1),
        "wuq": wuq.astype(BF16),
        "wuqp": wuqp.astype(BF16),
        "kvng": kv_norm_g.reshape(1, -1),
        "wuk": _pad_heads(ukv[:, :, :MLA_NOPE].reshape(MLA_KV_LORA, -1), MLA_HEADS, MLA_NOPE).astype(BF16),
        "wuv": _pad_heads(ukv[:, :, MLA_NOPE:].reshape(MLA_KV_LORA, -1), MLA_HEADS, MLA_V).astype(BF16),
    }


def _diff_partner(a):
    half = DIFF_HEAD_DIM // 2
    shape = a.shape
    a = a.reshape(shape[:-1] + (shape[-1] // LANE, LANE))
    first = (jnp.arange(LANE) % DIFF_HEAD_DIM) < half
    return jnp.where(first, jnp.roll(a, -half, axis=-1), jnp.roll(a, half, axis=-1)).reshape(shape)


def _diff_weights(w_in):
    d = w_in.shape[0]
    return jnp.concatenate([w_in, _diff_partner(w_in[:, :2 * d])], axis=1).astype(BF16)


def _diff_tables(tabs, qn_g, kn_g):
    cos, sa, sb = tabs
    out = []
    for g, scale in ((qn_g, DIFF_HEAD_DIM ** -0.5 * LOG2E), (kn_g, 1.0)):
        g2 = jnp.tile(g, 2)
        out += [cos * g2 * scale, (sa + sb) * _diff_partner(g2) * scale]
    return out


def _mla_tables(tabs, qn_g, kn_g):
    cos, sa, sb = tabs
    out = []
    for g, scale in ((qn_g, MLA_QK ** -0.5 * LOG2E), (kn_g, 1.0)):
        gp = jnp.pad(g, (0, LANE - MLA_QK))
        out += [cos * gp * scale, (sa + sb) * _mla_partner(gp) * scale]
    return out


def kernel(x, c, ctx, c_ctx, ada_w, ada_b, norm_mix_g, norm_ffn_g, mla_w_in, mla_q_norm_g, mla_w_uq, mla_kv_norm_g, mla_w_ukv, mla_qn_g, mla_kn_g, mla_w_out, diff_w_in, diff_qn_g, diff_kn_g, diff_lambda_q1, diff_lambda_k1, diff_lambda_q2, diff_lambda_k2, diff_sub_g, diff_w_out, sg_w_in, sg_ln_g, sg_ln_b, sg_w_s, sg_b_s, sg_w_out, moe_router, moe_w_gate, moe_w_up, moe_w_down):
    b, n_lat, d = x.shape
    n_ctx = ctx.shape[1]
    depth = ada_w.shape[0]
    assert n_lat % ROW_TILE == 0 and n_ctx == ROW_TILE and n_lat % n_ctx == 0

    rows = -(-(b + 1) // 8) * 8
    cc = jnp.concatenate([c, c_ctx[None], jnp.zeros((rows - b - 1, d), F32)], axis=0)
    mods = _ada(cc, ada_w, ada_b).reshape(depth, rows, 6, d)
    xs = (x, ctx) if depth > 1 else jnp.concatenate([x, ctx], axis=1)
    t = n_lat + n_ctx

    tabs_a = _rope_tables(n_lat, n_ctx, MLA_ROPE, (MLA_NOPE,))
    tabs_b = _rope_tables(n_lat, n_ctx, DIFF_HEAD_DIM, (0, DIFF_HEAD_DIM))

    for i in range(depth):
        kind, j = i % N_MIXERS, i // N_MIXERS
        last = i == depth - 1
        with_ctx = not last
        q_tiles = (n_lat + (n_ctx if with_ctx else 0)) // ROW_TILE
        mod = jnp.stack([mods[i, :b], jnp.broadcast_to(mods[i, b], (b, 6, d))], axis=1)
        g_mix, g_ffn = norm_mix_g[i].reshape(1, d), norm_ffn_g[i].reshape(1, d)
        if kind == 0:
            w = _mla_weights(mla_w_in[j], mla_q_norm_g[j], mla_w_uq[j], mla_kv_norm_g[j], mla_w_ukv[j])
            q, k, v = _mla_in(xs, g_mix, mod, w, _mla_tables(tabs_a, mla_qn_g[j], mla_kn_g[j]), n_lat, t)
            a = _attn_calls(_mla_attn_kernel, "mla_attn", [q], [k, v], [], n_lat, with_ctx,
                            MLA_HEADS, MLA_HEADS_PER_STEP, MLA_HEADS_PER_STEP * MLA_V)
            xs, h, aff = _out_proj(xs, a, mla_w_out[j].astype(BF16), mod, g_ffn, moe_router[i], n_lat, q_tiles)
        elif kind == 1:
            lam_init = 0.8 - 0.6 * math.exp(-0.3 * i)
            q0, q1, k, v = _diff_in(xs, g_mix, mod, _diff_weights(diff_w_in[j]),
                                    _diff_tables(tabs_b, diff_qn_g[j], diff_kn_g[j]), n_lat)
            wa = [a.reshape(1, -1) for a in (diff_lambda_q1[j], diff_lambda_k1[j], diff_lambda_q2[j],
                                             diff_lambda_k2[j], diff_sub_g[j])]
            a = _attn_calls(functools.partial(_diff_attn_kernel, lam_init=lam_init), "diff_attn", [q0, q1], [k, v],
                            wa, n_lat, with_ctx, DIFF_HEADS, DIFF_HEADS_PER_STEP, DIFF_HEADS_PER_STEP * LANE)
            xs, h, aff = _out_proj(xs, a, diff_w_out[j].astype(BF16), mod, g_ffn, moe_router[i], n_lat, q_tiles)
        else:
            w = {"win": sg_w_in[j].astype(BF16), "lng": sg_ln_g[j].reshape(1, -1), "lnb": sg_ln_b[j].reshape(1, -1),
                 "ws": sg_w_s[j].astype(BF16), "bs": sg_b_s[j].T, "wout": sg_w_out[j].astype(BF16)}
            xs, h, aff = _sg(xs, g_mix, mod, w, g_ffn, moe_router[i], n_lat)
        xs = _moe(xs, h, aff, mod, moe_w_gate, moe_w_up, moe_w_down, i, n_lat, with_ctx, last)
    return xs
```

```python
import functools
import math

import jax
import jax.numpy as jnp
import numpy as np
from jax import lax
from jax.experimental import pallas as pl
from jax.experimental.pallas import tpu as pltpu

F32 = jnp.float32
BF16 = jnp.bfloat16

GRID_W = 64
ROPE_BASE = 10000.0
EPS = 1e-6
N_MIXERS = 3

MLA_HEADS = 16
MLA_Q_LORA = 512
MLA_KV_LORA = 256
MLA_NOPE = 64
MLA_ROPE = 32
MLA_V = 64
MLA_QK = MLA_NOPE + MLA_ROPE

DIFF_HEADS = 8
DIFF_HEAD_DIM = 64
DIFF_V_DIM = 2 * DIFF_HEAD_DIM

SG_CHUNK = 128
SG_GROUPS = 8

N_EXPERTS = 16
CAPACITY_FACTOR = 2

LANE = 128
BF16_ROWS = 16
ROW_TILE = 256
SG_COLUMN_CHUNK = 1024
FFN_COLUMN_CHUNK = 256
ADA_COLUMN_TILE = 1536
PREFIX_CHUNK = 256
GATHER_WINDOW = 64
COMBINE_EXPERTS_PER_MATMUL = 4
MLA_HEADS_PER_STEP = 4
DIFF_HEADS_PER_STEP = 4
ATTN_Q_TILE = 256
LOG2E = math.log2(math.e)
VMEM_LIMIT = 56 * 1024 * 1024


def _cparams(n_axes):
    return pltpu.CompilerParams(dimension_semantics=("arbitrary",) * n_axes, vmem_limit_bytes=VMEM_LIMIT)


def _const_spec(shape):
    nd = len(shape)
    return pl.BlockSpec(shape, lambda *_: (0,) * nd)


def _dot(a, b):
    return jnp.dot(a, b, preferred_element_type=F32)


def _dot_nt(a, b):
    return lax.dot_general(a, b, (((1,), (1,)), ((), ())), preferred_element_type=F32)


def _rms(x, n):
    return x * lax.rsqrt(jnp.sum(x * x, axis=-1, keepdims=True) * (1.0 / n) + EPS)


def _norm_mod(x, g, shift, scale):
    return _rms(x, x.shape[-1]) * g * (1.0 + scale) + shift


def _silu(x):
    return x / (1.0 + jnp.exp(-x))


def _gelu_tanh(x):
    c = math.sqrt(2.0 / math.pi)
    return 0.5 * x * (1.0 + jnp.tanh(c * (x + 0.044715 * (x * x * x))))


def _ada_kernel(c_ref, w_ref, b_ref, o_ref):
    s = _silu(c_ref[...])
    o_ref[0] = jnp.dot(s, w_ref[0], preferred_element_type=F32, precision=lax.Precision.HIGHEST) + b_ref[0]


def _ada(cc, ada_w, ada_b):
    depth, d, six_d = ada_w.shape
    rows = cc.shape[0]
    tn = ADA_COLUMN_TILE
    return pl.pallas_call(
        _ada_kernel,
        grid=(depth, six_d // tn),
        in_specs=[
            _const_spec((rows, d)),
            pl.BlockSpec((1, d, tn), lambda i, j: (i, 0, j)),
            pl.BlockSpec((1, 1, tn), lambda i, j: (i, 0, j)),
        ],
        out_specs=pl.BlockSpec((1, rows, tn), lambda i, j: (i, 0, j)),
        out_shape=jax.ShapeDtypeStruct((depth, rows, six_d), F32),
        compiler_params=_cparams(2),
        name="ada",
    )(cc, ada_w, ada_b.reshape(depth, 1, six_d))


def _mod_spec(n_lat_tiles, d):
    return pl.BlockSpec((1, 1, 6, d), lambda b, i: (b, jnp.minimum(i // n_lat_tiles, 1), 0, 0))


def _stream_operands(xs, tm, n_lat):
    if not isinstance(xs, tuple):
        return [xs], [pl.BlockSpec((1, tm, xs.shape[-1]), lambda bb, i: (bb, i, 0))]
    lat, ctx = xs
    last = n_lat // tm - 1
    return [lat, ctx], [pl.BlockSpec((1, tm, lat.shape[-1]), lambda bb, i: (bb, jnp.minimum(i, last), 0)),
                        pl.BlockSpec((1, tm, ctx.shape[-1]), lambda bb, i: (bb, 0, 0))]


def _stream_tile(x_refs, n_lat):
    if len(x_refs) == 1:
        return x_refs[0][0]
    is_lat = pl.program_id(1) * x_refs[0].shape[1] < n_lat
    return jnp.where(is_lat, x_refs[0][0], x_refs[1][0])


def _mla_in_kernel(*refs, n_x, n_lat):
    (g_ref, mod_ref, win_ref, qng_ref, wuq_ref, wuqp_ref, kvng_ref, wuk_ref, wuv_ref,
     aq_ref, bq_ref, ak_ref, bk_ref, q_ref, k_ref, v_ref) = refs[n_x:]
    mod = mod_ref[0, 0]
    h = _norm_mod(_stream_tile(refs[:n_x], n_lat), g_ref[...], mod[0:1], mod[1:2])
    down = _dot(h.astype(BF16), win_ref[...])
    cq = (_rms(down[:, :MLA_Q_LORA], MLA_Q_LORA) * qng_ref[...]).astype(BF16)
    ckv = (_rms(down[:, MLA_Q_LORA:MLA_Q_LORA + MLA_KV_LORA], MLA_KV_LORA) * kvng_ref[...]).astype(BF16)
    q = _dot(cq, wuq_ref[...])
    qp = _dot(cq, wuqp_ref[...])
    kn = _dot(ckv, wuk_ref[...])
    v = _dot(ckv, wuv_ref[...])
    is_v = (lax.broadcasted_iota(jnp.int32, v.shape, 1) & (LANE - 1)) < MLA_V
    v_ref[0] = jnp.where(is_v, v, 1.0).astype(BF16)
    lat = MLA_Q_LORA + MLA_KV_LORA
    kr, krp = down[:, lat:lat + LANE], down[:, lat + LANE:]
    aq, bq, ak, bk = aq_ref[...], bq_ref[...], ak_ref[...], bk_ref[...]
    krot = krp * bk
    inv_n = 1.0 / MLA_QK
    for hd in range(MLA_HEADS):
        sl = slice(hd * LANE, (hd + 1) * LANE)
        qh = q[:, sl]
        rq = lax.rsqrt(jnp.sum(qh * qh, axis=-1, keepdims=True) * inv_n + EPS)
        group = qp[:, hd // 4 * LANE:(hd // 4 + 1) * LANE]
        shift = (MLA_NOPE - MLA_ROPE * (hd % 4)) % LANE
        qph = pltpu.roll(group, shift, 1) if shift else group
        q_ref[0, :, sl] = ((qh * aq + qph * bq) * rq).astype(BF16)
        kh = kn[:, sl] + kr
        rk = lax.rsqrt(jnp.sum(kh * kh, axis=-1, keepdims=True) * inv_n + EPS)
        k_ref[0, :, sl] = ((kh * ak + krot) * rk).astype(BF16)


def _mla_in(xs, g, mod, w, tabs, n_lat, t):
    tm = ROW_TILE
    x_ins, x_specs = _stream_operands(xs, tm, n_lat)
    b, _, d = x_ins[0].shape
    hw = MLA_HEADS * LANE
    row = lambda bb, i: (bb, i, 0)
    tab = pl.BlockSpec((tm, LANE), lambda bb, i: (i, 0))
    consts = [w["win"], w["qng"], w["wuq"], w["wuqp"], w["kvng"], w["wuk"], w["wuv"]]
    return pl.pallas_call(
        functools.partial(_mla_in_kernel, n_x=len(x_ins), n_lat=n_lat),
        grid=(b, t // tm),
        in_specs=x_specs + [_const_spec((1, d)), _mod_spec(n_lat // tm, d)]
        + [_const_spec(a.shape) for a in consts] + [tab] * 4,
        out_specs=[pl.BlockSpec((1, tm, hw), row)] * 3,
        out_shape=[jax.ShapeDtypeStruct((b, t, hw), BF16)] * 3,
        compiler_params=_cparams(2),
        name="mla_in",
    )(*x_ins, g, mod, *consts, *tabs)


def _diff_in_kernel(x_ref, g_ref, mod_ref, win_ref, aq_ref, bq_ref, ak_ref, bk_ref, q0_ref, q1_ref, k_ref, v_ref):
    mod = mod_ref[0, 0]
    d = x_ref.shape[-1]
    h = _norm_mod(x_ref[0], g_ref[...], mod[0:1], mod[1:2])
    hb = h.astype(BF16)
    part = {name: _dot(hb, win_ref[:, c * d:(c + 1) * d]) for name, c in (("q", 0), ("qp", 3), ("k", 1), ("kp", 4), ("v", 2))}
    ones = jnp.ones((x_ref.shape[1], LANE), BF16)
    lo = lax.broadcasted_iota(jnp.int32, (x_ref.shape[1], LANE), 1) < DIFF_HEAD_DIM
    inv_n = 1.0 / DIFF_HEAD_DIM

    def normed_rope(name, hd, a, b):
        xh = part[name][:, hd * LANE:(hd + 1) * LANE]
        xp = part[name + "p"][:, hd * LANE:(hd + 1) * LANE]
        sq = xh * xh
        s_lo = jnp.sum(jnp.where(lo, sq, 0.0), axis=-1, keepdims=True)
        s_hi = jnp.sum(jnp.where(lo, 0.0, sq), axis=-1, keepdims=True)
        return (xh * a + xp * b) * lax.rsqrt(jnp.where(lo, s_lo, s_hi) * inv_n + EPS)

    aq, bq, ak, bk = aq_ref[...], bq_ref[...], ak_ref[...], bk_ref[...]
    for hd in range(DIFF_HEADS):
        sl = slice(hd * LANE, (hd + 1) * LANE)
        qh = normed_rope("q", hd, aq, bq)
        q0_ref[0, :, sl] = jnp.where(lo, qh, 0.0).astype(BF16)
        q1_ref[0, :, sl] = jnp.where(lo, 0.0, qh).astype(BF16)
    for hd in range(DIFF_HEADS):
        sl = slice(hd * LANE, (hd + 1) * LANE)
        k_ref[0, :, sl] = normed_rope("k", hd, ak, bk).astype(BF16)
    for hd in range(DIFF_HEADS):
        v_ref[0, :, 2 * hd * LANE:(2 * hd + 1) * LANE] = part["v"][:, hd * LANE:(hd + 1) * LANE].astype(BF16)
        v_ref[0, :, (2 * hd + 1) * LANE:(2 * hd + 2) * LANE] = ones


def _diff_in(x, g, mod, win, tabs, n_lat):
    b, t, d = x.shape
    tm = ROW_TILE
    row = lambda bb, i: (bb, i, 0)
    tab = pl.BlockSpec((tm, LANE), lambda bb, i: (i, 0))
    out = jax.ShapeDtypeStruct((b, t, d), BF16)
    return pl.pallas_call(
        _diff_in_kernel,
        grid=(b, t // tm),
        in_specs=[pl.BlockSpec((1, tm, d), row), _const_spec((1, d)), _mod_spec(n_lat // tm, d),
                  _const_spec(win.shape)] + [tab] * 4,
        out_specs=[pl.BlockSpec((1, tm, d), row)] * 3 + [pl.BlockSpec((1, tm, 2 * d), row)],
        out_shape=[out] * 3 + [jax.ShapeDtypeStruct((b, t, 2 * d), BF16)],
        compiler_params=_cparams(2),
        name="diff_in",
    )(x, g, mod, win, *tabs)


def _exp2_scores(q, k):
    s = _dot_nt(q, k)
    return jnp.exp2(s - jnp.max(s, axis=-1, keepdims=True)).astype(BF16)


def _mla_attn_kernel(q_ref, k_ref, v_ref, *rest):
    o_ref = rest[-1]
    tq = q_ref.shape[1]
    low = lax.broadcasted_iota(jnp.int32, (tq, LANE), 1) < MLA_V
    blocks = [slice(hd * LANE, (hd + 1) * LANE) for hd in range(MLA_HEADS_PER_STEP)]
    scores = [_dot_nt(q_ref[0, :, sl], k_ref[0, :, sl]) for sl in blocks]
    weights = [jnp.exp2(s - jnp.max(s, axis=-1, keepdims=True)).astype(BF16) for s in scores]
    for pr in range(MLA_HEADS_PER_STEP // 2):
        halves = []
        for hd in (2 * pr, 2 * pr + 1):
            r = _dot(weights[hd], v_ref[0, :, blocks[hd]])
            halves.append(r / pltpu.roll(r, LANE - MLA_V, 1))
        pair = jnp.where(low, halves[0], pltpu.roll(halves[1], MLA_V, 1))
        o_ref[0, :, pr * LANE:(pr + 1) * LANE] = pair.astype(o_ref.dtype)


def _attn_calls(kernel, name, qs, kvs, consts, n_lat, with_ctx, heads, hs, out_width):
    b, t, _ = kvs[0].shape
    n_ctx = t - n_lat
    ctx_blk = n_lat // n_ctx
    out_shape = jax.ShapeDtypeStruct((b, t, heads // hs * out_width), BF16)
    cspecs = [_const_spec(a.shape) for a in consts]

    def call(tq, q_map, kv_rows, kv_map, n_tiles, prev):
        q_specs = [pl.BlockSpec((1, tq, a.shape[-1] // (heads // hs)), q_map) for a in qs]
        kv_specs = [pl.BlockSpec((1, kv_rows, a.shape[-1] // (heads // hs)), kv_map) for a in kvs]
        extra = [] if prev is None else [pl.BlockSpec(memory_space=pl.ANY)]
        n_in = len(qs) + len(kvs) + len(consts)
        return pl.pallas_call(
            kernel,
            grid=(b, heads // hs, n_tiles),
            in_specs=q_specs + kv_specs + cspecs + extra,
            out_specs=pl.BlockSpec((1, tq, out_width), q_map),
            out_shape=out_shape,
            input_output_aliases={} if prev is None else {n_in: 0},
            compiler_params=_cparams(3),
            name=name,
        )(*qs, *kvs, *consts, *([] if prev is None else [prev]))

    tq = min(ATTN_Q_TILE, n_lat)
    out = call(tq, lambda bb, hg, i: (bb, i, hg), t, lambda bb, hg, i: (bb, 0, hg), n_lat // tq, None)
    if with_ctx:
        ctx_map = lambda bb, hg, i: (bb, ctx_blk, hg)
        out = call(n_ctx, ctx_map, n_ctx, ctx_map, 1, out)
    return out


def _diff_attn_kernel(q0_ref, q1_ref, k_ref, v_ref, lq1_ref, lk1_ref, lq2_ref, lk2_ref, sub_ref, *rest, lam_init):
    o_ref = rest[-1]
    lam = (jnp.exp(jnp.sum(lq1_ref[...] * lk1_ref[...], axis=-1, keepdims=True))
           - jnp.exp(jnp.sum(lq2_ref[...] * lk2_ref[...], axis=-1, keepdims=True)) + lam_init)
    for a in range(DIFF_HEADS_PER_STEP):
        sl = slice(a * LANE, (a + 1) * LANE)
        k, v = k_ref[0, :, sl], v_ref[0, :, 2 * a * LANE:(2 * a + 2) * LANE]
        scores = [_dot_nt(q_ref[0, :, sl], k) for q_ref in (q0_ref, q1_ref)]
        outs = []
        for s in scores:
            r = _dot(jnp.exp2(s - jnp.max(s, axis=-1, keepdims=True)).astype(BF16), v)
            outs.append(r[:, :LANE] / r[:, LANE:])
        o = outs[0] - lam * outs[1]
        o_ref[0, :, sl] = (_rms(o, DIFF_V_DIM) * sub_ref[...] * (1.0 - lam_init)).astype(o_ref.dtype)


def _route(x, mod, gf_ref, rhi_ref, rlo_ref, h_ref, aff_ref):
    h = _norm_mod(x, gf_ref[...], mod[3:4], mod[4:5])
    h_hi = h.astype(BF16)
    h_lo = (h - h_hi.astype(F32)).astype(BF16)
    h_ref[0] = h_hi
    logits = _dot(h_hi, rhi_ref[...]) + (_dot(h_lo, rhi_ref[...]) + _dot(h_hi, rlo_ref[...]))
    e = jnp.exp(logits - jnp.max(logits, axis=-1, keepdims=True))
    aff_ref[0] = e / jnp.sum(e, axis=-1, keepdims=True)


def _route_operands(x, norm_ffn_g, router, n_tiles, tm):
    b, _, d = x.shape
    ne = router.shape[-1]
    row = lambda bb, i: (bb, i, 0)
    r_hi = router.astype(BF16)
    r_lo = (router - r_hi.astype(F32)).astype(BF16)
    ins = [norm_ffn_g, r_hi, r_lo]
    in_specs = [_const_spec(a.shape) for a in ins]
    out_specs = [pl.BlockSpec((1, tm, d), row), pl.BlockSpec((1, tm, ne), row)]
    out_shape = [jax.ShapeDtypeStruct((b, n_tiles * tm, d), BF16), jax.ShapeDtypeStruct((b, n_tiles * tm, ne), F32)]
    return ins, in_specs, out_specs, out_shape


def _out_proj_kernel(*refs, n_x, n_lat):
    a_ref, w_ref, mod_ref, gf_ref, rhi_ref, rlo_ref, o_ref, h_ref, aff_ref = refs[n_x:]
    mod = mod_ref[0, 0]
    x = _stream_tile(refs[:n_x], n_lat) + mod[2:3] * _dot(a_ref[0], w_ref[...])
    o_ref[0] = x
    _route(x, mod, gf_ref, rhi_ref, rlo_ref, h_ref, aff_ref)


def _out_proj(xs, a, w, mod, norm_ffn_g, router, n_lat, n_tiles):
    tm = ROW_TILE
    x_ins, x_specs = _stream_operands(xs, tm, n_lat)
    b, _, d = x_ins[0].shape
    t = a.shape[1]
    row = lambda bb, i: (bb, i, 0)
    r_ins, r_in_specs, r_out_specs, r_out_shape = _route_operands(x_ins[0], norm_ffn_g, router, n_tiles, tm)
    return pl.pallas_call(
        functools.partial(_out_proj_kernel, n_x=len(x_ins), n_lat=n_lat),
        grid=(b, n_tiles),
        in_specs=x_specs + [pl.BlockSpec((1, tm, a.shape[-1]), row), _const_spec(w.shape),
                            _mod_spec(n_lat // tm, d)] + r_in_specs,
        out_specs=[pl.BlockSpec((1, tm, d), row)] + r_out_specs,
        out_shape=[jax.ShapeDtypeStruct((b, t, d), F32)] + r_out_shape,
        input_output_aliases={0: 0} if len(x_ins) == 1 else {},
        compiler_params=_cparams(2),
        name="out_proj",
    )(*x_ins, a, w, mod, *r_ins)


def _sg_kernel(x_ref, g_ref, mod_ref, win_ref, lng_ref, lnb_ref, ws_ref, bs_ref, wout_ref, gf_ref, rhi_ref, rlo_ref,
               o_ref, h_ref, aff_ref, gated_ref):
    mod = mod_ref[0, 0]
    x = x_ref[0]
    tm = x.shape[0]
    width = lng_ref.shape[-1]
    gdim = width // SG_GROUPS
    h = _norm_mod(x, g_ref[...], mod[0:1], mod[1:2])
    hb = h.astype(BF16)
    pre = [_dot(hb, win_ref[:, c * SG_COLUMN_CHUNK:(c + 1) * SG_COLUMN_CHUNK])
           for c in list(range(width // SG_COLUMN_CHUNK, 2 * width // SG_COLUMN_CHUNK)) + list(range(width // SG_COLUMN_CHUNK))]
    z = [_gelu_tanh(p) for p in pre]
    n_half = width // SG_COLUMN_CHUNK
    v = jnp.concatenate(z[:n_half], axis=-1)
    u = jnp.concatenate(z[n_half:], axis=-1)
    mu = jnp.mean(v, axis=-1, keepdims=True)
    vc = v - mu
    var = jnp.mean(vc * vc, axis=-1, keepdims=True)
    vn = (vc * lax.rsqrt(var + EPS) * lng_ref[...] + lnb_ref[...]).astype(BF16)
    for c in range(tm // SG_CHUNK):
        rows = slice(c * SG_CHUNK, (c + 1) * SG_CHUNK)
        for gi in range(SG_GROUPS):
            cols = slice(gi * gdim, (gi + 1) * gdim)
            mixed = _dot(ws_ref[gi], vn[rows, cols]) + bs_ref[:, gi:gi + 1]
            gated_ref[rows, cols] = (u[rows, cols] * mixed).astype(BF16)
    x = x + mod[2:3] * _dot(gated_ref[...], wout_ref[...])
    o_ref[0] = x
    _route(x, mod, gf_ref, rhi_ref, rlo_ref, h_ref, aff_ref)


def _sg(x, g, mod, w, norm_ffn_g, router, n_lat):
    b, t, d = x.shape
    tm = ROW_TILE
    row = lambda bb, i: (bb, i, 0)
    consts = [w["win"], w["lng"], w["lnb"], w["ws"], w["bs"], w["wout"]]
    r_ins, r_in_specs, r_out_specs, r_out_shape = _route_operands(x, norm_ffn_g, router, t // tm, tm)
    return pl.pallas_call(
        _sg_kernel,
        grid=(b, t // tm),
        in_specs=[pl.BlockSpec((1, tm, d), row), _const_spec((1, d)), _mod_spec(n_lat // tm, d)]
        + [_const_spec(a.shape) for a in consts] + r_in_specs,
        out_specs=[pl.BlockSpec((1, tm, d), row)] + r_out_specs,
        out_shape=[jax.ShapeDtypeStruct(x.shape, F32)] + r_out_shape,
        scratch_shapes=[pltpu.VMEM((tm, w["lng"].shape[-1]), BF16)],
        input_output_aliases={0: 0},
        compiler_params=_cparams(2),
        name="chunk_mlp",
    )(x, g, mod, *consts, *r_ins)


def _prefix_counts(mask, tri):
    out = []
    carry = jnp.zeros((mask.shape[0], 1), F32)
    before = [carry]
    for c in range(mask.shape[1] // PREFIX_CHUNK):
        m = mask[:, c * PREFIX_CHUNK:(c + 1) * PREFIX_CHUNK]
        out.append(_dot(m.astype(BF16), tri) + carry)
        carry = carry + jnp.sum(m, axis=-1, keepdims=True)
        before.append(carry)
    return out, before


def _topk_kernel(aff_ref, pos_ref, cnt_ref, *, cap):
    a = aff_ref[0]
    ne = a.shape[0]
    bits = pltpu.bitcast(a, jnp.int32)

    def step(i, lo):
        cand = lo | jnp.left_shift(jnp.int32(1), 30 - i)
        cnt = jnp.sum(jnp.where(bits >= cand, 1.0, 0.0), axis=-1, keepdims=True)
        return jnp.where(cnt >= cap, cand, lo)

    thr = lax.fori_loop(0, 31, step, jnp.zeros((ne, 1), jnp.int32))
    gt = jnp.where(bits > thr, 1.0, 0.0)
    eq = jnp.where(bits == thr, 1.0, 0.0)
    room = cap - jnp.sum(gt, axis=-1, keepdims=True)
    ri = lax.broadcasted_iota(jnp.int32, (PREFIX_CHUNK, PREFIX_CHUNK), 0)
    ci = lax.broadcasted_iota(jnp.int32, (PREFIX_CHUNK, PREFIX_CHUNK), 1)
    tri = jnp.where(ri <= ci, 1.0, 0.0).astype(BF16)
    eq_rank, _ = _prefix_counts(eq, tri)
    sel = jnp.concatenate([
        jnp.maximum(gt[:, c * PREFIX_CHUNK:(c + 1) * PREFIX_CHUNK],
                    jnp.where(r <= room, eq[:, c * PREFIX_CHUNK:(c + 1) * PREFIX_CHUNK], 0.0))
        for c, r in enumerate(eq_rank)], axis=-1)
    sel_rank, before = _prefix_counts(sel, tri)
    for c, r in enumerate(sel_rank):
        cols = slice(c * PREFIX_CHUNK, (c + 1) * PREFIX_CHUNK)
        pos_ref[0, :, cols] = jnp.where(sel[:, cols] > 0.0, r - 1.0, -1.0).astype(jnp.int32)
    for c, cnt in enumerate(before):
        cnt_ref[0, :, c:c + 1] = cnt.astype(jnp.int32)


def _topk(aff_t, cap):
    b, ne, n = aff_t.shape
    spec = pl.BlockSpec((1, ne, n), lambda bb: (bb, 0, 0))
    n_cnt = n // PREFIX_CHUNK + 1
    return pl.pallas_call(
        functools.partial(_topk_kernel, cap=cap),
        grid=(b,),
        in_specs=[spec],
        out_specs=[spec, pl.BlockSpec((1, ne, n_cnt), lambda bb: (bb, 0, 0))],
        out_shape=[jax.ShapeDtypeStruct((b, ne, n), jnp.int32), jax.ShapeDtypeStruct((b, ne, n_cnt), jnp.int32)],
        compiler_params=_cparams(1),
        name="topk",
    )(aff_t)


def _moe_gather_kernel(cnt_ref, h_ref, pos_ref, aff_ref, xg_ref, gate_ref, *, cap, n_cnt):
    i = pl.program_id(1)
    ne, tn = pos_ref.shape[1:]
    win = min(GATHER_WINDOW, cap)
    row = (pl.program_id(0) * ne) * n_cnt + i

    @pl.when(i == 0)
    def _():
        xg_ref[...] = jnp.zeros(xg_ref.shape, xg_ref.dtype)
        gate_ref[...] = jnp.zeros(gate_ref.shape, gate_ref.dtype)

    h = h_ref[0]
    slot = lax.broadcasted_iota(jnp.int32, (win, tn), 0)

    def add_window(e, start, hit, res):
        rows = pl.ds(start, win)
        xg_ref[0, e, rows, :] = (xg_ref[0, e, rows, :].astype(F32) + res).astype(BF16)
        gate_ref[0, e, rows, :] += jnp.sum(jnp.where(hit, aff_ref[0, e:e + 1, :], 0.0), axis=-1, keepdims=True)

    starts, his, hits = [], [], []
    for e in range(ne):
        lo, hi = cnt_ref[row + e * n_cnt], cnt_ref[row + e * n_cnt + 1]
        start = pl.multiple_of(jnp.minimum((lo // BF16_ROWS) * BF16_ROWS, cap - win), BF16_ROWS)
        starts.append(start)
        his.append(hi)
        hits.append(pos_ref[0, e:e + 1, :] - start == slot)
    onehot = jnp.concatenate([jnp.where(hit, 1.0, 0.0).astype(BF16) for hit in hits], axis=0)
    res = _dot(onehot, h)
    for e in range(ne):
        add_window(e, starts[e], hits[e], res[e * win:(e + 1) * win])

    for e in range(ne):
        first = starts[e] + win
        n_more = jnp.maximum(his[e] - first + (win - 1), 0) // win

        def more(k, carry, e=e, first=first):
            lower = first + k * win
            start = pl.multiple_of(jnp.minimum(lower, cap - win), BF16_ROWS)
            p = pos_ref[0, e:e + 1, :]
            hit = jnp.logical_and(p - start == slot, p >= lower)
            add_window(e, start, hit, _dot(jnp.where(hit, 1.0, 0.0).astype(BF16), h))
            return carry

        lax.fori_loop(0, n_more, more, 0)


def _moe_gather(h, pos, aff_t, cnt, set_block, n_set, cap):
    b, _, d = h.shape
    ne = pos.shape[1]
    tn = PREFIX_CHUNK
    tiles = n_set // tn
    tile = pl.BlockSpec((1, ne, tn), lambda bb, i, c: (bb, 0, i))
    return pl.pallas_call(
        functools.partial(_moe_gather_kernel, cap=cap, n_cnt=cnt.shape[-1]),
        grid_spec=pltpu.PrefetchScalarGridSpec(
            num_scalar_prefetch=1,
            grid=(b, tiles),
            in_specs=[pl.BlockSpec((1, tn, d), lambda bb, i, c: (bb, set_block * tiles + i, 0)), tile, tile],
            out_specs=[pl.BlockSpec((1, ne, cap, d), lambda bb, i, c: (bb, 0, 0, 0)),
                       pl.BlockSpec((1, ne, cap, 1), lambda bb, i, c: (bb, 0, 0, 0))]),
        out_shape=[jax.ShapeDtypeStruct((b, ne, cap, d), BF16), jax.ShapeDtypeStruct((b, ne, cap, 1), F32)],
        compiler_params=_cparams(2),
        name="moe_gather",
    )(cnt.reshape(-1), h, pos, aff_t)


def _moe_ffn_kernel(xg_ref, gate_ref, wg32_ref, wu32_ref, wd32_ref, y_ref, wg_ref, wu_ref, wd_ref):
    @pl.when(pl.program_id(1) == 0)
    def _():
        wg_ref[...] = wg32_ref[0, 0].astype(BF16)
        wu_ref[...] = wu32_ref[0, 0].astype(BF16)
        wd_ref[...] = wd32_ref[0, 0].astype(BF16)

    bg, _, cap, _ = xg_ref.shape
    rows = lambda ref: ref[0, 0] if bg == 1 else jnp.concatenate([ref[bb, 0] for bb in range(bg)], axis=0)
    xg = rows(xg_ref)
    chunks = [slice(c * FFN_COLUMN_CHUNK, (c + 1) * FFN_COLUMN_CHUNK) for c in range(wg_ref.shape[1] // FFN_COLUMN_CHUNK)]
    pre = [(_dot(xg, wg_ref[:, c]), _dot(xg, wu_ref[:, c])) for c in chunks]
    hid = [(_silu(g) * u).astype(BF16) for g, u in pre]
    y = _dot(hid[0], wd_ref[chunks[0], :])
    for hc, c in zip(hid[1:], chunks[1:]):
        y = y + _dot(hc, wd_ref[c, :])
    y = y * rows(gate_ref)
    for bb in range(bg):
        y_ref[bb, 0] = y[bb * cap:(bb + 1) * cap].astype(BF16)


def _moe_ffn(xg, gates, wg, wu, wd, layer, bg):
    b, ne, cap, d = xg.shape
    ff = wg.shape[-1]
    tok = pl.BlockSpec((bg, 1, cap, d), lambda e, i: (i, e, 0, 0))
    return pl.pallas_call(
        _moe_ffn_kernel,
        grid=(ne, b // bg),
        in_specs=[tok, pl.BlockSpec((bg, 1, cap, 1), lambda e, i: (i, e, 0, 0)),
                  pl.BlockSpec((1, 1, d, ff), lambda e, i: (layer, e, 0, 0)),
                  pl.BlockSpec((1, 1, d, ff), lambda e, i: (layer, e, 0, 0)),
                  pl.BlockSpec((1, 1, ff, d), lambda e, i: (layer, e, 0, 0))],
        out_specs=tok,
        out_shape=jax.ShapeDtypeStruct((b, ne, cap, d), BF16),
        scratch_shapes=[pltpu.VMEM((d, ff), BF16), pltpu.VMEM((d, ff), BF16), pltpu.VMEM((ff, d), BF16)],
        compiler_params=_cparams(2),
        name="moe_ffn",
    )(xg, gates, wg, wu, wd)


def _moe_combine_kernel(cnt_ref, x_ref, pos_ref, y_ref, mod_ref, o_ref, *, cap, n_cnt):
    pos = pos_ref[0]
    tn, ne = pos.shape
    win = min(GATHER_WINDOW, cap)
    per = COMBINE_EXPERTS_PER_MATMUL
    gate = mod_ref[0, 0][5:6]
    row = (pl.program_id(0) * ne) * n_cnt + pl.program_id(1)
    lane = lax.broadcasted_iota(jnp.int32, (tn, per * win), 1)
    acc = jnp.zeros(x_ref.shape[1:], F32)
    overflow = False
    for g in range(ne // per):
        onehot = jnp.zeros((tn, per * win), F32)
        parts = []
        for j in range(per):
            e = g * per + j
            lo, hi = cnt_ref[row + e * n_cnt], cnt_ref[row + e * n_cnt + 1]
            start = pl.multiple_of(jnp.minimum((lo // BF16_ROWS) * BF16_ROWS, cap - win), BF16_ROWS)
            overflow = jnp.logical_or(overflow, hi > start + win)
            rel = pos[:, e:e + 1] - start
            target = jnp.where(jnp.logical_and(rel >= 0, rel < win), rel + j * win, -1)
            onehot = jnp.where(target == lane, 1.0, onehot)
            parts.append(y_ref[0, e, pl.ds(start, win), :])
        acc = acc + _dot(onehot.astype(BF16), jnp.concatenate(parts, axis=0))
    o_ref[0] = x_ref[0] + gate * acc

    @pl.when(overflow)
    def _():
        lane_all = lax.broadcasted_iota(jnp.int32, (tn, cap), 1)
        full = jnp.zeros(x_ref.shape[1:], F32)
        for e in range(ne):
            onehot = jnp.where(pos[:, e:e + 1] == lane_all, 1.0, 0.0).astype(BF16)
            full = full + _dot(onehot, y_ref[0, e])
        o_ref[0] = x_ref[0] + gate * full


def _moe_combine(x, pos_t, cnt, y, mod, tile_off, n_set, mod_sel, out_rows):
    b, t, d = x.shape
    _, ne, cap, _ = y.shape
    tn = PREFIX_CHUNK
    n_cnt = cnt.shape[-1]
    alias = out_rows == t
    out_off = tile_off if alias else 0
    return pl.pallas_call(
        functools.partial(_moe_combine_kernel, cap=cap, n_cnt=n_cnt),
        grid_spec=pltpu.PrefetchScalarGridSpec(
            num_scalar_prefetch=1,
            grid=(b, n_set // tn),
            in_specs=[pl.BlockSpec((1, tn, d), lambda bb, i, c: (bb, i + tile_off, 0)),
                      pl.BlockSpec((1, tn, ne), lambda bb, i, c: (bb, i, 0)),
                      pl.BlockSpec((1, ne, cap, d), lambda bb, i, c: (bb, 0, 0, 0)),
                      pl.BlockSpec((1, 1, 6, d), lambda bb, i, c: (bb, mod_sel, 0, 0))],
            out_specs=pl.BlockSpec((1, tn, d), lambda bb, i, c: (bb, i + out_off, 0))),
        out_shape=jax.ShapeDtypeStruct((b, out_rows, d), F32),
        input_output_aliases={1: 0} if alias else {},
        compiler_params=_cparams(2),
        name="moe_combine",
    )(cnt.reshape(-1), x, pos_t, y, mod)


def _moe(x, h, aff, mod, wg, wu, wd, layer, n_lat, with_ctx, final):
    b, t, d = x.shape
    n_ctx = t - n_lat
    aff_t = jnp.swapaxes(aff, 1, 2)
    sets = [(0, n_lat, 0, 1)]
    if with_ctx:
        sets.append((n_lat, n_ctx, 1, b))
    for start, n_set, mod_sel, bg in sets:
        cap = CAPACITY_FACTOR * n_set // N_EXPERTS
        a_set = aff_t[:, :, start:start + n_set]
        pos, cnt = _topk(a_set, cap)
        xg, gates = _moe_gather(h, pos, a_set, cnt, start // n_set, n_set, cap)
        y = _moe_ffn(xg, gates, wg, wu, wd, layer, bg)
        x = _moe_combine(x, jnp.swapaxes(pos, 1, 2), cnt, y, mod, start // PREFIX_CHUNK, n_set, mod_sel,
                         n_lat if final else t)
    return x


def _rope_tables(n_lat, n_ctx, rot_dim, starts):
    n_rows = n_lat // GRID_W
    rows = np.repeat(np.arange(n_rows, dtype=np.float32), GRID_W)
    cols = np.tile(np.arange(GRID_W, dtype=np.float32), n_rows)
    n_freq = rot_dim // 4
    inv_freq = (np.float32(ROPE_BASE) ** (-np.arange(n_freq, dtype=np.float32) / np.float32(n_freq))).astype(np.float32)
    ang = np.concatenate([rows[:, None] * inv_freq, cols[:, None] * inv_freq], axis=-1)
    half = rot_dim // 2
    cos_l, sin_l = np.cos(ang).astype(np.float32), np.sin(ang).astype(np.float32)
    cos = np.ones((n_lat + n_ctx, LANE), np.float32)
    sa = np.zeros((n_lat + n_ctx, LANE), np.float32)
    sb = np.zeros((n_lat + n_ctx, LANE), np.float32)
    for s in starts:
        cos[:n_lat, s:s + half] = cos_l
        cos[:n_lat, s + half:s + rot_dim] = cos_l
        sa[:n_lat, s:s + half] = -sin_l
        sb[:n_lat, s + half:s + rot_dim] = sin_l
    return jnp.asarray(cos), jnp.asarray(sa), jnp.asarray(sb)


def _pad_heads(w, heads, width):
    k = w.shape[0]
    w = w.reshape(k, heads, width)
    return jnp.pad(w, ((0, 0), (0, 0), (0, LANE - width))).reshape(k, heads * LANE)


def _mla_partner(a):
    half = MLA_ROPE // 2
    lane = jnp.arange(LANE)
    first = (lane >= MLA_NOPE) & (lane < MLA_NOPE + half)
    second = (lane >= MLA_NOPE + half) & (lane < MLA_QK)
    return jnp.where(first, jnp.roll(a, -half, axis=-1), jnp.where(second, jnp.roll(a, half, axis=-1), 0.0))


def _mla_weights(w_in, q_norm_g, w_uq, kv_norm_g, w_ukv):
    d = w_in.shape[0]
    lat = MLA_Q_LORA + MLA_KV_LORA
    rope_block = jnp.concatenate([jnp.zeros((d, MLA_NOPE), F32), w_in[:, lat:],
                                  jnp.zeros((d, LANE - MLA_QK), F32)], axis=1)
    ukv = w_ukv.reshape(MLA_KV_LORA, MLA_HEADS, MLA_NOPE + MLA_V)
    wuq = _pad_heads(w_uq, MLA_HEADS, MLA_QK)
    wuqp = _mla_partner(wuq.reshape(-1, MLA_HEADS, LANE))[:, :, MLA_NOPE:MLA_QK].reshape(wuq.shape[0], -1)
    return {
        "win": jnp.concatenate([w_in[:, :lat], rope_block, _mla_partner(rope_block)], axis=1).astype(BF16),
        "qng": q_norm_g.reshape(1, -1),
        "wuq": wuq.astype(BF16),
        "wuqp": wuqp.astype(BF16),
        "kvng": kv_norm_g.reshape(1, -1),
        "wuk": _pad_heads(ukv[:, :, :MLA_NOPE].reshape(MLA_KV_LORA, -1), MLA_HEADS, MLA_NOPE).astype(BF16),
        "wuv": _pad_heads(ukv[:, :, MLA_NOPE:].reshape(MLA_KV_LORA, -1), MLA_HEADS, MLA_V).astype(BF16),
    }


def _diff_partner(a):
    half = DIFF_HEAD_DIM // 2
    shape = a.shape
    a = a.reshape(shape[:-1] + (shape[-1] // LANE, LANE))
    first = (jnp.arange(LANE) % DIFF_HEAD_DIM) < half
    return jnp.where(first, jnp.roll(a, -half, axis=-1), jnp.roll(a, half, axis=-1)).reshape(shape)


def _diff_weights(w_in):
    d = w_in.shape[0]
    return jnp.concatenate([w_in, _diff_partner(w_in[:, :2 * d])], axis=1).astype(BF16)


def _diff_tables(tabs, qn_g, kn_g):
    cos, sa, sb = tabs
    out = []
    for g, scale in ((qn_g, DIFF_HEAD_DIM ** -0.5 * LOG2E), (kn_g, 1.0)):
        g2 = jnp.tile(g, 2)
        out += [cos * g2 * scale, (sa + sb) * _diff_partner(g2) * scale]
    return out


def _mla_tables(tabs, qn_g, kn_g):
    cos, sa, sb = tabs
    out = []
    for g, scale in ((qn_g, MLA_QK ** -0.5 * LOG2E), (kn_g, 1.0)):
        gp = jnp.pad(g, (0, LANE - MLA_QK))
        out += [cos * gp * scale, (sa + sb) * _mla_partner(gp) * scale]
    return out


def kernel(x, c, ctx, c_ctx, ada_w, ada_b, norm_mix_g, norm_ffn_g, mla_w_in, mla_q_norm_g, mla_w_uq, mla_kv_norm_g, mla_w_ukv, mla_qn_g, mla_kn_g, mla_w_out, diff_w_in, diff_qn_g, diff_kn_g, diff_lambda_q1, diff_lambda_k1, diff_lambda_q2, diff_lambda_k2, diff_sub_g, diff_w_out, sg_w_in, sg_ln_g, sg_ln_b, sg_w_s, sg_b_s, sg_w_out, moe_router, moe_w_gate, moe_w_up, moe_w_down):
    b, n_lat, d = x.shape
    n_ctx = ctx.shape[1]
    depth = ada_w.shape[0]
    assert n_lat % ROW_TILE == 0 and n_ctx == ROW_TILE and n_lat % n_ctx == 0

    rows = -(-(b + 1) // 8) * 8
    cc = jnp.concatenate([c, c_ctx[None], jnp.zeros((rows - b - 1, d), F32)], axis=0)
    mods = _ada(cc, ada_w, ada_b).reshape(depth, rows, 6, d)
    xs = (x, ctx) if depth > 1 else jnp.concatenate([x, ctx], axis=1)
    t = n_lat + n_ctx

    tabs_a = _rope_tables(n_lat, n_ctx, MLA_ROPE, (MLA_NOPE,))
    tabs_b = _rope_tables(n_lat, n_ctx, DIFF_HEAD_DIM, (0, DIFF_HEAD_DIM))

    for i in range(depth):
        kind, j = i % N_MIXERS, i // N_MIXERS
        last = i == depth - 1
        with_ctx = not last
        q_tiles = (n_lat + (n_ctx if with_ctx else 0)) // ROW_TILE
        mod = jnp.stack([mods[i, :b], jnp.broadcast_to(mods[i, b], (b, 6, d))], axis=1)
        g_mix, g_ffn = norm_mix_g[i].reshape(1, d), norm_ffn_g[i].reshape(1, d)
        if kind == 0:
            w = _mla_weights(mla_w_in[j], mla_q_norm_g[j], mla_w_uq[j], mla_kv_norm_g[j], mla_w_ukv[j])
            q, k, v = _mla_in(xs, g_mix, mod, w, _mla_tables(tabs_a, mla_qn_g[j], mla_kn_g[j]), n_lat, t)
            a = _attn_calls(_mla_attn_kernel, "mla_attn", [q], [k, v], [], n_lat, with_ctx,
                            MLA_HEADS, MLA_HEADS_PER_STEP, MLA_HEADS_PER_STEP * MLA_V)
            xs, h, aff = _out_proj(xs, a, mla_w_out[j].astype(BF16), mod, g_ffn, moe_router[i], n_lat, q_tiles)
        elif kind == 1:
            lam_init = 0.8 - 0.6 * math.exp(-0.3 * i)
            q0, q1, k, v = _diff_in(xs, g_mix, mod, _diff_weights(diff_w_in[j]),
                                    _diff_tables(tabs_b, diff_qn_g[j], diff_kn_g[j]), n_lat)
            wa = [a.reshape(1, -1) for a in (diff_lambda_q1[j], diff_lambda_k1[j], diff_lambda_q2[j],
                                             diff_lambda_k2[j], diff_sub_g[j])]
            a = _attn_calls(functools.partial(_diff_attn_kernel, lam_init=lam_init), "diff_attn", [q0, q1], [k, v],
                            wa, n_lat, with_ctx, DIFF_HEADS, DIFF_HEADS_PER_STEP, DIFF_HEADS_PER_STEP * LANE)
            xs, h, aff = _out_proj(xs, a, diff_w_out[j].astype(BF16), mod, g_ffn, moe_router[i], n_lat, q_tiles)
        else:
            w = {"win": sg_w_in[j].astype(BF16), "lng": sg_ln_g[j].reshape(1, -1), "lnb": sg_ln_b[j].reshape(1, -1),
                 "ws": sg_w_s[j].astype(BF16), "bs": sg_b_s[j].T, "wout": sg_w_out[j].astype(BF16)}
            xs, h, aff = _sg(xs, g_mix, mod, w, g_ffn, moe_router[i], n_lat)
        xs = _moe(xs, h, aff, mod, moe_w_gate, moe_w_up, moe_w_down, i, n_lat, with_ctx, last)
    return xs
```

```python
import functools
import math

import jax
import jax.numpy as jnp
import numpy as np
from jax import lax
from jax.experimental import pallas as pl
from jax.experimental.pallas import tpu as pltpu

F32 = jnp.float32
BF16 = jnp.bfloat16

GRID_W = 64
ROPE_BASE = 10000.0
EPS = 1e-6
N_MIXERS = 3

MLA_HEADS = 16
MLA_Q_LORA = 512
MLA_KV_LORA = 256
MLA_NOPE = 64
MLA_ROPE = 32
MLA_V = 64
MLA_QK = MLA_NOPE + MLA_ROPE

DIFF_HEADS = 8
DIFF_HEAD_DIM = 64
DIFF_V_DIM = 2 * DIFF_HEAD_DIM

SG_CHUNK = 128
SG_GROUPS = 8

N_EXPERTS = 16
CAPACITY_FACTOR = 2

LANE = 128
BF16_ROWS = 16
ROW_TILE = 256
SG_COLUMN_CHUNK = 1024
FFN_COLUMN_CHUNK = 256
ADA_COLUMN_TILE = 1536
PREFIX_CHUNK = 256
GATHER_WINDOW = 64
COMBINE_EXPERTS_PER_MATMUL = 4
MLA_HEADS_PER_STEP = 4
DIFF_HEADS_PER_STEP = 4
ATTN_Q_TILE = 256
LOG2E = math.log2(math.e)
VMEM_LIMIT = 56 * 1024 * 1024


def _cparams(n_axes):
    return pltpu.CompilerParams(dimension_semantics=("arbitrary",) * n_axes, vmem_limit_bytes=VMEM_LIMIT)


def _const_spec(shape):
    nd = len(shape)
    return pl.BlockSpec(shape, lambda *_: (0,) * nd)


def _dot(a, b):
    return jnp.dot(a, b, preferred_element_type=F32)


def _dot_nt(a, b):
    return lax.dot_general(a, b, (((1,), (1,)), ((), ())), preferred_element_type=F32)


def _rms(x, n):
    return x * lax.rsqrt(jnp.sum(x * x, axis=-1, keepdims=True) * (1.0 / n) + EPS)


def _norm_mod(x, g, shift, scale):
    return _rms(x, x.shape[-1]) * g * (1.0 + scale) + shift


def _silu(x):
    return x / (1.0 + jnp.exp(-x))


def _gelu_tanh(x):
    c = math.sqrt(2.0 / math.pi)
    return 0.5 * x * (1.0 + jnp.tanh(c * (x + 0.044715 * (x * x * x))))


def _ada_kernel(c_ref, w_ref, b_ref, o_ref):
    s = _silu(c_ref[...])
    o_ref[0] = jnp.dot(s, w_ref[0], preferred_element_type=F32, precision=lax.Precision.HIGHEST) + b_ref[0]


def _ada(cc, ada_w, ada_b):
    depth, d, six_d = ada_w.shape
    rows = cc.shape[0]
    tn = ADA_COLUMN_TILE
    return pl.pallas_call(
        _ada_kernel,
        grid=(depth, six_d // tn),
        in_specs=[
            _const_spec((rows, d)),
            pl.BlockSpec((1, d, tn), lambda i, j: (i, 0, j)),
            pl.BlockSpec((1, 1, tn), lambda i, j: (i, 0, j)),
        ],
        out_specs=pl.BlockSpec((1, rows, tn), lambda i, j: (i, 0, j)),
        out_shape=jax.ShapeDtypeStruct((depth, rows, six_d), F32),
        compiler_params=_cparams(2),
        name="ada",
    )(cc, ada_w, ada_b.reshape(depth, 1, six_d))


def _mod_spec(n_lat_tiles, d):
    return pl.BlockSpec((1, 1, 6, d), lambda b, i: (b, jnp.minimum(i // n_lat_tiles, 1), 0, 0))


def _stream_operands(xs, tm, n_lat):
    if not isinstance(xs, tuple):
        return [xs], [pl.BlockSpec((1, tm, xs.shape[-1]), lambda bb, i: (bb, i, 0))]
    lat, ctx = xs
    last = n_lat // tm - 1
    return [lat, ctx], [pl.BlockSpec((1, tm, lat.shape[-1]), lambda bb, i: (bb, jnp.minimum(i, last), 0)),
                        pl.BlockSpec((1, tm, ctx.shape[-1]), lambda bb, i: (bb, 0, 0))]


def _stream_tile(x_refs, n_lat):
    if len(x_refs) == 1:
        return x_refs[0][0]
    is_lat = pl.program_id(1) * x_refs[0].shape[1] < n_lat
    return jnp.where(is_lat, x_refs[0][0], x_refs[1][0])


def _mla_in_kernel(*refs, n_x, n_lat):
    (g_ref, mod_ref, win_ref, qng_ref, wuq_ref, wuqp_ref, kvng_ref, wuk_ref, wuv_ref,
     aq_ref, bq_ref, ak_ref, bk_ref, q_ref, k_ref, v_ref) = refs[n_x:]
    mod = mod_ref[0, 0]
    h = _norm_mod(_stream_tile(refs[:n_x], n_lat), g_ref[...], mod[0:1], mod[1:2])
    down = _dot(h.astype(BF16), win_ref[...])
    cq = (_rms(down[:, :MLA_Q_LORA], MLA_Q_LORA) * qng_ref[...]).astype(BF16)
    ckv = (_rms(down[:, MLA_Q_LORA:MLA_Q_LORA + MLA_KV_LORA], MLA_KV_LORA) * kvng_ref[...]).astype(BF16)
    q = _dot(cq, wuq_ref[...])
    qp = _dot(cq, wuqp_ref[...])
    kn = _dot(ckv, wuk_ref[...])
    v = _dot(ckv, wuv_ref[...])
    is_v = (lax.broadcasted_iota(jnp.int32, v.shape, 1) & (LANE - 1)) < MLA_V
    v_ref[0] = jnp.where(is_v, v, 1.0).astype(BF16)
    lat = MLA_Q_LORA + MLA_KV_LORA
    kr, krp = down[:, lat:lat + LANE], down[:, lat + LANE:]
    aq, bq, ak, bk = aq_ref[...], bq_ref[...], ak_ref[...], bk_ref[...]
    krot = krp * bk
    inv_n = 1.0 / MLA_QK
    for hd in range(MLA_HEADS):
        sl = slice(hd * LANE, (hd + 1) * LANE)
        qh = q[:, sl]
        rq = lax.rsqrt(jnp.sum(qh * qh, axis=-1, keepdims=True) * inv_n + EPS)
        group = qp[:, hd // 4 * LANE:(hd // 4 + 1) * LANE]
        shift = (MLA_NOPE - MLA_ROPE * (hd % 4)) % LANE
        qph = pltpu.roll(group, shift, 1) if shift else group
        q_ref[0, :, sl] = ((qh * aq + qph * bq) * rq).astype(BF16)
        kh = kn[:, sl] + kr
        rk = lax.rsqrt(jnp.sum(kh * kh, axis=-1, keepdims=True) * inv_n + EPS)
        k_ref[0, :, sl] = ((kh * ak + krot) * rk).astype(BF16)


def _mla_in(xs, g, mod, w, tabs, n_lat, t):
    tm = ROW_TILE
    x_ins, x_specs = _stream_operands(xs, tm, n_lat)
    b, _, d = x_ins[0].shape
    hw = MLA_HEADS * LANE
    row = lambda bb, i: (bb, i, 0)
    tab = pl.BlockSpec((tm, LANE), lambda bb, i: (i, 0))
    consts = [w["win"], w["qng"], w["wuq"], w["wuqp"], w["kvng"], w["wuk"], w["wuv"]]
    return pl.pallas_call(
        functools.partial(_mla_in_kernel, n_x=len(x_ins), n_lat=n_lat),
        grid=(b, t // tm),
        in_specs=x_specs + [_const_spec((1, d)), _mod_spec(n_lat // tm, d)]
        + [_const_spec(a.shape) for a in consts] + [tab] * 4,
        out_specs=[pl.BlockSpec((1, tm, hw), row)] * 3,
        out_shape=[jax.ShapeDtypeStruct((b, t, hw), BF16)] * 3,
        compiler_params=_cparams(2),
        name="mla_in",
    )(*x_ins, g, mod, *consts, *tabs)


def _diff_in_kernel(x_ref, g_ref, mod_ref, win_ref, aq_ref, bq_ref, ak_ref, bk_ref, q0_ref, q1_ref, k_ref, v_ref):
    mod = mod_ref[0, 0]
    d = x_ref.shape[-1]
    h = _norm_mod(x_ref[0], g_ref[...], mod[0:1], mod[1:2])
    hb = h.astype(BF16)
    part = {name: _dot(hb, win_ref[:, c * d:(c + 1) * d]) for name, c in (("q", 0), ("qp", 3), ("k", 1), ("kp", 4), ("v", 2))}
    ones = jnp.ones((x_ref.shape[1], LANE), BF16)
    lo = lax.broadcasted_iota(jnp.int32, (x_ref.shape[1], LANE), 1) < DIFF_HEAD_DIM
    inv_n = 1.0 / DIFF_HEAD_DIM

    def normed_rope(name, hd, a, b):
        xh = part[name][:, hd * LANE:(hd + 1) * LANE]
        xp = part[name + "p"][:, hd * LANE:(hd + 1) * LANE]
        sq = xh * xh
        s_lo = jnp.sum(jnp.where(lo, sq, 0.0), axis=-1, keepdims=True)
        s_hi = jnp.sum(jnp.where(lo, 0.0, sq), axis=-1, keepdims=True)
        return (xh * a + xp * b) * lax.rsqrt(jnp.where(lo, s_lo, s_hi) * inv_n + EPS)

    aq, bq, ak, bk = aq_ref[...], bq_ref[...], ak_ref[...], bk_ref[...]
    for hd in range(DIFF_HEADS):
        sl = slice(hd * LANE, (hd + 1) * LANE)
        qh = normed_rope("q", hd, aq, bq)
        q0_ref[0, :, sl] = jnp.where(lo, qh, 0.0).astype(BF16)
        q1_ref[0, :, sl] = jnp.where(lo, 0.0, qh).astype(BF16)
    for hd in range(DIFF_HEADS):
        sl = slice(hd * LANE, (hd + 1) * LANE)
        k_ref[0, :, sl] = normed_rope("k", hd, ak, bk).astype(BF16)
    for hd in range(DIFF_HEADS):
        v_ref[0, :, 2 * hd * LANE:(2 * hd + 1) * LANE] = part["v"][:, hd * LANE:(hd + 1) * LANE].astype(BF16)
        v_ref[0, :, (2 * hd + 1) * LANE:(2 * hd + 2) * LANE] = ones


def _diff_in(x, g, mod, win, tabs, n_lat):
    b, t, d = x.shape
    tm = ROW_TILE
    row = lambda bb, i: (bb, i, 0)
    tab = pl.BlockSpec((tm, LANE), lambda bb, i: (i, 0))
    out = jax.ShapeDtypeStruct((b, t, d), BF16)
    return pl.pallas_call(
        _diff_in_kernel,
        grid=(b, t // tm),
        in_specs=[pl.BlockSpec((1, tm, d), row), _const_spec((1, d)), _mod_spec(n_lat // tm, d),
                  _const_spec(win.shape)] + [tab] * 4,
        out_specs=[pl.BlockSpec((1, tm, d), row)] * 3 + [pl.BlockSpec((1, tm, 2 * d), row)],
        out_shape=[out] * 3 + [jax.ShapeDtypeStruct((b, t, 2 * d), BF16)],
        compiler_params=_cparams(2),
        name="diff_in",
    )(x, g, mod, win, *tabs)


def _mla_attn_kernel(q_ref, k_ref, v_ref, *rest):
    o_ref = rest[-1]
    tq = q_ref.shape[1]
    low = lax.broadcasted_iota(jnp.int32, (tq, LANE), 1) < MLA_V
    blocks = [slice(hd * LANE, (hd + 1) * LANE) for hd in range(MLA_HEADS_PER_STEP)]
    scores = [_dot_nt(q_ref[0, :, sl], k_ref[0, :, sl]) for sl in blocks]
    weights = [jnp.exp2(s - jnp.max(s, axis=-1, keepdims=True)).astype(BF16) for s in scores]
    for pr in range(MLA_HEADS_PER_STEP // 2):
        halves = []
        for hd in (2 * pr, 2 * pr + 1):
            r = _dot(weights[hd], v_ref[0, :, blocks[hd]])
            halves.append(r / pltpu.roll(r, LANE - MLA_V, 1))
        pair = jnp.where(low, halves[0], pltpu.roll(halves[1], MLA_V, 1))
        o_ref[0, :, pr * LANE:(pr + 1) * LANE] = pair.astype(o_ref.dtype)


def _attn_calls(kernel, name, qs, kvs, consts, n_lat, with_ctx, heads, hs, out_width):
    b, t, _ = kvs[0].shape
    n_ctx = t - n_lat
    ctx_blk = n_lat // n_ctx
    out_shape = jax.ShapeDtypeStruct((b, t, heads // hs * out_width), BF16)
    cspecs = [_const_spec(a.shape) for a in consts]

    def call(tq, q_map, kv_rows, kv_map, n_tiles, prev):
        q_specs = [pl.BlockSpec((1, tq, a.shape[-1] // (heads // hs)), q_map) for a in qs]
        kv_specs = [pl.BlockSpec((1, kv_rows, a.shape[-1] // (heads // hs)), kv_map) for a in kvs]
        extra = [] if prev is None else [pl.BlockSpec(memory_space=pl.ANY)]
        n_in = len(qs) + len(kvs) + len(consts)
        return pl.pallas_call(
            kernel,
            grid=(b, heads // hs, n_tiles),
            in_specs=q_specs + kv_specs + cspecs + extra,
            out_specs=pl.BlockSpec((1, tq, out_width), q_map),
            out_shape=out_shape,
            input_output_aliases={} if prev is None else {n_in: 0},
            compiler_params=_cparams(3),
            name=name,
        )(*qs, *kvs, *consts, *([] if prev is None else [prev]))

    tq = min(ATTN_Q_TILE, n_lat)
    out = call(tq, lambda bb, hg, i: (bb, i, hg), t, lambda bb, hg, i: (bb, 0, hg), n_lat // tq, None)
    if with_ctx:
        ctx_map = lambda bb, hg, i: (bb, ctx_blk, hg)
        out = call(n_ctx, ctx_map, n_ctx, ctx_map, 1, out)
    return out


def _diff_attn_kernel(q0_ref, q1_ref, k_ref, v_ref, lq1_ref, lk1_ref, lq2_ref, lk2_ref, sub_ref, *rest, lam_init):
    o_ref = rest[-1]
    lam = (jnp.exp(jnp.sum(lq1_ref[...] * lk1_ref[...], axis=-1, keepdims=True))
           - jnp.exp(jnp.sum(lq2_ref[...] * lk2_ref[...], axis=-1, keepdims=True)) + lam_init)
    for a in range(DIFF_HEADS_PER_STEP):
        sl = slice(a * LANE, (a + 1) * LANE)
        k, v = k_ref[0, :, sl], v_ref[0, :, 2 * a * LANE:(2 * a + 2) * LANE]
        scores = [_dot_nt(q_ref[0, :, sl], k) for q_ref in (q0_ref, q1_ref)]
        outs = []
        for s in scores:
            r = _dot(jnp.exp2(s - jnp.max(s, axis=-1, keepdims=True)).astype(BF16), v)
            outs.append(r[:, :LANE] / r[:, LANE:])
        o = outs[0] - lam * outs[1]
        o_ref[0, :, sl] = (_rms(o, DIFF_V_DIM) * sub_ref[...] * (1.0 - lam_init)).astype(o_ref.dtype)


def _route(x, mod, gf_ref, rhi_ref, rlo_ref, h_ref, aff_ref):
    ne = aff_ref.shape[-1]
    h = _norm_mod(x, gf_ref[...], mod[3:4], mod[4:5])
    h_hi = h.astype(BF16)
    h_lo = (h - h_hi.astype(F32)).astype(BF16)
    h_ref[0] = h_hi
    parts = _dot(h_hi, rhi_ref[...]) + _dot(h_lo, rlo_ref[...])
    logits = (parts + pltpu.roll(parts, LANE // 2, 1))[:, :ne]
    e = jnp.exp(logits - jnp.max(logits, axis=-1, keepdims=True))
    aff_ref[0] = e / jnp.sum(e, axis=-1, keepdims=True)


def _route_operands(x, norm_ffn_g, router, n_tiles, tm):
    b, _, d = x.shape
    ne = router.shape[-1]
    assert ne <= LANE // 2
    row = lambda bb, i: (bb, i, 0)
    r_hi = router.astype(BF16)
    r_lo = (router - r_hi.astype(F32)).astype(BF16)
    half = lambda r: jnp.pad(r, ((0, 0), (0, LANE // 2 - ne)))
    ins = [norm_ffn_g, jnp.concatenate([half(r_hi), half(r_lo)], axis=1),
           jnp.concatenate([half(r_hi), jnp.zeros((d, LANE // 2), BF16)], axis=1)]
    in_specs = [_const_spec(a.shape) for a in ins]
    out_specs = [pl.BlockSpec((1, tm, d), row), pl.BlockSpec((1, tm, ne), row)]
    out_shape = [jax.ShapeDtypeStruct((b, n_tiles * tm, d), BF16), jax.ShapeDtypeStruct((b, n_tiles * tm, ne), F32)]
    return ins, in_specs, out_specs, out_shape


def _out_proj_kernel(*refs, n_x, n_lat):
    a_ref, w_ref, mod_ref, gf_ref, rhi_ref, rlo_ref, o_ref, h_ref, aff_ref = refs[n_x:]
    mod = mod_ref[0, 0]
    x = _stream_tile(refs[:n_x], n_lat) + mod[2:3] * _dot(a_ref[0], w_ref[...])
    o_ref[0] = x
    _route(x, mod, gf_ref, rhi_ref, rlo_ref, h_ref, aff_ref)


def _out_proj(xs, a, w, mod, norm_ffn_g, router, n_lat, n_tiles):
    tm = ROW_TILE
    x_ins, x_specs = _stream_operands(xs, tm, n_lat)
    b, _, d = x_ins[0].shape
    t = a.shape[1]
    row = lambda bb, i: (bb, i, 0)
    r_ins, r_in_specs, r_out_specs, r_out_shape = _route_operands(x_ins[0], norm_ffn_g, router, n_tiles, tm)
    return pl.pallas_call(
        functools.partial(_out_proj_kernel, n_x=len(x_ins), n_lat=n_lat),
        grid=(b, n_tiles),
        in_specs=x_specs + [pl.BlockSpec((1, tm, a.shape[-1]), row), _const_spec(w.shape),
                            _mod_spec(n_lat // tm, d)] + r_in_specs,
        out_specs=[pl.BlockSpec((1, tm, d), row)] + r_out_specs,
        out_shape=[jax.ShapeDtypeStruct((b, t, d), F32)] + r_out_shape,
        input_output_aliases={0: 0} if len(x_ins) == 1 else {},
        compiler_params=_cparams(2),
        name="out_proj",
    )(*x_ins, a, w, mod, *r_ins)


def _sg_kernel(x_ref, g_ref, mod_ref, win_ref, lng_ref, lnb_ref, ws_ref, bs_ref, wout_ref, gf_ref, rhi_ref, rlo_ref,
               o_ref, h_ref, aff_ref, gated_ref):
    mod = mod_ref[0, 0]
    x = x_ref[0]
    tm = x.shape[0]
    width = lng_ref.shape[-1]
    gdim = width // SG_GROUPS
    h = _norm_mod(x, g_ref[...], mod[0:1], mod[1:2])
    hb = h.astype(BF16)
    pre = [_dot(hb, win_ref[:, c * SG_COLUMN_CHUNK:(c + 1) * SG_COLUMN_CHUNK])
           for c in list(range(width // SG_COLUMN_CHUNK, 2 * width // SG_COLUMN_CHUNK)) + list(range(width // SG_COLUMN_CHUNK))]
    z = [_gelu_tanh(p) for p in pre]
    n_half = width // SG_COLUMN_CHUNK
    v = jnp.concatenate(z[:n_half], axis=-1)
    u = jnp.concatenate(z[n_half:], axis=-1)
    mu = jnp.mean(v, axis=-1, keepdims=True)
    vc = v - mu
    var = jnp.mean(vc * vc, axis=-1, keepdims=True)
    vn = (vc * lax.rsqrt(var + EPS) * lng_ref[...] + lnb_ref[...]).astype(BF16)
    for c in range(tm // SG_CHUNK):
        rows = slice(c * SG_CHUNK, (c + 1) * SG_CHUNK)
        for gi in range(SG_GROUPS):
            cols = slice(gi * gdim, (gi + 1) * gdim)
            mixed = _dot(ws_ref[gi], vn[rows, cols]) + bs_ref[:, gi:gi + 1]
            gated_ref[rows, cols] = (u[rows, cols] * mixed).astype(BF16)
    x = x + mod[2:3] * _dot(gated_ref[...], wout_ref[...])
    o_ref[0] = x
    _route(x, mod, gf_ref, rhi_ref, rlo_ref, h_ref, aff_ref)


def _sg(x, g, mod, w, norm_ffn_g, router, n_lat):
    b, t, d = x.shape
    tm = ROW_TILE
    row = lambda bb, i: (bb, i, 0)
    consts = [w["win"], w["lng"], w["lnb"], w["ws"], w["bs"], w["wout"]]
    r_ins, r_in_specs, r_out_specs, r_out_shape = _route_operands(x, norm_ffn_g, router, t // tm, tm)
    return pl.pallas_call(
        _sg_kernel,
        grid=(b, t // tm),
        in_specs=[pl.BlockSpec((1, tm, d), row), _const_spec((1, d)), _mod_spec(n_lat // tm, d)]
        + [_const_spec(a.shape) for a in consts] + r_in_specs,
        out_specs=[pl.BlockSpec((1, tm, d), row)] + r_out_specs,
        out_shape=[jax.ShapeDtypeStruct(x.shape, F32)] + r_out_shape,
        scratch_shapes=[pltpu.VMEM((tm, w["lng"].shape[-1]), BF16)],
        input_output_aliases={0: 0},
        compiler_params=_cparams(2),
        name="chunk_mlp",
    )(x, g, mod, *consts, *r_ins)


def _prefix_counts(mask, tri):
    out = []
    carry = jnp.zeros((mask.shape[0], 1), F32)
    before = [carry]
    for c in range(mask.shape[1] // PREFIX_CHUNK):
        m = mask[:, c * PREFIX_CHUNK:(c + 1) * PREFIX_CHUNK]
        out.append(_dot(m.astype(BF16), tri) + carry)
        carry = carry + jnp.sum(m, axis=-1, keepdims=True)
        before.append(carry)
    return out, before


def _topk_kernel(aff_ref, pos_ref, cnt_ref, *, cap):
    a = aff_ref[0]
    ne = a.shape[0]
    bits = pltpu.bitcast(a, jnp.int32)

    def step(i, lo):
        cand = lo | jnp.left_shift(jnp.int32(1), 30 - i)
        cnt = jnp.sum(jnp.where(bits >= cand, 1.0, 0.0), axis=-1, keepdims=True)
        return jnp.where(cnt >= cap, cand, lo)

    thr = lax.fori_loop(0, 31, step, jnp.zeros((ne, 1), jnp.int32))
    gt = jnp.where(bits > thr, 1.0, 0.0)
    eq = jnp.where(bits == thr, 1.0, 0.0)
    room = cap - jnp.sum(gt, axis=-1, keepdims=True)
    ri = lax.broadcasted_iota(jnp.int32, (PREFIX_CHUNK, PREFIX_CHUNK), 0)
    ci = lax.broadcasted_iota(jnp.int32, (PREFIX_CHUNK, PREFIX_CHUNK), 1)
    tri = jnp.where(ri <= ci, 1.0, 0.0).astype(BF16)
    eq_rank, _ = _prefix_counts(eq, tri)
    sel = jnp.concatenate([
        jnp.maximum(gt[:, c * PREFIX_CHUNK:(c + 1) * PREFIX_CHUNK],
                    jnp.where(r <= room, eq[:, c * PREFIX_CHUNK:(c + 1) * PREFIX_CHUNK], 0.0))
        for c, r in enumerate(eq_rank)], axis=-1)
    sel_rank, before = _prefix_counts(sel, tri)
    for c, r in enumerate(sel_rank):
        cols = slice(c * PREFIX_CHUNK, (c + 1) * PREFIX_CHUNK)
        pos_ref[0, :, cols] = jnp.where(sel[:, cols] > 0.0, r - 1.0, -1.0).astype(jnp.int32)
    for c, cnt in enumerate(before):
        cnt_ref[0, :, c:c + 1] = cnt.astype(jnp.int32)


def _topk(aff_t, cap):
    b, ne, n = aff_t.shape
    spec = pl.BlockSpec((1, ne, n), lambda bb: (bb, 0, 0))
    n_cnt = n // PREFIX_CHUNK + 1
    return pl.pallas_call(
        functools.partial(_topk_kernel, cap=cap),
        grid=(b,),
        in_specs=[spec],
        out_specs=[spec, pl.BlockSpec((1, ne, n_cnt), lambda bb: (bb, 0, 0))],
        out_shape=[jax.ShapeDtypeStruct((b, ne, n), jnp.int32), jax.ShapeDtypeStruct((b, ne, n_cnt), jnp.int32)],
        compiler_params=_cparams(1),
        name="topk",
    )(aff_t)


def _moe_gather_kernel(cnt_ref, h_ref, pos_ref, aff_ref, xg_ref, gate_ref, *, cap, n_cnt):
    i = pl.program_id(1)
    ne, tn = pos_ref.shape[1:]
    win = min(GATHER_WINDOW, cap)
    row = (pl.program_id(0) * ne) * n_cnt + i

    @pl.when(i == 0)
    def _():
        xg_ref[...] = jnp.zeros(xg_ref.shape, xg_ref.dtype)
        gate_ref[...] = jnp.zeros(gate_ref.shape, gate_ref.dtype)

    h = h_ref[0]
    slot = lax.broadcasted_iota(jnp.int32, (win, tn), 0)

    def add_window(e, start, hit, res):
        rows = pl.ds(start, win)
        xg_ref[0, e, rows, :] = (xg_ref[0, e, rows, :].astype(F32) + res).astype(BF16)
        gate_ref[0, e, rows, :] += jnp.sum(jnp.where(hit, aff_ref[0, e:e + 1, :], 0.0), axis=-1, keepdims=True)

    starts, his, hits = [], [], []
    for e in range(ne):
        lo, hi = cnt_ref[row + e * n_cnt], cnt_ref[row + e * n_cnt + 1]
        start = pl.multiple_of(jnp.minimum((lo // BF16_ROWS) * BF16_ROWS, cap - win), BF16_ROWS)
        starts.append(start)
        his.append(hi)
        hits.append(pos_ref[0, e:e + 1, :] - start == slot)
    onehot = jnp.concatenate([jnp.where(hit, 1.0, 0.0).astype(BF16) for hit in hits], axis=0)
    res = _dot(onehot, h)
    for e in range(ne):
        add_window(e, starts[e], hits[e], res[e * win:(e + 1) * win])

    for e in range(ne):
        first = starts[e] + win
        n_more = jnp.maximum(his[e] - first + (win - 1), 0) // win

        def more(k, carry, e=e, first=first):
            lower = first + k * win
            start = pl.multiple_of(jnp.minimum(lower, cap - win), BF16_ROWS)
            p = pos_ref[0, e:e + 1, :]
            hit = jnp.logical_and(p - start == slot, p >= lower)
            add_window(e, start, hit, _dot(jnp.where(hit, 1.0, 0.0).astype(BF16), h))
            return carry

        lax.fori_loop(0, n_more, more, 0)


def _moe_gather(h, pos, aff_t, cnt, set_block, n_set, cap):
    b, _, d = h.shape
    ne = pos.shape[1]
    tn = PREFIX_CHUNK
    tiles = n_set // tn
    tile = pl.BlockSpec((1, ne, tn), lambda bb, i, c: (bb, 0, i))
    return pl.pallas_call(
        functools.partial(_moe_gather_kernel, cap=cap, n_cnt=cnt.shape[-1]),
        grid_spec=pltpu.PrefetchScalarGridSpec(
            num_scalar_prefetch=1,
            grid=(b, tiles),
            in_specs=[pl.BlockSpec((1, tn, d), lambda bb, i, c: (bb, set_block * tiles + i, 0)), tile, tile],
            out_specs=[pl.BlockSpec((1, ne, cap, d), lambda bb, i, c: (bb, 0, 0, 0)),
                       pl.BlockSpec((1, ne, cap, 1), lambda bb, i, c: (bb, 0, 0, 0))]),
        out_shape=[jax.ShapeDtypeStruct((b, ne, cap, d), BF16), jax.ShapeDtypeStruct((b, ne, cap, 1), F32)],
        compiler_params=_cparams(2),
        name="moe_gather",
    )(cnt.reshape(-1), h, pos, aff_t)


def _moe_ffn_kernel(xg_ref, gate_ref, wg32_ref, wu32_ref, wd32_ref, y_ref, wg_ref, wu_ref, wd_ref):
    @pl.when(pl.program_id(1) == 0)
    def _():
        wg_ref[...] = wg32_ref[0, 0].astype(BF16)
        wu_ref[...] = wu32_ref[0, 0].astype(BF16)
        wd_ref[...] = wd32_ref[0, 0].astype(BF16)

    bg, _, cap, _ = xg_ref.shape
    rows = lambda ref: ref[0, 0] if bg == 1 else jnp.concatenate([ref[bb, 0] for bb in range(bg)], axis=0)
    xg = rows(xg_ref)
    chunks = [slice(c * FFN_COLUMN_CHUNK, (c + 1) * FFN_COLUMN_CHUNK) for c in range(wg_ref.shape[1] // FFN_COLUMN_CHUNK)]
    pre = [(_dot(xg, wg_ref[:, c]), _dot(xg, wu_ref[:, c])) for c in chunks]
    hid = [(_silu(g) * u).astype(BF16) for g, u in pre]
    y = _dot(hid[0], wd_ref[chunks[0], :])
    for hc, c in zip(hid[1:], chunks[1:]):
        y = y + _dot(hc, wd_ref[c, :])
    y = y * rows(gate_ref)
    for bb in range(bg):
        y_ref[bb, 0] = y[bb * cap:(bb + 1) * cap].astype(BF16)


def _moe_ffn(xg, gates, wg, wu, wd, layer, bg):
    b, ne, cap, d = xg.shape
    ff = wg.shape[-1]
    tok = pl.BlockSpec((bg, 1, cap, d), lambda e, i: (i, e, 0, 0))
    return pl.pallas_call(
        _moe_ffn_kernel,
        grid=(ne, b // bg),
        in_specs=[tok, pl.BlockSpec((bg, 1, cap, 1), lambda e, i: (i, e, 0, 0)),
                  pl.BlockSpec((1, 1, d, ff), lambda e, i: (layer, e, 0, 0)),
                  pl.BlockSpec((1, 1, d, ff), lambda e, i: (layer, e, 0, 0)),
                  pl.BlockSpec((1, 1, ff, d), lambda e, i: (layer, e, 0, 0))],
        out_specs=tok,
        out_shape=jax.ShapeDtypeStruct((b, ne, cap, d), BF16),
        scratch_shapes=[pltpu.VMEM((d, ff), BF16), pltpu.VMEM((d, ff), BF16), pltpu.VMEM((ff, d), BF16)],
        compiler_params=_cparams(2),
        name="moe_ffn",
    )(xg, gates, wg, wu, wd)


def _moe_combine_kernel(cnt_ref, x_ref, pos_ref, y_ref, mod_ref, o_ref, *, cap, n_cnt):
    pos = pos_ref[0]
    tn, ne = pos.shape
    win = min(GATHER_WINDOW, cap)
    per = COMBINE_EXPERTS_PER_MATMUL
    gate = mod_ref[0, 0][5:6]
    row = (pl.program_id(0) * ne) * n_cnt + pl.program_id(1)
    lane = lax.broadcasted_iota(jnp.int32, (tn, per * win), 1)
    acc = jnp.zeros(x_ref.shape[1:], F32)
    overflow = False
    for g in range(ne // per):
        onehot = jnp.zeros((tn, per * win), F32)
        parts = []
        for j in range(per):
            e = g * per + j
            lo, hi = cnt_ref[row + e * n_cnt], cnt_ref[row + e * n_cnt + 1]
            start = pl.multiple_of(jnp.minimum((lo // BF16_ROWS) * BF16_ROWS, cap - win), BF16_ROWS)
            overflow = jnp.logical_or(overflow, hi > start + win)
            rel = pos[:, e:e + 1] - start
            target = jnp.where(jnp.logical_and(rel >= 0, rel < win), rel + j * win, -1)
            onehot = jnp.where(target == lane, 1.0, onehot)
            parts.append(y_ref[0, e, pl.ds(start, win), :])
        acc = acc + _dot(onehot.astype(BF16), jnp.concatenate(parts, axis=0))
    o_ref[0] = x_ref[0] + gate * acc

    @pl.when(overflow)
    def _():
        lane_all = lax.broadcasted_iota(jnp.int32, (tn, cap), 1)
        full = jnp.zeros(x_ref.shape[1:], F32)
        for e in range(ne):
            onehot = jnp.where(pos[:, e:e + 1] == lane_all, 1.0, 0.0).astype(BF16)
            full = full + _dot(onehot, y_ref[0, e])
        o_ref[0] = x_ref[0] + gate * full


def _moe_combine(x, pos_t, cnt, y, mod, tile_off, n_set, mod_sel, out_rows):
    b, t, d = x.shape
    _, ne, cap, _ = y.shape
    tn = PREFIX_CHUNK
    n_cnt = cnt.shape[-1]
    alias = out_rows == t
    out_off = tile_off if alias else 0
    return pl.pallas_call(
        functools.partial(_moe_combine_kernel, cap=cap, n_cnt=n_cnt),
        grid_spec=pltpu.PrefetchScalarGridSpec(
            num_scalar_prefetch=1,
            grid=(b, n_set // tn),
            in_specs=[pl.BlockSpec((1, tn, d), lambda bb, i, c: (bb, i + tile_off, 0)),
                      pl.BlockSpec((1, tn, ne), lambda bb, i, c: (bb, i, 0)),
                      pl.BlockSpec((1, ne, cap, d), lambda bb, i, c: (bb, 0, 0, 0)),
                      pl.BlockSpec((1, 1, 6, d), lambda bb, i, c: (bb, mod_sel, 0, 0))],
            out_specs=pl.BlockSpec((1, tn, d), lambda bb, i, c: (bb, i + out_off, 0))),
        out_shape=jax.ShapeDtypeStruct((b, out_rows, d), F32),
        input_output_aliases={1: 0} if alias else {},
        compiler_params=_cparams(2),
        name="moe_combine",
    )(cnt.reshape(-1), x, pos_t, y, mod)


def _moe(x, h, aff, mod, wg, wu, wd, layer, n_lat, with_ctx, final):
    b, t, d = x.shape
    n_ctx = t - n_lat
    aff_t = jnp.swapaxes(aff, 1, 2)
    sets = [(0, n_lat, 0, 1)]
    if with_ctx:
        sets.append((n_lat, n_ctx, 1, b))
    for start, n_set, mod_sel, bg in sets:
        cap = CAPACITY_FACTOR * n_set // N_EXPERTS
        a_set = aff_t[:, :, start:start + n_set]
        pos, cnt = _topk(a_set, cap)
        xg, gates = _moe_gather(h, pos, a_set, cnt, start // n_set, n_set, cap)
        y = _moe_ffn(xg, gates, wg, wu, wd, layer, bg)
        x = _moe_combine(x, jnp.swapaxes(pos, 1, 2), cnt, y, mod, start // PREFIX_CHUNK, n_set, mod_sel,
                         n_lat if final else t)
    return x


def _rope_tables(n_lat, n_ctx, rot_dim, starts):
    n_rows = n_lat // GRID_W
    rows = np.repeat(np.arange(n_rows, dtype=np.float32), GRID_W)
    cols = np.tile(np.arange(GRID_W, dtype=np.float32), n_rows)
    n_freq = rot_dim // 4
    inv_freq = (np.float32(ROPE_BASE) ** (-np.arange(n_freq, dtype=np.float32) / np.float32(n_freq))).astype(np.float32)
    ang = np.concatenate([rows[:, None] * inv_freq, cols[:, None] * inv_freq], axis=-1)
    half = rot_dim // 2
    cos_l, sin_l = np.cos(ang).astype(np.float32), np.sin(ang).astype(np.float32)
    cos = np.ones((n_lat + n_ctx, LANE), np.float32)
    sa = np.zeros((n_lat + n_ctx, LANE), np.float32)
    sb = np.zeros((n_lat + n_ctx, LANE), np.float32)
    for s in starts:
        cos[:n_lat, s:s + half] = cos_l
        cos[:n_lat, s + half:s + rot_dim] = cos_l
        sa[:n_lat, s:s + half] = -sin_l
        sb[:n_lat, s + half:s + rot_dim] = sin_l
    return jnp.asarray(cos), jnp.asarray(sa), jnp.asarray(sb)


def _pad_heads(w, heads, width):
    k = w.shape[0]
    w = w.reshape(k, heads, width)
    return jnp.pad(w, ((0, 0), (0, 0), (0, LANE - width))).reshape(k, heads * LANE)


def _mla_partner(a):
    half = MLA_ROPE // 2
    lane = jnp.arange(LANE)
    first = (lane >= MLA_NOPE) & (lane < MLA_NOPE + half)
    second = (lane >= MLA_NOPE + half) & (lane < MLA_QK)
    return jnp.where(first, jnp.roll(a, -half, axis=-1), jnp.where(second, jnp.roll(a, half, axis=-1), 0.0))


def _mla_weights(w_in, q_norm_g, w_uq, kv_norm_g, w_ukv):
    d = w_in.shape[0]
    lat = MLA_Q_LORA + MLA_KV_LORA
    rope_block = jnp.concatenate([jnp.zeros((d, MLA_NOPE), F32), w_in[:, lat:],
                                  jnp.zeros((d, LANE - MLA_QK), F32)], axis=1)
    ukv = w_ukv.reshape(MLA_KV_LORA, MLA_HEADS, MLA_NOPE + MLA_V)
    wuq = _pad_heads(w_uq, MLA_HEADS, MLA_QK)
    wuqp = _mla_partner(wuq.reshape(-1, MLA_HEADS, LANE))[:, :, MLA_NOPE:MLA_QK].reshape(wuq.shape[0], -1)
    return {
        "win": jnp.concatenate([w_in[:, :lat], rope_block, _mla_partner(rope_block)], axis=1).astype(BF16),
        "qng": q_norm_g.reshape(1, -1),
        "wuq": wuq.astype(BF16),
        "wuqp": wuqp.astype(BF16),
        "kvng": kv_norm_g.reshape(1, -1),
        "wuk": _pad_heads(ukv[:, :, :MLA_NOPE].reshape(MLA_KV_LORA, -1), MLA_HEADS, MLA_NOPE).astype(BF16),
        "wuv": _pad_heads(ukv[:, :, MLA_NOPE:].reshape(MLA_KV_LORA, -1), MLA_HEADS, MLA_V).astype(BF16),
    }


def _diff_partner(a):
    half = DIFF_HEAD_DIM // 2
    shape = a.shape
    a = a.reshape(shape[:-1] + (shape[-1] // LANE, LANE))
    first = (jnp.arange(LANE) % DIFF_HEAD_DIM) < half
    return jnp.where(first, jnp.roll(a, -half, axis=-1), jnp.roll(a, half, axis=-1)).reshape(shape)


def _diff_weights(w_in):
    d = w_in.shape[0]
    return jnp.concatenate([w_in, _diff_partner(w_in[:, :2 * d])], axis=1).astype(BF16)


def _diff_tables(tabs, qn_g, kn_g):
    cos, sa, sb = tabs
    out = []
    for g, scale in ((qn_g, DIFF_HEAD_DIM ** -0.5 * LOG2E), (kn_g, 1.0)):
        g2 = jnp.tile(g, 2)
        out += [cos * g2 * scale, (sa + sb) * _diff_partner(g2) * scale]
    return out


def _mla_tables(tabs, qn_g, kn_g):
    cos, sa, sb = tabs
    out = []
    for g, scale in ((qn_g, MLA_QK ** -0.5 * LOG2E), (kn_g, 1.0)):
        gp = jnp.pad(g, (0, LANE - MLA_QK))
        out += [cos * gp * scale, (sa + sb) * _mla_partner(gp) * scale]
    return out


def kernel(x, c, ctx, c_ctx, ada_w, ada_b, norm_mix_g, norm_ffn_g, mla_w_in, mla_q_norm_g, mla_w_uq, mla_kv_norm_g, mla_w_ukv, mla_qn_g, mla_kn_g, mla_w_out, diff_w_in, diff_qn_g, diff_kn_g, diff_lambda_q1, diff_lambda_k1, diff_lambda_q2, diff_lambda_k2, diff_sub_g, diff_w_out, sg_w_in, sg_ln_g, sg_ln_b, sg_w_s, sg_b_s, sg_w_out, moe_router, moe_w_gate, moe_w_up, moe_w_down):
    b, n_lat, d = x.shape
    n_ctx = ctx.shape[1]
    depth = ada_w.shape[0]
    assert n_lat % ROW_TILE == 0 and n_ctx == ROW_TILE and n_lat % n_ctx == 0

    rows = -(-(b + 1) // 8) * 8
    cc = jnp.concatenate([c, c_ctx[None], jnp.zeros((rows - b - 1, d), F32)], axis=0)
    mods = _ada(cc, ada_w, ada_b).reshape(depth, rows, 6, d)
    xs = (x, ctx) if depth > 1 else jnp.concatenate([x, ctx], axis=1)
    t = n_lat + n_ctx

    tabs_a = _rope_tables(n_lat, n_ctx, MLA_ROPE, (MLA_NOPE,))
    tabs_b = _rope_tables(n_lat, n_ctx, DIFF_HEAD_DIM, (0, DIFF_HEAD_DIM))

    for i in range(depth):
        kind, j = i % N_MIXERS, i // N_MIXERS
        last = i == depth - 1
        with_ctx = not last
        q_tiles = (n_lat + (n_ctx if with_ctx else 0)) // ROW_TILE
        mod = jnp.stack([mods[i, :b], jnp.broadcast_to(mods[i, b], (b, 6, d))], axis=1)
        g_mix, g_ffn = norm_mix_g[i].reshape(1, d), norm_ffn_g[i].reshape(1, d)
        if kind == 0:
            w = _mla_weights(mla_w_in[j], mla_q_norm_g[j], mla_w_uq[j], mla_kv_norm_g[j], mla_w_ukv[j])
            q, k, v = _mla_in(xs, g_mix, mod, w, _mla_tables(tabs_a, mla_qn_g[j], mla_kn_g[j]), n_lat, t)
            a = _attn_calls(_mla_attn_kernel, "mla_attn", [q], [k, v], [], n_lat, with_ctx,
                            MLA_HEADS, MLA_HEADS_PER_STEP, MLA_HEADS_PER_STEP * MLA_V)
            xs, h, aff = _out_proj(xs, a, mla_w_out[j].astype(BF16), mod, g_ffn, moe_router[i], n_lat, q_tiles)
        elif kind == 1:
            lam_init = 0.8 - 0.6 * math.exp(-0.3 * i)
            q0, q1, k, v = _diff_in(xs, g_mix, mod, _diff_weights(diff_w_in[j]),
                                    _diff_tables(tabs_b, diff_qn_g[j], diff_kn_g[j]), n_lat)
            wa = [a.reshape(1, -1) for a in (diff_lambda_q1[j], diff_lambda_k1[j], diff_lambda_q2[j],
                                             diff_lambda_k2[j], diff_sub_g[j])]
            a = _attn_calls(functools.partial(_diff_attn_kernel, lam_init=lam_init), "diff_attn", [q0, q1], [k, v],
                            wa, n_lat, with_ctx, DIFF_HEADS, DIFF_HEADS_PER_STEP, DIFF_HEADS_PER_STEP * LANE)
            xs, h, aff = _out_proj(xs, a, diff_w_out[j].astype(BF16), mod, g_ffn, moe_router[i], n_lat, q_tiles)
        else:
            w = {"win": sg_w_in[j].astype(BF16), "lng": sg_ln_g[j].reshape(1, -1), "lnb": sg_ln_b[j].reshape(1, -1),
                 "ws": sg_w_s[j].astype(BF16), "bs": sg_b_s[j].T, "wout": sg_w_out[j].astype(BF16)}
            xs, h, aff = _sg(xs, g_mix, mod, w, g_ffn, moe_router[i], n_lat)
        xs = _moe(xs, h, aff, mod, moe_w_gate, moe_w_up, moe_w_down, i, n_lat, with_ctx, last)
    return xs
```

```python
import functools
import math

import jax
import jax.numpy as jnp
import numpy as np
from jax import lax
from jax.experimental import pallas as pl
from jax.experimental.pallas import tpu as pltpu

F32 = jnp.float32
BF16 = jnp.bfloat16

GRID_W = 64
ROPE_BASE = 10000.0
EPS = 1e-6
N_MIXERS = 3

MLA_HEADS = 16
MLA_Q_LORA = 512
MLA_KV_LORA = 256
MLA_NOPE = 64
MLA_ROPE = 32
MLA_V = 64
MLA_QK = MLA_NOPE + MLA_ROPE

DIFF_HEADS = 8
DIFF_HEAD_DIM = 64
DIFF_V_DIM = 2 * DIFF_HEAD_DIM

SG_CHUNK = 128
SG_GROUPS = 8

N_EXPERTS = 16
CAPACITY_FACTOR = 2

LANE = 128
BF16_ROWS = 16
ROW_TILE = 256
SG_COLUMN_CHUNK = 1024
FFN_COLUMN_CHUNK = 256
ADA_COLUMN_TILE = 1536
PREFIX_CHUNK = 256
GATHER_WINDOW = 64
COMBINE_EXPERTS_PER_MATMUL = 4
MLA_HEADS_PER_STEP = 4
DIFF_HEADS_PER_STEP = 4
ATTN_Q_TILE = 256
LOG2E = math.log2(math.e)
VMEM_LIMIT = 56 * 1024 * 1024


def _cparams(n_axes):
    return pltpu.CompilerParams(dimension_semantics=("arbitrary",) * n_axes, vmem_limit_bytes=VMEM_LIMIT)


def _const_spec(shape):
    nd = len(shape)
    return pl.BlockSpec(shape, lambda *_: (0,) * nd)


def _dot(a, b):
    return jnp.dot(a, b, preferred_element_type=F32)


def _dot_nt(a, b):
    return lax.dot_general(a, b, (((1,), (1,)), ((), ())), preferred_element_type=F32)


def _rms(x, n):
    return x * lax.rsqrt(jnp.sum(x * x, axis=-1, keepdims=True) * (1.0 / n) + EPS)


def _norm_mod(x, g, shift, scale):
    return _rms(x, x.shape[-1]) * g * (1.0 + scale) + shift


def _silu(x):
    return x / (1.0 + jnp.exp(-x))


def _gelu_tanh(x):
    c = math.sqrt(2.0 / math.pi)
    return 0.5 * x * (1.0 + jnp.tanh(c * (x + 0.044715 * (x * x * x))))


def _ada_kernel(c_ref, w_ref, b_ref, o_ref):
    s = _silu(c_ref[...])
    o_ref[0] = jnp.dot(s, w_ref[0], preferred_element_type=F32, precision=lax.Precision.HIGHEST) + b_ref[0]


def _ada(cc, ada_w, ada_b):
    depth, d, six_d = ada_w.shape
    rows = cc.shape[0]
    tn = ADA_COLUMN_TILE
    return pl.pallas_call(
        _ada_kernel,
        grid=(depth, six_d // tn),
        in_specs=[
            _const_spec((rows, d)),
            pl.BlockSpec((1, d, tn), lambda i, j: (i, 0, j)),
            pl.BlockSpec((1, 1, tn), lambda i, j: (i, 0, j)),
        ],
        out_specs=pl.BlockSpec((1, rows, tn), lambda i, j: (i, 0, j)),
        out_shape=jax.ShapeDtypeStruct((depth, rows, six_d), F32),
        compiler_params=_cparams(2),
        name="ada",
    )(cc, ada_w, ada_b.reshape(depth, 1, six_d))


def _mod_spec(n_lat_tiles, d):
    return pl.BlockSpec((1, 1, 6, d), lambda b, i: (b, jnp.minimum(i // n_lat_tiles, 1), 0, 0))


def _stream_operands(xs, tm, n_lat):
    if not isinstance(xs, tuple):
        return [xs], [pl.BlockSpec((1, tm, xs.shape[-1]), lambda bb, i: (bb, i, 0))]
    lat, ctx = xs
    last = n_lat // tm - 1
    return [lat, ctx], [pl.BlockSpec((1, tm, lat.shape[-1]), lambda bb, i: (bb, jnp.minimum(i, last), 0)),
                        pl.BlockSpec((1, tm, ctx.shape[-1]), lambda bb, i: (bb, 0, 0))]


def _stream_tile(x_refs, n_lat):
    if len(x_refs) == 1:
        return x_refs[0][0]
    is_lat = pl.program_id(1) * x_refs[0].shape[1] < n_lat
    return jnp.where(is_lat, x_refs[0][0], x_refs[1][0])


def _mla_in_kernel(*refs, n_x, n_lat):
    (g_ref, mod_ref, win_ref, qng_ref, wuq_ref, wuqp_ref, kvng_ref, wuk_ref, wuv_ref,
     aq_ref, bq_ref, ak_ref, bk_ref, q_ref, k_ref, v_ref) = refs[n_x:]
    mod = mod_ref[0, 0]
    h = _norm_mod(_stream_tile(refs[:n_x], n_lat), g_ref[...], mod[0:1], mod[1:2])
    down = _dot(h.astype(BF16), win_ref[...])
    cq = (_rms(down[:, :MLA_Q_LORA], MLA_Q_LORA) * qng_ref[...]).astype(BF16)
    ckv = (_rms(down[:, MLA_Q_LORA:MLA_Q_LORA + MLA_KV_LORA], MLA_KV_LORA) * kvng_ref[...]).astype(BF16)
    q = _dot(cq, wuq_ref[...])
    qp = _dot(cq, wuqp_ref[...])
    kn = _dot(ckv, wuk_ref[...])
    v = _dot(ckv, wuv_ref[...])
    is_v = (lax.broadcasted_iota(jnp.int32, v.shape, 1) & (LANE - 1)) < MLA_V
    v_ref[0] = jnp.where(is_v, v, 1.0).astype(BF16)
    lat = MLA_Q_LORA + MLA_KV_LORA
    kr, krp = down[:, lat:lat + LANE], down[:, lat + LANE:]
    aq, bq, ak, bk = aq_ref[...], bq_ref[...], ak_ref[...], bk_ref[...]
    krot = krp * bk
    inv_n = 1.0 / MLA_QK
    for hd in range(MLA_HEADS):
        sl = slice(hd * LANE, (hd + 1) * LANE)
        qh = q[:, sl]
        rq = lax.rsqrt(jnp.sum(qh * qh, axis=-1, keepdims=True) * inv_n + EPS)
        group = qp[:, hd // 4 * LANE:(hd // 4 + 1) * LANE]
        shift = (MLA_NOPE - MLA_ROPE * (hd % 4)) % LANE
        qph = pltpu.roll(group, shift, 1) if shift else group
        q_ref[0, :, sl] = ((qh * aq + qph * bq) * rq).astype(BF16)
        kh = kn[:, sl] + kr
        rk = lax.rsqrt(jnp.sum(kh * kh, axis=-1, keepdims=True) * inv_n + EPS)
        k_ref[0, :, sl] = ((kh * ak + krot) * rk).astype(BF16)


def _mla_in(xs, g, mod, w, tabs, n_lat, t):
    tm = ROW_TILE
    x_ins, x_specs = _stream_operands(xs, tm, n_lat)
    b, _, d = x_ins[0].shape
    hw = MLA_HEADS * LANE
    row = lambda bb, i: (bb, i, 0)
    tab = pl.BlockSpec((tm, LANE), lambda bb, i: (i, 0))
    consts = [w["win"], w["qng"], w["wuq"], w["wuqp"], w["kvng"], w["wuk"], w["wuv"]]
    return pl.pallas_call(
        functools.partial(_mla_in_kernel, n_x=len(x_ins), n_lat=n_lat),
        grid=(b, t // tm),
        in_specs=x_specs + [_const_spec((1, d)), _mod_spec(n_lat // tm, d)]
        + [_const_spec(a.shape) for a in consts] + [tab] * 4,
        out_specs=[pl.BlockSpec((1, tm, hw), row)] * 3,
        out_shape=[jax.ShapeDtypeStruct((b, t, hw), BF16)] * 3,
        compiler_params=_cparams(2),
        name="mla_in",
    )(*x_ins, g, mod, *consts, *tabs)


def _diff_in_kernel(x_ref, g_ref, mod_ref, win_ref, aq_ref, bq_ref, ak_ref, bk_ref, q0_ref, q1_ref, k_ref, v_ref):
    mod = mod_ref[0, 0]
    d = x_ref.shape[-1]
    h = _norm_mod(x_ref[0], g_ref[...], mod[0:1], mod[1:2])
    hb = h.astype(BF16)
    part = {name: _dot(hb, win_ref[:, c * d:(c + 1) * d]) for name, c in (("q", 0), ("qp", 3), ("k", 1), ("kp", 4), ("v", 2))}
    ones = jnp.ones((x_ref.shape[1], LANE), BF16)
    lo = lax.broadcasted_iota(jnp.int32, (x_ref.shape[1], LANE), 1) < DIFF_HEAD_DIM
    inv_n = 1.0 / DIFF_HEAD_DIM

    def normed_rope(name, hd, a, b):
        xh = part[name][:, hd * LANE:(hd + 1) * LANE]
        xp = part[name + "p"][:, hd * LANE:(hd + 1) * LANE]
        sq = xh * xh
        s_lo = jnp.sum(jnp.where(lo, sq, 0.0), axis=-1, keepdims=True)
        s_hi = jnp.sum(jnp.where(lo, 0.0, sq), axis=-1, keepdims=True)
        return (xh * a + xp * b) * lax.rsqrt(jnp.where(lo, s_lo, s_hi) * inv_n + EPS)

    aq, bq, ak, bk = aq_ref[...], bq_ref[...], ak_ref[...], bk_ref[...]
    for hd in range(DIFF_HEADS):
        sl = slice(hd * LANE, (hd + 1) * LANE)
        qh = normed_rope("q", hd, aq, bq)
        q0_ref[0, :, sl] = jnp.where(lo, qh, 0.0).astype(BF16)
        q1_ref[0, :, sl] = jnp.where(lo, 0.0, qh).astype(BF16)
    for hd in range(DIFF_HEADS):
        sl = slice(hd * LANE, (hd + 1) * LANE)
        k_ref[0, :, sl] = normed_rope("k", hd, ak, bk).astype(BF16)
    for hd in range(DIFF_HEADS):
        v_ref[0, :, 2 * hd * LANE:(2 * hd + 1) * LANE] = part["v"][:, hd * LANE:(hd + 1) * LANE].astype(BF16)
        v_ref[0, :, (2 * hd + 1) * LANE:(2 * hd + 2) * LANE] = ones


def _diff_in(x, g, mod, win, tabs, n_lat):
    b, t, d = x.shape
    tm = ROW_TILE
    row = lambda bb, i: (bb, i, 0)
    tab = pl.BlockSpec((tm, LANE), lambda bb, i: (i, 0))
    out = jax.ShapeDtypeStruct((b, t, d), BF16)
    return pl.pallas_call(
        _diff_in_kernel,
        grid=(b, t // tm),
        in_specs=[pl.BlockSpec((1, tm, d), row), _const_spec((1, d)), _mod_spec(n_lat // tm, d),
                  _const_spec(win.shape)] + [tab] * 4,
        out_specs=[pl.BlockSpec((1, tm, d), row)] * 3 + [pl.BlockSpec((1, tm, 2 * d), row)],
        out_shape=[out] * 3 + [jax.ShapeDtypeStruct((b, t, 2 * d), BF16)],
        compiler_params=_cparams(2),
        name="diff_in",
    )(x, g, mod, win, *tabs)


def _mla_attn_kernel(q_ref, k_ref, v_ref, *rest):
    o_ref = rest[-1]
    tq = q_ref.shape[1]
    low = lax.broadcasted_iota(jnp.int32, (tq, LANE), 1) < MLA_V
    blocks = [slice(hd * LANE, (hd + 1) * LANE) for hd in range(MLA_HEADS_PER_STEP)]
    scores = [_dot_nt(q_ref[0, :, sl], k_ref[0, :, sl]) for sl in blocks]
    weights = [jnp.exp2(s - jnp.max(s, axis=-1, keepdims=True)).astype(BF16) for s in scores]
    for pr in range(MLA_HEADS_PER_STEP // 2):
        halves = []
        for hd in (2 * pr, 2 * pr + 1):
            r = _dot(weights[hd], v_ref[0, :, blocks[hd]])
            halves.append(r / pltpu.roll(r, LANE - MLA_V, 1))
        pair = jnp.where(low, halves[0], pltpu.roll(halves[1], MLA_V, 1))
        o_ref[0, :, pr * LANE:(pr + 1) * LANE] = pair.astype(o_ref.dtype)


def _attn_calls(kernel, name, qs, kvs, consts, n_lat, with_ctx, heads, hs, out_width):
    b, t, _ = kvs[0].shape
    n_ctx = t - n_lat
    ctx_blk = n_lat // n_ctx
    out_shape = jax.ShapeDtypeStruct((b, t, heads // hs * out_width), BF16)
    cspecs = [_const_spec(a.shape) for a in consts]

    def call(tq, q_map, kv_rows, kv_map, n_tiles, prev):
        q_specs = [pl.BlockSpec((1, tq, a.shape[-1] // (heads // hs)), q_map) for a in qs]
        kv_specs = [pl.BlockSpec((1, kv_rows, a.shape[-1] // (heads // hs)), kv_map) for a in kvs]
        extra = [] if prev is None else [pl.BlockSpec(memory_space=pl.ANY)]
        n_in = len(qs) + len(kvs) + len(consts)
        return pl.pallas_call(
            kernel,
            grid=(b, heads // hs, n_tiles),
            in_specs=q_specs + kv_specs + cspecs + extra,
            out_specs=pl.BlockSpec((1, tq, out_width), q_map),
            out_shape=out_shape,
            input_output_aliases={} if prev is None else {n_in: 0},
            compiler_params=_cparams(3),
            name=name,
        )(*qs, *kvs, *consts, *([] if prev is None else [prev]))

    tq = min(ATTN_Q_TILE, n_lat)
    out = call(tq, lambda bb, hg, i: (bb, i, hg), t, lambda bb, hg, i: (bb, 0, hg), n_lat // tq, None)
    if with_ctx:
        ctx_map = lambda bb, hg, i: (bb, ctx_blk, hg)
        out = call(n_ctx, ctx_map, n_ctx, ctx_map, 1, out)
    return out


def _diff_attn_kernel(q0_ref, q1_ref, k_ref, v_ref, lq1_ref, lk1_ref, lq2_ref, lk2_ref, sub_ref, *rest, lam_init):
    o_ref = rest[-1]
    lam = (jnp.exp(jnp.sum(lq1_ref[...] * lk1_ref[...], axis=-1, keepdims=True))
           - jnp.exp(jnp.sum(lq2_ref[...] * lk2_ref[...], axis=-1, keepdims=True)) + lam_init)
    for a in range(DIFF_HEADS_PER_STEP):
        sl = slice(a * LANE, (a + 1) * LANE)
        k, v = k_ref[0, :, sl], v_ref[0, :, 2 * a * LANE:(2 * a + 2) * LANE]
        scores = [_dot_nt(q_ref[0, :, sl], k) for q_ref in (q0_ref, q1_ref)]
        outs = []
        for s in scores:
            r = _dot(jnp.exp2(s - jnp.max(s, axis=-1, keepdims=True)).astype(BF16), v)
            outs.append(r[:, :LANE] / r[:, LANE:])
        o = outs[0] - lam * outs[1]
        o_ref[0, :, sl] = (_rms(o, DIFF_V_DIM) * sub_ref[...] * (1.0 - lam_init)).astype(o_ref.dtype)


def _route(x, mod, gf_ref, rhi_ref, rlo_ref, h_ref, aff_ref):
    ne = aff_ref.shape[-1]
    h = _norm_mod(x, gf_ref[...], mod[3:4], mod[4:5])
    h_hi = h.astype(BF16)
    h_lo = (h - h_hi.astype(F32)).astype(BF16)
    h_ref[0] = h_hi
    parts = _dot(h_hi, rhi_ref[...]) + _dot(h_lo, rlo_ref[...])
    logits = (parts + pltpu.roll(parts, LANE // 2, 1))[:, :ne]
    e = jnp.exp(logits - jnp.max(logits, axis=-1, keepdims=True))
    aff_ref[0] = e / jnp.sum(e, axis=-1, keepdims=True)


def _route_operands(x, norm_ffn_g, router, n_tiles, tm):
    b, _, d = x.shape
    ne = router.shape[-1]
    assert ne <= LANE // 2
    row = lambda bb, i: (bb, i, 0)
    r_hi = router.astype(BF16)
    r_lo = (router - r_hi.astype(F32)).astype(BF16)
    half = lambda r: jnp.pad(r, ((0, 0), (0, LANE // 2 - ne)))
    ins = [norm_ffn_g, jnp.concatenate([half(r_hi), half(r_lo)], axis=1),
           jnp.concatenate([half(r_hi), jnp.zeros((d, LANE // 2), BF16)], axis=1)]
    in_specs = [_const_spec(a.shape) for a in ins]
    out_specs = [pl.BlockSpec((1, tm, d), row), pl.BlockSpec((1, tm, ne), row)]
    out_shape = [jax.ShapeDtypeStruct((b, n_tiles * tm, d), BF16), jax.ShapeDtypeStruct((b, n_tiles * tm, ne), F32)]
    return ins, in_specs, out_specs, out_shape


def _out_proj_kernel(*refs, n_x, n_lat):
    a_ref, w_ref, mod_ref, gf_ref, rhi_ref, rlo_ref, o_ref, h_ref, aff_ref = refs[n_x:]
    mod = mod_ref[0, 0]
    x = _stream_tile(refs[:n_x], n_lat) + mod[2:3] * _dot(a_ref[0], w_ref[...])
    o_ref[0] = x
    _route(x, mod, gf_ref, rhi_ref, rlo_ref, h_ref, aff_ref)


def _out_proj(xs, a, w, mod, norm_ffn_g, router, n_lat, n_tiles):
    tm = ROW_TILE
    x_ins, x_specs = _stream_operands(xs, tm, n_lat)
    b, _, d = x_ins[0].shape
    t = a.shape[1]
    row = lambda bb, i: (bb, i, 0)
    r_ins, r_in_specs, r_out_specs, r_out_shape = _route_operands(x_ins[0], norm_ffn_g, router, n_tiles, tm)
    return pl.pallas_call(
        functools.partial(_out_proj_kernel, n_x=len(x_ins), n_lat=n_lat),
        grid=(b, n_tiles),
        in_specs=x_specs + [pl.BlockSpec((1, tm, a.shape[-1]), row), _const_spec(w.shape),
                            _mod_spec(n_lat // tm, d)] + r_in_specs,
        out_specs=[pl.BlockSpec((1, tm, d), row)] + r_out_specs,
        out_shape=[jax.ShapeDtypeStruct((b, t, d), F32)] + r_out_shape,
        input_output_aliases={0: 0} if len(x_ins) == 1 else {},
        compiler_params=_cparams(2),
        name="out_proj",
    )(*x_ins, a, w, mod, *r_ins)


def _sg_kernel(x_ref, g_ref, mod_ref, win_ref, lng_ref, lnb_ref, ws_ref, bs_ref, wout_ref, gf_ref, rhi_ref, rlo_ref,
               o_ref, h_ref, aff_ref, gated_ref):
    mod = mod_ref[0, 0]
    x = x_ref[0]
    tm = x.shape[0]
    width = lng_ref.shape[-1]
    gdim = width // SG_GROUPS
    h = _norm_mod(x, g_ref[...], mod[0:1], mod[1:2])
    hb = h.astype(BF16)
    pre = [_dot(hb, win_ref[:, c * SG_COLUMN_CHUNK:(c + 1) * SG_COLUMN_CHUNK])
           for c in list(range(width // SG_COLUMN_CHUNK, 2 * width // SG_COLUMN_CHUNK)) + list(range(width // SG_COLUMN_CHUNK))]
    z = [_gelu_tanh(p) for p in pre]
    n_half = width // SG_COLUMN_CHUNK
    v = jnp.concatenate(z[:n_half], axis=-1)
    u = jnp.concatenate(z[n_half:], axis=-1)
    mu = jnp.mean(v, axis=-1, keepdims=True)
    vc = v - mu
    var = jnp.mean(vc * vc, axis=-1, keepdims=True)
    vn = (vc * lax.rsqrt(var + EPS) * lng_ref[...] + lnb_ref[...]).astype(BF16)
    for c in range(tm // SG_CHUNK):
        rows = slice(c * SG_CHUNK, (c + 1) * SG_CHUNK)
        for gi in range(SG_GROUPS):
            cols = slice(gi * gdim, (gi + 1) * gdim)
            mixed = _dot(ws_ref[gi], vn[rows, cols]) + bs_ref[:, gi:gi + 1]
            gated_ref[rows, cols] = (u[rows, cols] * mixed).astype(BF16)
    x = x + mod[2:3] * _dot(gated_ref[...], wout_ref[...])
    o_ref[0] = x
    _route(x, mod, gf_ref, rhi_ref, rlo_ref, h_ref, aff_ref)


def _sg(x, g, mod, w, norm_ffn_g, router, n_lat):
    b, t, d = x.shape
    tm = ROW_TILE
    row = lambda bb, i: (bb, i, 0)
    consts = [w["win"], w["lng"], w["lnb"], w["ws"], w["bs"], w["wout"]]
    r_ins, r_in_specs, r_out_specs, r_out_shape = _route_operands(x, norm_ffn_g, router, t // tm, tm)
    return pl.pallas_call(
        _sg_kernel,
        grid=(b, t // tm),
        in_specs=[pl.BlockSpec((1, tm, d), row), _const_spec((1, d)), _mod_spec(n_lat // tm, d)]
        + [_const_spec(a.shape) for a in consts] + r_in_specs,
        out_specs=[pl.BlockSpec((1, tm, d), row)] + r_out_specs,
        out_shape=[jax.ShapeDtypeStruct(x.shape, F32)] + r_out_shape,
        scratch_shapes=[pltpu.VMEM((tm, w["lng"].shape[-1]), BF16)],
        input_output_aliases={0: 0},
        compiler_params=_cparams(2),
        name="chunk_mlp",
    )(x, g, mod, *consts, *r_ins)


def _prefix_counts(mask, tri):
    out = []
    carry = jnp.zeros((mask.shape[0], 1), F32)
    before = [carry]
    for c in range(mask.shape[1] // PREFIX_CHUNK):
        m = mask[:, c * PREFIX_CHUNK:(c + 1) * PREFIX_CHUNK]
        out.append(_dot(m.astype(BF16), tri) + carry)
        carry = carry + jnp.sum(m, axis=-1, keepdims=True)
        before.append(carry)
    return out, before


def _topk_kernel(aff_ref, pos_ref, cnt_ref, *, cap):
    a = aff_ref[0]
    ne = a.shape[0]
    bits = pltpu.bitcast(a, jnp.int32)

    def step(i, lo):
        cand = lo | jnp.left_shift(jnp.int32(1), 30 - i)
        cnt = jnp.sum(jnp.where(bits >= cand, 1.0, 0.0), axis=-1, keepdims=True)
        return jnp.where(cnt >= cap, cand, lo)

    thr = lax.fori_loop(0, 31, step, jnp.zeros((ne, 1), jnp.int32))
    gt = jnp.where(bits > thr, 1.0, 0.0)
    eq = jnp.where(bits == thr, 1.0, 0.0)
    room = cap - jnp.sum(gt, axis=-1, keepdims=True)
    ri = lax.broadcasted_iota(jnp.int32, (PREFIX_CHUNK, PREFIX_CHUNK), 0)
    ci = lax.broadcasted_iota(jnp.int32, (PREFIX_CHUNK, PREFIX_CHUNK), 1)
    tri = jnp.where(ri <= ci, 1.0, 0.0).astype(BF16)
    eq_rank, _ = _prefix_counts(eq, tri)
    sel = jnp.concatenate([
        jnp.maximum(gt[:, c * PREFIX_CHUNK:(c + 1) * PREFIX_CHUNK],
                    jnp.where(r <= room, eq[:, c * PREFIX_CHUNK:(c + 1) * PREFIX_CHUNK], 0.0))
        for c, r in enumerate(eq_rank)], axis=-1)
    sel_rank, before = _prefix_counts(sel, tri)
    for c, r in enumerate(sel_rank):
        cols = slice(c * PREFIX_CHUNK, (c + 1) * PREFIX_CHUNK)
        pos_ref[0, :, cols] = jnp.where(sel[:, cols] > 0.0, r - 1.0, -1.0).astype(jnp.int32)
    for c, cnt in enumerate(before):
        cnt_ref[0, :, c:c + 1] = cnt.astype(jnp.int32)


def _topk(aff_t, cap):
    b, ne, n = aff_t.shape
    spec = pl.BlockSpec((1, ne, n), lambda bb: (bb, 0, 0))
    n_cnt = n // PREFIX_CHUNK + 1
    return pl.pallas_call(
        functools.partial(_topk_kernel, cap=cap),
        grid=(b,),
        in_specs=[spec],
        out_specs=[spec, pl.BlockSpec((1, ne, n_cnt), lambda bb: (bb, 0, 0))],
        out_shape=[jax.ShapeDtypeStruct((b, ne, n), jnp.int32), jax.ShapeDtypeStruct((b, ne, n_cnt), jnp.int32)],
        compiler_params=_cparams(1),
        name="topk",
    )(aff_t)


def _moe_gather_kernel(cnt_ref, h_ref, pos_ref, aff_ref, xg_ref, gate_ref, *, cap, n_cnt):
    i = pl.program_id(1)
    ne, tn = pos_ref.shape[1:]
    win = min(GATHER_WINDOW, cap)
    row = (pl.program_id(0) * ne) * n_cnt + i

    @pl.when(i == 0)
    def _():
        xg_ref[...] = jnp.zeros(xg_ref.shape, xg_ref.dtype)
        gate_ref[...] = jnp.zeros(gate_ref.shape, gate_ref.dtype)

    h = h_ref[0]
    slot = lax.broadcasted_iota(jnp.int32, (win, tn), 0)

    def add_window(e, start, hit, res):
        rows = pl.ds(start, win)
        xg_ref[0, e, rows, :] = (xg_ref[0, e, rows, :].astype(F32) + res).astype(BF16)
        gate_ref[0, e, rows, :] += jnp.sum(jnp.where(hit, aff_ref[0, e:e + 1, :], 0.0), axis=-1, keepdims=True)

    starts, his, hits = [], [], []
    for e in range(ne):
        lo, hi = cnt_ref[row + e * n_cnt], cnt_ref[row + e * n_cnt + 1]
        start = pl.multiple_of(jnp.minimum((lo // BF16_ROWS) * BF16_ROWS, cap - win), BF16_ROWS)
        starts.append(start)
        his.append(hi)
        hits.append(pos_ref[0, e:e + 1, :] - start == slot)
    onehot = jnp.concatenate([jnp.where(hit, 1.0, 0.0).astype(BF16) for hit in hits], axis=0)
    res = _dot(onehot, h)
    for e in range(ne):
        add_window(e, starts[e], hits[e], res[e * win:(e + 1) * win])

    for e in range(ne):
        first = starts[e] + win
        n_more = jnp.maximum(his[e] - first + (win - 1), 0) // win

        def more(k, carry, e=e, first=first):
            lower = first + k * win
            start = pl.multiple_of(jnp.minimum(lower, cap - win), BF16_ROWS)
            p = pos_ref[0, e:e + 1, :]
            hit = jnp.logical_and(p - start == slot, p >= lower)
            add_window(e, start, hit, _dot(jnp.where(hit, 1.0, 0.0).astype(BF16), h))
            return carry

        lax.fori_loop(0, n_more, more, 0)


def _moe_gather(h, pos, aff_t, cnt, set_block, n_set, cap):
    b, _, d = h.shape
    ne = pos.shape[1]
    tn = PREFIX_CHUNK
    tiles = n_set // tn
    tile = pl.BlockSpec((1, ne, tn), lambda bb, i, c: (bb, 0, i))
    return pl.pallas_call(
        functools.partial(_moe_gather_kernel, cap=cap, n_cnt=cnt.shape[-1]),
        grid_spec=pltpu.PrefetchScalarGridSpec(
            num_scalar_prefetch=1,
            grid=(b, tiles),
            in_specs=[pl.BlockSpec((1, tn, d), lambda bb, i, c: (bb, set_block * tiles + i, 0)), tile, tile],
            out_specs=[pl.BlockSpec((1, ne, cap, d), lambda bb, i, c: (bb, 0, 0, 0)),
                       pl.BlockSpec((1, ne, cap, 1), lambda bb, i, c: (bb, 0, 0, 0))]),
        out_shape=[jax.ShapeDtypeStruct((b, ne, cap, d), BF16), jax.ShapeDtypeStruct((b, ne, cap, 1), F32)],
        compiler_params=_cparams(2),
        name="moe_gather",
    )(cnt.reshape(-1), h, pos, aff_t)


def _moe_ffn_kernel(xg_ref, gate_ref, wg32_ref, wu32_ref, wd32_ref, y_ref, wg_ref, wu_ref, wd_ref):
    @pl.when(pl.program_id(1) == 0)
    def _():
        wg_ref[...] = wg32_ref[0, 0].astype(BF16)
        wu_ref[...] = wu32_ref[0, 0].astype(BF16)
        wd_ref[...] = wd32_ref[0, 0].astype(BF16)

    bg, _, cap, _ = xg_ref.shape
    rows = lambda ref: ref[0, 0] if bg == 1 else jnp.concatenate([ref[bb, 0] for bb in range(bg)], axis=0)
    xg = rows(xg_ref)
    chunks = [slice(c * FFN_COLUMN_CHUNK, (c + 1) * FFN_COLUMN_CHUNK) for c in range(wg_ref.shape[1] // FFN_COLUMN_CHUNK)]
    pre = [(_dot(xg, wg_ref[:, c]), _dot(xg, wu_ref[:, c])) for c in chunks]
    hid = [(_silu(g) * u).astype(BF16) for g, u in pre]
    y = _dot(hid[0], wd_ref[chunks[0], :])
    for hc, c in zip(hid[1:], chunks[1:]):
        y = y + _dot(hc, wd_ref[c, :])
    y = y * rows(gate_ref)
    for bb in range(bg):
        y_ref[bb, 0] = y[bb * cap:(bb + 1) * cap].astype(BF16)


def _moe_ffn(xg, gates, wg, wu, wd, layer, bg):
    b, ne, cap, d = xg.shape
    ff = wg.shape[-1]
    tok = pl.BlockSpec((bg, 1, cap, d), lambda e, i: (i, e, 0, 0))
    return pl.pallas_call(
        _moe_ffn_kernel,
        grid=(ne, b // bg),
        in_specs=[tok, pl.BlockSpec((bg, 1, cap, 1), lambda e, i: (i, e, 0, 0)),
                  pl.BlockSpec((1, 1, d, ff), lambda e, i: (layer, e, 0, 0)),
                  pl.BlockSpec((1, 1, d, ff), lambda e, i: (layer, e, 0, 0)),
                  pl.BlockSpec((1, 1, ff, d), lambda e, i: (layer, e, 0, 0))],
        out_specs=tok,
        out_shape=jax.ShapeDtypeStruct((b, ne, cap, d), BF16),
        scratch_shapes=[pltpu.VMEM((d, ff), BF16), pltpu.VMEM((d, ff), BF16), pltpu.VMEM((ff, d), BF16)],
        compiler_params=_cparams(2),
        name="moe_ffn",
    )(xg, gates, wg, wu, wd)


def _moe_combine_kernel(cnt_ref, x_ref, pos_ref, y_ref, mod_ref, o_ref, *, cap, n_cnt):
    pos = pos_ref[0]
    tn, ne = pos.shape
    win = min(GATHER_WINDOW, cap)
    per = COMBINE_EXPERTS_PER_MATMUL
    gate = mod_ref[0, 0][5:6]
    row = (pl.program_id(0) * ne) * n_cnt + pl.program_id(1)
    lane = lax.broadcasted_iota(jnp.int32, (tn, per * win), 1)
    acc = jnp.zeros(x_ref.shape[1:], F32)
    overflow = False
    for g in range(ne // per):
        onehot = jnp.zeros((tn, per * win), F32)
        parts = []
        for j in range(per):
            e = g * per + j
            lo, hi = cnt_ref[row + e * n_cnt], cnt_ref[row + e * n_cnt + 1]
            start = pl.multiple_of(jnp.minimum((lo // BF16_ROWS) * BF16_ROWS, cap - win), BF16_ROWS)
            overflow = jnp.logical_or(overflow, hi > start + win)
            rel = pos[:, e:e + 1] - start
            target = jnp.where(jnp.logical_and(rel >= 0, rel < win), rel + j * win, -1)
            onehot = jnp.where(target == lane, 1.0, onehot)
            parts.append(y_ref[0, e, pl.ds(start, win), :])
        acc = acc + _dot(onehot.astype(BF16), jnp.concatenate(parts, axis=0))
    o_ref[0] = x_ref[0] + gate * acc

    @pl.when(overflow)
    def _():
        lane_all = lax.broadcasted_iota(jnp.int32, (tn, cap), 1)
        full = jnp.zeros(x_ref.shape[1:], F32)
        for e in range(ne):
            onehot = jnp.where(pos[:, e:e + 1] == lane_all, 1.0, 0.0).astype(BF16)
            full = full + _dot(onehot, y_ref[0, e])
        o_ref[0] = x_ref[0] + gate * full


def _moe_combine(x, pos_t, cnt, y, mod, tile_off, n_set, mod_sel, out_rows):
    b, t, d = x.shape
    _, ne, cap, _ = y.shape
    tn = PREFIX_CHUNK
    n_cnt = cnt.shape[-1]
    alias = out_rows == t
    out_off = tile_off if alias else 0
    return pl.pallas_call(
        functools.partial(_moe_combine_kernel, cap=cap, n_cnt=n_cnt),
        grid_spec=pltpu.PrefetchScalarGridSpec(
            num_scalar_prefetch=1,
            grid=(b, n_set // tn),
            in_specs=[pl.BlockSpec((1, tn, d), lambda bb, i, c: (bb, i + tile_off, 0)),
                      pl.BlockSpec((1, tn, ne), lambda bb, i, c: (bb, i, 0)),
                      pl.BlockSpec((1, ne, cap, d), lambda bb, i, c: (bb, 0, 0, 0)),
                      pl.BlockSpec((1, 1, 6, d), lambda bb, i, c: (bb, mod_sel, 0, 0))],
            out_specs=pl.BlockSpec((1, tn, d), lambda bb, i, c: (bb, i + out_off, 0))),
        out_shape=jax.ShapeDtypeStruct((b, out_rows, d), F32),
        input_output_aliases={1: 0} if alias else {},
        compiler_params=_cparams(2),
        name="moe_combine",
    )(cnt.reshape(-1), x, pos_t, y, mod)


def _moe(x, h, aff, mod, wg, wu, wd, layer, n_lat, with_ctx, final):
    b, t, d = x.shape
    n_ctx = t - n_lat
    aff_t = jnp.swapaxes(aff, 1, 2)
    sets = [(0, n_lat, 0, 2 if b % 2 == 0 else 1)]
    if with_ctx:
        sets.append((n_lat, n_ctx, 1, b))
    for start, n_set, mod_sel, bg in sets:
        cap = CAPACITY_FACTOR * n_set // N_EXPERTS
        a_set = aff_t[:, :, start:start + n_set]
        pos, cnt = _topk(a_set, cap)
        xg, gates = _moe_gather(h, pos, a_set, cnt, start // n_set, n_set, cap)
        y = _moe_ffn(xg, gates, wg, wu, wd, layer, bg)
        x = _moe_combine(x, jnp.swapaxes(pos, 1, 2), cnt, y, mod, start // PREFIX_CHUNK, n_set, mod_sel,
                         n_lat if final else t)
    return x


def _rope_tables(n_lat, n_ctx, rot_dim, starts):
    n_rows = n_lat // GRID_W
    rows = np.repeat(np.arange(n_rows, dtype=np.float32), GRID_W)
    cols = np.tile(np.arange(GRID_W, dtype=np.float32), n_rows)
    n_freq = rot_dim // 4
    inv_freq = (np.float32(ROPE_BASE) ** (-np.arange(n_freq, dtype=np.float32) / np.float32(n_freq))).astype(np.float32)
    ang = np.concatenate([rows[:, None] * inv_freq, cols[:, None] * inv_freq], axis=-1)
    half = rot_dim // 2
    cos_l, sin_l = np.cos(ang).astype(np.float32), np.sin(ang).astype(np.float32)
    cos = np.ones((n_lat + n_ctx, LANE), np.float32)
    sa = np.zeros((n_lat + n_ctx, LANE), np.float32)
    sb = np.zeros((n_lat + n_ctx, LANE), np.float32)
    for s in starts:
        cos[:n_lat, s:s + half] = cos_l
        cos[:n_lat, s + half:s + rot_dim] = cos_l
        sa[:n_lat, s:s + half] = -sin_l
        sb[:n_lat, s + half:s + rot_dim] = sin_l
    return jnp.asarray(cos), jnp.asarray(sa), jnp.asarray(sb)


def _pad_heads(w, heads, width):
    k = w.shape[0]
    w = w.reshape(k, heads, width)
    return jnp.pad(w, ((0, 0), (0, 0), (0, LANE - width))).reshape(k, heads * LANE)


def _mla_partner(a):
    half = MLA_ROPE // 2
    lane = jnp.arange(LANE)
    first = (lane >= MLA_NOPE) & (lane < MLA_NOPE + half)
    second = (lane >= MLA_NOPE + half) & (lane < MLA_QK)
    return jnp.where(first, jnp.roll(a, -half, axis=-1), jnp.where(second, jnp.roll(a, half, axis=-1), 0.0))


def _mla_weights(w_in, q_norm_g, w_uq, kv_norm_g, w_ukv):
    d = w_in.shape[0]
    lat = MLA_Q_LORA + MLA_KV_LORA
    rope_block = jnp.concatenate([jnp.zeros((d, MLA_NOPE), F32), w_in[:, lat:],
                                  jnp.zeros((d, LANE - MLA_QK), F32)], axis=1)
    ukv = w_ukv.reshape(MLA_KV_LORA, MLA_HEADS, MLA_NOPE + MLA_V)
    wuq = _pad_heads(w_uq, MLA_HEADS, MLA_QK)
    wuqp = _mla_partner(wuq.reshape(-1, MLA_HEADS, LANE))[:, :, MLA_NOPE:MLA_QK].reshape(wuq.shape[0], -1)
    return {
        "win": jnp.concatenate([w_in[:, :lat], rope_block, _mla_partner(rope_block)], axis=1).astype(BF16),
        "qng": q_norm_g.reshape(1, -1),
        "wuq": wuq.astype(BF16),
        "wuqp": wuqp.astype(BF16),
        "kvng": kv_norm_g.reshape(1, -1),
        "wuk": _pad_heads(ukv[:, :, :MLA_NOPE].reshape(MLA_KV_LORA, -1), MLA_HEADS, MLA_NOPE).astype(BF16),
        "wuv": _pad_heads(ukv[:, :, MLA_NOPE:].reshape(MLA_KV_LORA, -1), MLA_HEADS, MLA_V).astype(BF16),
    }


def _diff_partner(a):
    half = DIFF_HEAD_DIM // 2
    shape = a.shape
    a = a.reshape(shape[:-1] + (shape[-1] // LANE, LANE))
    first = (jnp.arange(LANE) % DIFF_HEAD_DIM) < half
    return jnp.where(first, jnp.roll(a, -half, axis=-1), jnp.roll(a, half, axis=-1)).reshape(shape)


def _diff_weights(w_in):
    d = w_in.shape[0]
    return jnp.concatenate([w_in, _diff_partner(w_in[:, :2 * d])], axis=1).astype(BF16)


def _diff_tables(tabs, qn_g, kn_g):
    cos, sa, sb = tabs
    out = []
    for g, scale in ((qn_g, DIFF_HEAD_DIM ** -0.5 * LOG2E), (kn_g, 1.0)):
        g2 = jnp.tile(g, 2)
        out += [cos * g2 * scale, (sa + sb) * _diff_partner(g2) * scale]
    return out


def _mla_tables(tabs, qn_g, kn_g):
    cos, sa, sb = tabs
    out = []
    for g, scale in ((qn_g, MLA_QK ** -0.5 * LOG2E), (kn_g, 1.0)):
        gp = jnp.pad(g, (0, LANE - MLA_QK))
        out += [cos * gp * scale, (sa + sb) * _mla_partner(gp) * scale]
    return out


def kernel(x, c, ctx, c_ctx, ada_w, ada_b, norm_mix_g, norm_ffn_g, mla_w_in, mla_q_norm_g, mla_w_uq, mla_kv_norm_g, mla_w_ukv, mla_qn_g, mla_kn_g, mla_w_out, diff_w_in, diff_qn_g, diff_kn_g, diff_lambda_q1, diff_lambda_k1, diff_lambda_q2, diff_lambda_k2, diff_sub_g, diff_w_out, sg_w_in, sg_ln_g, sg_ln_b, sg_w_s, sg_b_s, sg_w_out, moe_router, moe_w_gate, moe_w_up, moe_w_down):
    b, n_lat, d = x.shape
    n_ctx = ctx.shape[1]
    depth = ada_w.shape[0]
    assert n_lat % ROW_TILE == 0 and n_ctx == ROW_TILE and n_lat % n_ctx == 0

    rows = -(-(b + 1) // 8) * 8
    cc = jnp.concatenate([c, c_ctx[None], jnp.zeros((rows - b - 1, d), F32)], axis=0)
    mods = _ada(cc, ada_w, ada_b).reshape(depth, rows, 6, d)
    xs = (x, ctx) if depth > 1 else jnp.concatenate([x, ctx], axis=1)
    t = n_lat + n_ctx

    tabs_a = _rope_tables(n_lat, n_ctx, MLA_ROPE, (MLA_NOPE,))
    tabs_b = _rope_tables(n_lat, n_ctx, DIFF_HEAD_DIM, (0, DIFF_HEAD_DIM))

    for i in range(depth):
        kind, j = i % N_MIXERS, i // N_MIXERS
        last = i == depth - 1
        with_ctx = not last
        q_tiles = (n_lat + (n_ctx if with_ctx else 0)) // ROW_TILE
        mod = jnp.stack([mods[i, :b], jnp.broadcast_to(mods[i, b], (b, 6, d))], axis=1)
        g_mix, g_ffn = norm_mix_g[i].reshape(1, d), norm_ffn_g[i].reshape(1, d)
        if kind == 0:
            w = _mla_weights(mla_w_in[j], mla_q_norm_g[j], mla_w_uq[j], mla_kv_norm_g[j], mla_w_ukv[j])
            q, k, v = _mla_in(xs, g_mix, mod, w, _mla_tables(tabs_a, mla_qn_g[j], mla_kn_g[j]), n_lat, t)
            a = _attn_calls(_mla_attn_kernel, "mla_attn", [q], [k, v], [], n_lat, with_ctx,
                            MLA_HEADS, MLA_HEADS_PER_STEP, MLA_HEADS_PER_STEP * MLA_V)
            xs, h, aff = _out_proj(xs, a, mla_w_out[j].astype(BF16), mod, g_ffn, moe_router[i], n_lat, q_tiles)
        elif kind == 1:
            lam_init = 0.8 - 0.6 * math.exp(-0.3 * i)
            q0, q1, k, v = _diff_in(xs, g_mix, mod, _diff_weights(diff_w_in[j]),
                                    _diff_tables(tabs_b, diff_qn_g[j], diff_kn_g[j]), n_lat)
            wa = [a.reshape(1, -1) for a in (diff_lambda_q1[j], diff_lambda_k1[j], diff_lambda_q2[j],
                                             diff_lambda_k2[j], diff_sub_g[j])]
            a = _attn_calls(functools.partial(_diff_attn_kernel, lam_init=lam_init), "diff_attn", [q0, q1], [k, v],
                            wa, n_lat, with_ctx, DIFF_HEADS, DIFF_HEADS_PER_STEP, DIFF_HEADS_PER_STEP * LANE)
            xs, h, aff = _out_proj(xs, a, diff_w_out[j].astype(BF16), mod, g_ffn, moe_router[i], n_lat, q_tiles)
        else:
            w = {"win": sg_w_in[j].astype(BF16), "lng": sg_ln_g[j].reshape(1, -1), "lnb": sg_ln_b[j].reshape(1, -1),
                 "ws": sg_w_s[j].astype(BF16), "bs": sg_b_s[j].T, "wout": sg_w_out[j].astype(BF16)}
            xs, h, aff = _sg(xs, g_mix, mod, w, g_ffn, moe_router[i], n_lat)
        xs = _moe(xs, h, aff, mod, moe_w_gate, moe_w_up, moe_w_down, i, n_lat, with_ctx, last)
    return xs
```
